```python
import math
import jax, jax.numpy as jnp
from jax import lax
import numpy as np

D_MODEL = 1024
BATCH = 8
SEQ = 2048
DEPTH = 2

HEAD_DIM = 64
BLOCK = 128
A_Q_HEADS = 8
A_KV_HEADS = 2
A_WINDOW = 128
B_PATTERNS = ((128, 1), (512, 4), (2048, 16))
B_HEADS = 8
C_HEADS = 4
C_DK = 64
C_DV = 128
C_GATE_RANK = 16
C_GATE_TAU = 16.0
C_CHUNK = 64
BRANCH_WIDTH = 512
N_BRANCHES = 3
N_BUCKETS = 32
BUCKET_MAX_DIST = 2048
N_BIAS_HEADS = A_Q_HEADS + len(B_PATTERNS) * B_HEADS
PEER_HEADS = 8
PEER_KEYS = 128
PEER_N_EXPERTS = PEER_KEYS * PEER_KEYS
PEER_TOPK = 16
PEER_DKEY = 256
PEER_TOKEN_BLOCK = 128
RMS_EPS = 1e-6
NEG_INF = -1e30

IN_SIZES = (
    A_Q_HEADS * HEAD_DIM, A_KV_HEADS * HEAD_DIM, A_KV_HEADS * HEAD_DIM,
    B_HEADS * HEAD_DIM, B_HEADS * HEAD_DIM, B_HEADS * HEAD_DIM,
    B_HEADS * HEAD_DIM, B_HEADS * HEAD_DIM, B_HEADS * HEAD_DIM,
    B_HEADS * HEAD_DIM, B_HEADS * HEAD_DIM, B_HEADS * HEAD_DIM,
    C_HEADS * C_DK, C_HEADS * C_DK, C_HEADS * C_DV, C_GATE_RANK, C_HEADS * C_DV,
    N_BRANCHES * D_MODEL,
)
IN_WIDTH = sum(IN_SIZES)
SPLIT_POINTS = tuple(int(v) for v in np.cumsum(IN_SIZES)[:-1])

kernel_name = 'hybrid_swa_dilated_gla_peer_block'


def rmsnorm(x, g):
    xf = x.astype(jnp.float32)
    y = xf * lax.rsqrt(jnp.mean(xf * xf, axis=-1, keepdims=True) + RMS_EPS)
    return (y * g.astype(jnp.float32)).astype(x.dtype)


def t5_bucket(dist):
    max_exact = N_BUCKETS // 2
    large = max_exact + (jnp.log(jnp.maximum(dist, 1).astype(jnp.float32) / max_exact)
                         / math.log(BUCKET_MAX_DIST / max_exact) * (N_BUCKETS - max_exact)).astype(jnp.int32)
    large = jnp.minimum(large, N_BUCKETS - 1)
    return jnp.where(dist < max_exact, dist, large)


def rel_bias(table_cols, dilation):
    qi = jnp.arange(BLOCK)[:, None]
    kj = jnp.arange(2 * BLOCK)[None, :]
    dist = jnp.maximum(qi + BLOCK - kj, 0) * dilation
    return jnp.moveaxis(table_cols[t5_bucket(dist)], -1, 0).astype(jnp.float32)


def banded_attention(q, k, v, bias, max_dist, sink):
    n, length, hkv, grp, hd = q.shape
    nb = -(-length // BLOCK)
    pad = nb * BLOCK - length
    qb = jnp.pad(q, ((0, 0), (0, pad), (0, 0), (0, 0), (0, 0))).reshape(n, nb, BLOCK, hkv, grp, hd)

    def band(t):
        t = jnp.pad(t, ((0, 0), (BLOCK, pad), (0, 0), (0, 0))).reshape(n, nb + 1, BLOCK, hkv, hd)
        return jnp.concatenate([t[:, :-1], t[:, 1:]], axis=2)

    kb, vb = band(k), band(v)
    s = jnp.einsum('nbqhgd,nbkhd->nhgbqk', qb, kb, preferred_element_type=jnp.float32) * (hd ** -0.5)
    qi = jnp.arange(BLOCK)[:, None]
    kj = jnp.arange(2 * BLOCK)[None, :]
    dist = qi + BLOCK - kj
    in_window = (dist >= 0) & (dist <= max_dist)
    key_ok = (jnp.arange(nb)[:, None] * BLOCK - BLOCK + kj) >= 0
    mask = in_window[None] & key_ok[:, None, :]
    s = jnp.where(mask, s + bias[None, :, :, None], NEG_INF)
    m = s.max(-1)
    if sink is not None:
        sk = sink.astype(jnp.float32)[None, :, :, None, None]
        m = jnp.maximum(m, sk)
    p = jnp.exp(s - m[..., None])
    denom = p.sum(-1)
    if sink is not None:
        denom = denom + jnp.exp(sk - m)
    o = jnp.einsum('nhgbqk,nbkhd->nbqhgd', p.astype(v.dtype), vb, preferred_element_type=jnp.float32)
    o = o / jnp.transpose(denom, (0, 3, 4, 1, 2))[..., None]
    o = o.reshape(n, nb * BLOCK, hkv, grp, hd)[:, :length]
    lse = jnp.transpose(m + jnp.log(denom), (0, 3, 4, 1, 2)).reshape(n, nb * BLOCK, hkv, grp)[:, :length]
    return o, lse


def sliding_window_gqa(q, k, v, bias, sink):
    b, s, _ = q.shape
    grp = A_Q_HEADS // A_KV_HEADS
    o, _ = banded_attention(q.reshape(b, s, A_KV_HEADS, grp, HEAD_DIM),
                            k.reshape(b, s, A_KV_HEADS, HEAD_DIM),
                            v.reshape(b, s, A_KV_HEADS, HEAD_DIM),
                            bias, A_WINDOW - 1, sink.reshape(A_KV_HEADS, grp))
    return o.reshape(b, s, A_Q_HEADS * HEAD_DIM)


def dilated_attention(groups, biases):
    outs, lses = [], []
    for (window, dil), (q, k, v), bias in zip(B_PATTERNS, groups, biases):
        b, s, _ = q.shape
        sub = s // dil

        def to_sub(t, extra):
            t = t.reshape(b, sub, dil, B_HEADS, HEAD_DIM).transpose(0, 2, 1, 3, 4)
            return t.reshape((b * dil, sub, B_HEADS) + extra + (HEAD_DIM,))

        o, lse = banded_attention(to_sub(q, (1,)), to_sub(k, ()), to_sub(v, ()), bias, window // dil, None)
        outs.append(o.reshape(b, dil, sub, B_HEADS, HEAD_DIM).transpose(0, 2, 1, 3, 4).reshape(b, s, B_HEADS, HEAD_DIM))
        lses.append(lse.reshape(b, dil, sub, B_HEADS).transpose(0, 2, 1, 3).reshape(b, s, B_HEADS))
    w = jax.nn.softmax(jnp.stack(lses), axis=0)
    o = jnp.einsum('pbsh,pbshd->bshd', w, jnp.stack(outs))
    return o.reshape(o.shape[0], o.shape[1], B_HEADS * HEAD_DIM)


def gla_branch(q, k, v, z, r, head_norm):
    b, s, _ = q.shape
    nc = s // C_CHUNK

    def chunks(t, d):
        return t.reshape(b, nc, C_CHUNK, C_HEADS, d)

    q = chunks(q, C_DK) * (C_DK ** -0.5)
    k = chunks(k, C_DK)
    v = chunks(v, C_DV)
    log_a = chunks(jax.nn.log_sigmoid(z.astype(jnp.float32)) / C_GATE_TAU, C_DK)
    cum = jnp.cumsum(log_a, axis=2)
    last = cum[:, :, -1:]
    q_dec = q * jnp.exp(cum)
    k_inv = k * jnp.exp(-cum)
    k_end = k * jnp.exp(last - cum)
    causal = jnp.tril(jnp.ones((C_CHUNK, C_CHUNK), dtype=bool))
    att = jnp.where(causal, jnp.einsum('bnthd,bnshd->bnhts', q_dec, k_inv), 0.0)
    o_intra = jnp.einsum('bnhts,bnshv->bnthv', att, v)
    kv = jnp.einsum('bnshd,bnshv->nbhdv', k_end, v)
    decay = jnp.exp(jnp.moveaxis(last[:, :, 0], 1, 0))

    def step(state, inp):
        kv_n, dec_n = inp
        return state * dec_n[..., None] + kv_n, state

    _, s_prev = lax.scan(step, jnp.zeros((b, C_HEADS, C_DK, C_DV), jnp.float32), (kv, decay))
    o_inter = jnp.einsum('bnthd,nbhdv->bnthv', q_dec, s_prev)
    o = rmsnorm((o_intra + o_inter).reshape(b, s, C_HEADS, C_DV), head_norm)
    o = o * jax.nn.silu(r.astype(jnp.float32)).reshape(b, s, C_HEADS, C_DV)
    return o.reshape(b, s, C_HEADS * C_DV)


def peer_ffn(x, w_q, sub_keys, u_table, v_table):
    b, s, d = x.shape
    xt = x.reshape(-1, PEER_TOKEN_BLOCK, d)

    def block(xb):
        qy = (xb @ w_q).reshape(PEER_TOKEN_BLOCK, PEER_HEADS, 2, PEER_DKEY // 2)
        sc = jnp.einsum('thpc,hpnc->thpn', qy, sub_keys)
        s1, i1 = lax.top_k(sc[:, :, 0], PEER_TOPK)
        s2, i2 = lax.top_k(sc[:, :, 1], PEER_TOPK)
        cand_s = (s1[..., :, None] + s2[..., None, :]).reshape(PEER_TOKEN_BLOCK, PEER_HEADS, PEER_TOPK * PEER_TOPK)
        cand_i = (i1[..., :, None] * PEER_KEYS + i2[..., None, :]).reshape(PEER_TOKEN_BLOCK, PEER_HEADS, PEER_TOPK * PEER_TOPK)
        top_s, pos = lax.top_k(cand_s, PEER_TOPK)
        idx = jnp.take_along_axis(cand_i, pos, axis=-1)
        gate = jax.nn.softmax(top_s.astype(jnp.float32), axis=-1)
        hid = jax.nn.gelu(jnp.einsum('thkd,td->thk', u_table[idx], xb))
        return jnp.einsum('thk,thkd->td', (gate * hid).astype(xb.dtype), v_table[idx])

    return lax.map(block, xt).reshape(b, s, d)


def setup_inputs(seed: int = 0) -> dict:
    key = jax.random.key(seed)
    ks = jax.random.split(key, 16)

    def nrm(k, shape, scale):
        return jax.random.normal(k, shape, jnp.float32) * scale

    return {
        'x': nrm(ks[0], (BATCH, SEQ, D_MODEL), 1.0),
        'w_in': nrm(ks[1], (DEPTH, D_MODEL, IN_WIDTH), D_MODEL ** -0.5),
        'attn_sinks': nrm(ks[2], (DEPTH, A_Q_HEADS), 0.5),
        'gla_alpha_w': nrm(ks[3], (DEPTH, C_GATE_RANK, C_HEADS * C_DK), C_GATE_RANK ** -0.5),
        'gla_alpha_b': nrm(ks[4], (DEPTH, C_HEADS * C_DK), 0.1),
        'gla_head_norm': 1.0 + nrm(ks[5], (DEPTH, C_DV), 0.02),
        'w_branch': nrm(ks[6], (DEPTH, N_BRANCHES, BRANCH_WIDTH, D_MODEL), BRANCH_WIDTH ** -0.5),
        'w_out': nrm(ks[7], (DEPTH, D_MODEL, D_MODEL), D_MODEL ** -0.5),
        'norm_mix': 1.0 + nrm(ks[8], (DEPTH, D_MODEL), 0.02),
        'norm_ffn': 1.0 + nrm(ks[9], (DEPTH, D_MODEL), 0.02),
        'peer_wq': nrm(ks[10], (DEPTH, D_MODEL, PEER_HEADS * PEER_DKEY), D_MODEL ** -0.5),
        'peer_keys': nrm(ks[11], (DEPTH, PEER_HEADS, 2, PEER_KEYS, PEER_DKEY // 2), (PEER_DKEY // 2) ** -0.5),
        'peer_u': nrm(ks[12], (DEPTH, PEER_N_EXPERTS, D_MODEL), D_MODEL ** -0.5),
        'peer_v': nrm(ks[13], (DEPTH, PEER_N_EXPERTS, D_MODEL), PEER_TOPK ** -0.5),
        'rel_bias_table': nrm(ks[14], (N_BUCKETS, N_BIAS_HEADS), 0.5),
        'norm_final': 1.0 + nrm(ks[15], (D_MODEL,), 0.02),
    }


def reference(x, w_in, attn_sinks, gla_alpha_w, gla_alpha_b, gla_head_norm, w_branch, w_out,
              norm_mix, norm_ffn, peer_wq, peer_keys, peer_u, peer_v, rel_bias_table, norm_final):
    b, s, _ = x.shape
    bias_a = rel_bias(rel_bias_table[:, :A_Q_HEADS], 1).reshape(
        A_KV_HEADS, A_Q_HEADS // A_KV_HEADS, BLOCK, 2 * BLOCK)
    bias_b = [rel_bias(rel_bias_table[:, A_Q_HEADS + i * B_HEADS:A_Q_HEADS + (i + 1) * B_HEADS], dil).reshape(
        B_HEADS, 1, BLOCK, 2 * BLOCK) for i, (_, dil) in enumerate(B_PATTERNS)]
    for l in range(DEPTH):
        h = rmsnorm(x, norm_mix[l])
        parts = jnp.split(h @ w_in[l], SPLIT_POINTS, axis=-1)
        qa, ka, va = parts[0], parts[1], parts[2]
        b_groups = [(parts[3 + 3 * i], parts[4 + 3 * i], parts[5 + 3 * i]) for i in range(len(B_PATTERNS))]
        qc, kc, vc, alr, rc, gates = parts[12], parts[13], parts[14], parts[15], parts[16], parts[17]
        o_a = sliding_window_gqa(qa, ka, va, bias_a, attn_sinks[l])
        o_b = dilated_attention(b_groups, bias_b)
        o_c = gla_branch(qc, kc, vc, alr @ gla_alpha_w[l] + gla_alpha_b[l], rc, gla_head_norm[l])
        branches = jnp.stack([o_a, o_b, o_c], axis=2).astype(x.dtype)
        proj = jnp.einsum('bsnc,ncd->bsnd', branches, w_branch[l])
        merged = (jax.nn.sigmoid(gates.reshape(b, s, N_BRANCHES, D_MODEL)) * proj).sum(axis=2)
        x = x + merged @ w_out[l]
        x = x + peer_ffn(rmsnorm(x, norm_ffn[l]), peer_wq[l], peer_keys[l], peer_u[l], peer_v[l])
    return rmsnorm(x, norm_final)
```

```python
import functools
import math

import numpy as np
import jax
import jax.numpy as jnp
from jax import lax
from jax.experimental import pallas as pl
from jax.experimental.pallas import tpu as pltpu

F32 = jnp.float32
BF16 = jnp.bfloat16

D_MODEL = 1024
HEAD_DIM = 64
BLOCK = 128
A_Q_HEADS = 8
A_KV_HEADS = 2
A_WINDOW = 128
B_PATTERNS = ((128, 1), (512, 4), (2048, 16))
B_HEADS = 8
C_HEADS = 4
C_DK = 64
C_DV = 128
C_GATE_RANK = 16
C_GATE_TAU = 16.0
C_CHUNK = 64
BRANCH_WIDTH = 512
N_BRANCHES = 3
N_BUCKETS = 32
BUCKET_MAX_DIST = 2048
PEER_HEADS = 8
PEER_KEYS = 128
PEER_N_EXPERTS = PEER_KEYS * PEER_KEYS
PEER_TOPK = 16
PEER_DKEY = 256
RMS_EPS = 1e-6
NEG_INF = -1e30

LANES = 128
VMEM_LIMIT = 56 * 1024 * 1024

A_WIDTH = (A_Q_HEADS + 2 * A_KV_HEADS) * HEAD_DIM
B_WIDTH = len(B_PATTERNS) * 3 * B_HEADS * HEAD_DIM
C_WIDTH = 2 * C_HEADS * C_DK + C_HEADS * C_DV
G_GATES = N_BRANCHES * D_MODEL
G_WIDTH = G_GATES + C_HEADS * C_DV + 2 * LANES
G_R_COL = G_GATES
G_ALR_COL = G_GATES + C_HEADS * C_DV


def _cparams(sem):
    return pltpu.CompilerParams(dimension_semantics=sem, vmem_limit_bytes=VMEM_LIMIT)


def _norm_matmul_kernel(x_ref, g_ref, w_ref, o_ref, h_ref):
    @pl.when(pl.program_id(1) == 0)
    def _():
        x = x_ref[...]
        ms = jnp.mean(x * x, axis=-1, keepdims=True)
        h_ref[...] = (x * lax.rsqrt(ms + RMS_EPS) * g_ref[...]).astype(BF16)

    o_ref[...] = jnp.dot(h_ref[...], w_ref[...], preferred_element_type=F32).astype(o_ref.dtype)


def _norm_matmul(x2, g, w, tn, out_dtype, name):
    rows, d = x2.shape
    n = w.shape[1]
    tm = min(1024, rows)
    return pl.pallas_call(
        _norm_matmul_kernel,
        grid=(rows // tm, n // tn),
        in_specs=[
            pl.BlockSpec((tm, d), lambda i, j: (i, 0)),
            pl.BlockSpec((1, d), lambda i, j: (0, 0)),
            pl.BlockSpec((d, tn), lambda i, j: (0, j)),
        ],
        out_specs=pl.BlockSpec((tm, tn), lambda i, j: (i, j)),
        out_shape=jax.ShapeDtypeStruct((rows, n), out_dtype),
        scratch_shapes=[pltpu.VMEM((tm, d), BF16)],
        compiler_params=_cparams(("parallel", "arbitrary")),
        name=name,
    )(x2, g.reshape(1, d), w)


def _attn_kernel(*refs, hq, grp, max_dist, has_sink, want_lse):
    q_ref, kp_ref, kc_ref, vp_ref, vc_ref, bias_ref = refs[:6]
    pos = 6
    sink_ref = None
    if has_sink:
        sink_ref = refs[pos]
        pos += 1
    o_ref = refs[pos]
    lse_ref = refs[pos + 1] if want_lse else None

    blk = pl.program_id(2)
    q = q_ref[0]
    k = jnp.concatenate([kp_ref[0], kc_ref[0]], axis=0)
    v = jnp.concatenate([vp_ref[0], vc_ref[0]], axis=0)
    qi = lax.broadcasted_iota(jnp.int32, (BLOCK, 2 * BLOCK), 0)
    kj = lax.broadcasted_iota(jnp.int32, (BLOCK, 2 * BLOCK), 1)
    dist = qi + BLOCK - kj
    first_key = jnp.where(blk > 0, 0, BLOCK)
    mask = (dist >= 0) & (dist <= max_dist) & (kj >= first_key)

    outs, lses = [], []
    for h in range(hq):
        kvh = h // grp
        qh = q[:, h * HEAD_DIM:(h + 1) * HEAD_DIM]
        kh = k[:, kvh * HEAD_DIM:(kvh + 1) * HEAD_DIM]
        vh = v[:, kvh * HEAD_DIM:(kvh + 1) * HEAD_DIM]
        s = lax.dot_general(qh, kh, (((1,), (1,)), ((), ())), preferred_element_type=F32)
        s = s * (HEAD_DIM ** -0.5)
        s = jnp.where(mask, s + bias_ref[h], NEG_INF)
        m = jnp.max(s, axis=-1, keepdims=True)
        if has_sink:
            m = jnp.maximum(m, sink_ref[h])
        p = jnp.exp(s - m)
        denom = jnp.sum(p, axis=-1, keepdims=True)
        if has_sink:
            denom = denom + jnp.exp(sink_ref[h] - m)
        o = jnp.dot(p.astype(BF16), vh, preferred_element_type=F32) / denom
        outs.append(o)
        if want_lse:
            lses.append(jnp.broadcast_to(m + jnp.log(denom), (BLOCK, HEAD_DIM)))
    o_ref[0] = jnp.concatenate(outs, axis=-1).astype(o_ref.dtype)
    if want_lse:
        lse_ref[0] = jnp.concatenate(lses, axis=-1)


def _banded_attention(qkv, bias, sink, *, batch, seq, dil, width, q_col, k_col, v_col,
                      hq, grp, max_dist, want_lse, out_dtype, name):
    sub = seq // dil
    nb = sub // BLOCK
    qw = hq * HEAD_DIM
    kw = (hq // grp) * HEAD_DIM
    view = qkv.reshape(batch, sub, dil * width)
    q_blk, k_blk, v_blk = q_col // qw, k_col // kw, v_col // kw
    q_per, k_per = width // qw, width // kw
    assert q_col % qw == 0 and k_col % kw == 0 and v_col % kw == 0
    assert dil == 1 or (width % qw == 0 and width % kw == 0)

    in_specs = [
        pl.BlockSpec((1, BLOCK, qw), lambda b, r, i: (b, i, r * q_per + q_blk)),
        pl.BlockSpec((1, BLOCK, kw), lambda b, r, i: (b, jnp.maximum(i - 1, 0), r * k_per + k_blk)),
        pl.BlockSpec((1, BLOCK, kw), lambda b, r, i: (b, i, r * k_per + k_blk)),
        pl.BlockSpec((1, BLOCK, kw), lambda b, r, i: (b, jnp.maximum(i - 1, 0), r * k_per + v_blk)),
        pl.BlockSpec((1, BLOCK, kw), lambda b, r, i: (b, i, r * k_per + v_blk)),
        pl.BlockSpec((hq, BLOCK, 2 * BLOCK), lambda b, r, i: (0, 0, 0)),
    ]
    args = [view, view, view, view, view, bias]
    if sink is not None:
        in_specs.append(pl.BlockSpec(memory_space=pltpu.SMEM))
        args.append(sink)
    o_spec = pl.BlockSpec((1, BLOCK, qw), lambda b, r, i: (b, i, r))
    o_shape = jax.ShapeDtypeStruct((batch, sub, dil * qw), out_dtype)
    out_specs, out_shape = o_spec, o_shape
    if want_lse:
        out_specs = [o_spec, pl.BlockSpec((1, BLOCK, qw), lambda b, r, i: (b, i, r))]
        out_shape = [o_shape, jax.ShapeDtypeStruct((batch, sub, dil * qw), F32)]
    res = pl.pallas_call(
        functools.partial(_attn_kernel, hq=hq, grp=grp, max_dist=max_dist,
                          has_sink=sink is not None, want_lse=want_lse),
        grid=(batch, dil, nb),
        in_specs=in_specs,
        out_specs=out_specs,
        out_shape=out_shape,
        compiler_params=_cparams(("parallel", "parallel", "arbitrary")),
        name=name,
    )(*args)
    if want_lse:
        return res[0].reshape(batch * seq, qw), res[1].reshape(batch * seq, qw)
    return res.reshape(batch * seq, qw)


def _gla_kernel(q_ref, k_ref, v_ref, alr_ref, r_ref, aw_ref, ab_ref, hn_ref, o_ref, st_ref):
    @pl.when(pl.program_id(1) == 0)
    def _():
        st_ref[...] = jnp.zeros_like(st_ref)

    ck = C_CHUNK
    z = jnp.dot(alr_ref[0].astype(BF16), aw_ref[...], preferred_element_type=F32) + ab_ref[...]
    log_a = (jnp.minimum(z, 0.0) - jnp.log1p(jnp.exp(-jnp.abs(z)))) * (1.0 / C_GATE_TAU)
    row = lax.broadcasted_iota(jnp.int32, log_a.shape, 0)
    cum = log_a
    shift = 1
    while shift < ck:
        cum = cum + jnp.where(row >= shift, pltpu.roll(cum, shift, 0), 0.0)
        shift *= 2
    last = cum[ck - 1:ck, :]
    q = q_ref[0].astype(F32) * (C_DK ** -0.5)
    k = k_ref[0].astype(F32)
    q_dec = (q * jnp.exp(cum)).astype(BF16)
    k_inv = (k * jnp.exp(-cum)).astype(BF16)
    k_end = (k * jnp.exp(last - cum)).astype(BF16)
    decay = jnp.exp(last)
    v = v_ref[0]
    r = r_ref[0]
    ti = lax.broadcasted_iota(jnp.int32, (ck, ck), 0)
    si = lax.broadcasted_iota(jnp.int32, (ck, ck), 1)
    causal = ti >= si
    nt = (((1,), (1,)), ((), ()))
    tn = (((0,), (0,)), ((), ()))
    outs = []
    for h in range(C_HEADS):
        ks = slice(h * C_DK, (h + 1) * C_DK)
        vs = slice(h * C_DV, (h + 1) * C_DV)
        qd, ki, ke, vh = q_dec[:, ks], k_inv[:, ks], k_end[:, ks], v[:, vs]
        att = jnp.where(causal, lax.dot_general(qd, ki, nt, preferred_element_type=F32), 0.0)
        st = st_ref[h]
        o = jnp.dot(att.astype(BF16), vh, preferred_element_type=F32)
        o = o + lax.dot_general(qd, st.astype(BF16), nt, preferred_element_type=F32)
        kv_t = lax.dot_general(vh, ke, tn, preferred_element_type=F32)
        st_ref[h] = st * decay[:, ks] + kv_t
        o = o * lax.rsqrt(jnp.mean(o * o, axis=-1, keepdims=True) + RMS_EPS) * hn_ref[...]
        rh = r[:, vs]
        outs.append(o * (rh * jax.nn.sigmoid(rh)))
    o_ref[0] = jnp.concatenate(outs, axis=-1).astype(o_ref.dtype)


def _gla(c_qkv, g_all, aw, ab, hn, *, batch, seq):
    nc = seq // C_CHUNK
    qk_w = C_HEADS * C_DK
    v_w = C_HEADS * C_DV
    cv = c_qkv.reshape(batch, seq, C_WIDTH)
    gv = g_all.reshape(batch, seq, G_WIDTH)
    return pl.pallas_call(
        _gla_kernel,
        grid=(batch, nc),
        in_specs=[
            pl.BlockSpec((1, C_CHUNK, qk_w), lambda b, c: (b, c, 0)),
            pl.BlockSpec((1, C_CHUNK, qk_w), lambda b, c: (b, c, 1)),
            pl.BlockSpec((1, C_CHUNK, v_w), lambda b, c: (b, c, 1)),
            pl.BlockSpec((1, C_CHUNK, LANES), lambda b, c: (b, c, G_ALR_COL // LANES)),
            pl.BlockSpec((1, C_CHUNK, v_w), lambda b, c: (b, c, G_R_COL // v_w)),
            pl.BlockSpec((LANES, qk_w), lambda b, c: (0, 0)),
            pl.BlockSpec((1, qk_w), lambda b, c: (0, 0)),
            pl.BlockSpec((1, C_DV), lambda b, c: (0, 0)),
        ],
        out_specs=pl.BlockSpec((1, C_CHUNK, v_w), lambda b, c: (b, c, 0)),
        out_shape=jax.ShapeDtypeStruct((batch, seq, v_w), BF16),
        scratch_shapes=[pltpu.VMEM((C_HEADS, C_DV, C_DK), F32)],
        compiler_params=_cparams(("parallel", "arbitrary")),
        name="gla",
    )(cv, cv, cv, gv, gv, aw, ab, hn).reshape(batch * seq, v_w)


def _merge_kernel(oa_ref, ob1_ref, ob2_ref, ob3_ref, l1_ref, l2_ref, l3_ref, oc_ref, gates_ref,
                  x_ref, wb_ref, wo_ref, gn_ref, xo_ref, xnt_ref):
    l1, l2, l3 = l1_ref[...], l2_ref[...], l3_ref[...]
    lm = jnp.maximum(jnp.maximum(l1, l2), l3)
    e1, e2, e3 = jnp.exp(l1 - lm), jnp.exp(l2 - lm), jnp.exp(l3 - lm)
    ob = (e1 * ob1_ref[...] + e2 * ob2_ref[...] + e3 * ob3_ref[...]) / (e1 + e2 + e3)
    branches = (oa_ref[...], ob.astype(BF16), oc_ref[...])
    merged = None
    for n in range(N_BRANCHES):
        proj = jnp.dot(branches[n], wb_ref[n], preferred_element_type=F32)
        term = jax.nn.sigmoid(gates_ref[:, n * D_MODEL:(n + 1) * D_MODEL]) * proj
        merged = term if merged is None else merged + term
    x = x_ref[...] + jnp.dot(merged.astype(BF16), wo_ref[...], preferred_element_type=F32)
    xo_ref[...] = x
    xn = x * lax.rsqrt(jnp.mean(x * x, axis=-1, keepdims=True) + RMS_EPS) * gn_ref[...]
    xnt_ref[...] = xn.T.astype(BF16)


def _merge(o_a, o_b, lse_b, o_c, g_all, x2, wb, wo, gn):
    rows = x2.shape[0]
    tm = min(512, rows)
    row_spec = lambda w: pl.BlockSpec((tm, w), lambda i: (i, 0))
    return pl.pallas_call(
        _merge_kernel,
        grid=(rows // tm,),
        in_specs=[row_spec(BRANCH_WIDTH)] * 8 + [
            row_spec(G_GATES),
            row_spec(D_MODEL),
            pl.BlockSpec((N_BRANCHES, BRANCH_WIDTH, D_MODEL), lambda i: (0, 0, 0)),
            pl.BlockSpec((D_MODEL, D_MODEL), lambda i: (0, 0)),
            pl.BlockSpec((1, D_MODEL), lambda i: (0, 0)),
        ],
        out_specs=[row_spec(D_MODEL), pl.BlockSpec((D_MODEL, tm), lambda i: (0, i))],
        out_shape=[jax.ShapeDtypeStruct((rows, D_MODEL), F32),
                   jax.ShapeDtypeStruct((D_MODEL, rows), BF16)],
        compiler_params=_cparams(("parallel",)),
        name="merge",
    )(o_a, o_b[0], o_b[1], o_b[2], lse_b[0], lse_b[1], lse_b[2], o_c, g_all, x2, wb, wo,
      gn.reshape(1, D_MODEL))


_PAIRS = tuple((i, j) for i in range(PEER_TOPK) for j in range(PEER_TOPK)
               if (i + 1) * (j + 1) <= PEER_TOPK)


def _peer_select_kernel(xnt_ref, wq_ref, kb_ref, cnt_ref, e1_ref, rank_ref, e2_ref,
                        sc_ref, wk_ref, top_ref, rk_ref, eb_ref, *, tb):
    nk, nh, kk = PEER_KEYS, PEER_HEADS, PEER_TOPK
    xnt = xnt_ref[...]
    for p in range(2):
        qy = jnp.dot(wq_ref[p], xnt, preferred_element_type=F32).astype(BF16)
        sc_ref[p] = jnp.dot(kb_ref[p], qy, preferred_element_type=F32).reshape(nk, nh, tb)

    for c in range(tb // LANES):
        cs = slice(c * LANES, (c + 1) * LANES)
        for p in range(2):
            wk_ref[...] = sc_ref[p, :, :, cs]

            def extract(i, carry, p=p):
                w = wk_ref[...]
                m = jnp.max(w, axis=0)
                top_ref[p, i] = m
                wk_ref[...] = jnp.where(w == m[None], -jnp.inf, w)
                return carry

            lax.fori_loop(0, kk, extract, 0)
        s1s = [top_ref[0, i] for i in range(kk)]
        s2s = [top_ref[1, i] for i in range(kk)]
        cands = [s1s[i] + s2s[j] for (i, j) in _PAIRS]
        work = list(cands)
        tau = None
        for it in range(kk):
            tau = functools.reduce(jnp.maximum, work)
            if it + 1 < kk:
                work = [jnp.where(w == tau, -jnp.inf, w) for w in work]
        top = cands[0]
        zsum = functools.reduce(
            lambda a, b: a + b, [jnp.where(cd >= tau, jnp.exp(cd - top), 0.0) for cd in cands])
        inv_z = 1.0 / zsum

        s1 = sc_ref[0, :, :, cs]
        s2 = sc_ref[1, :, :, cs]
        cnt = jnp.zeros((nk, nh, LANES), F32)
        rank = jnp.zeros((nk, nh, LANES), F32)
        for j in range(kk):
            cnt = cnt + jnp.where(s1 + s2s[j][None] >= tau[None], 1.0, 0.0)
            rank = rank + jnp.where(s2s[j][None] > s2, 1.0, 0.0)
        cnt_ref[:, cs] = cnt.reshape(nk * nh, LANES)
        e1_ref[:, cs] = jnp.exp(s1 - s1s[0][None]).reshape(nk * nh, LANES)
        rk_ref[...] = rank.reshape(nk * nh, LANES)
        eb_ref[...] = (jnp.exp(s2 - s2s[0][None]) * inv_z[None]).reshape(nk * nh, LANES)
        for h in range(nh):
            rank_ref[h * nk:(h + 1) * nk, cs] = rk_ref[pl.ds(h, nk, stride=nh), :]
            e2_ref[h * nk:(h + 1) * nk, cs] = eb_ref[pl.ds(h, nk, stride=nh), :]


def _peer_select(xnt, wq_t, k_big):
    d, rows = xnt.shape
    tb = min(256, rows)
    nrow = PEER_KEYS * PEER_HEADS
    tab = jax.ShapeDtypeStruct((nrow, rows), F32)
    tab_spec = pl.BlockSpec((nrow, tb), lambda i: (0, i))
    return pl.pallas_call(
        functools.partial(_peer_select_kernel, tb=tb),
        grid=(rows // tb,),
        in_specs=[
            pl.BlockSpec((d, tb), lambda i: (0, i)),
            pl.BlockSpec((2, nrow, d), lambda i: (0, 0, 0)),
            pl.BlockSpec((2, nrow, nrow), lambda i: (0, 0, 0)),
        ],
        out_specs=[tab_spec] * 4,
        out_shape=[tab] * 4,
        scratch_shapes=[
            pltpu.VMEM((2, PEER_KEYS, PEER_HEADS, tb), F32),
            pltpu.VMEM((PEER_KEYS, PEER_HEADS, LANES), F32),
            pltpu.VMEM((2, PEER_TOPK, PEER_HEADS, LANES), F32),
            pltpu.VMEM((nrow, LANES), F32),
            pltpu.VMEM((nrow, LANES), F32),
        ],
        compiler_params=_cparams(("parallel",)),
        name="peer_select",
    )(xnt, wq_t, k_big)


def _gelu_tanh(x):
    c = math.sqrt(2.0 / math.pi)
    return 0.5 * x * (1.0 + jnp.tanh(c * (x + 0.044715 * (x * x * x))))


def _peer_dense_kernel(xnt_ref, u_ref, vt_ref, cnt_ref, e1_ref, rank_ref, e2_ref, x_ref, gn_ref,
                       o_ref, acc_ref, ht_ref, w_ref, *, tb, eb, final_norm):
    j = pl.program_id(1)
    nk, nh = PEER_KEYS, PEER_HEADS

    @pl.when(j == 0)
    def _():
        acc_ref[...] = jnp.zeros_like(acc_ref)

    ht_ref[...] = jnp.dot(u_ref[...], xnt_ref[...], preferred_element_type=F32)

    def slab(al, carry):
        r0 = pl.multiple_of(al * nk, nk)
        h0 = pl.multiple_of(al * nh, nh)
        cnt8 = cnt_ref[pl.ds(h0, nh), :]
        e18 = e1_ref[pl.ds(h0, nh), :]
        for c in range(tb // LANES):
            cs = slice(c * LANES, (c + 1) * LANES)
            act = _gelu_tanh(ht_ref[pl.ds(r0, nk), cs])
            g = jnp.zeros((nk, LANES), F32)
            for h in range(nh):
                sel = rank_ref[h * nk:(h + 1) * nk, cs] < cnt8[h:h + 1, cs]
                g = g + jnp.where(sel, e2_ref[h * nk:(h + 1) * nk, cs], 0.0) * e18[h:h + 1, cs]
            w_ref[pl.ds(r0, nk), cs] = (act * g).astype(BF16)
        return carry

    lax.fori_loop(0, eb // nk, slab, 0)
    acc_ref[...] += jnp.dot(vt_ref[...], w_ref[...], preferred_element_type=F32)

    @pl.when(j == pl.num_programs(1) - 1)
    def _():
        x = x_ref[...] + acc_ref[...].T
        if final_norm:
            x = x * lax.rsqrt(jnp.mean(x * x, axis=-1, keepdims=True) + RMS_EPS) * gn_ref[...]
        o_ref[...] = x


def _peer_dense(xnt, u, vt, cnt, e1, rank, e2, x2, gn, final_norm):
    d, rows = xnt.shape
    ne = u.shape[0]
    tb = min(512, rows)
    eb = 1024
    nrow = PEER_KEYS * PEER_HEADS
    arow = (eb // PEER_KEYS) * PEER_HEADS
    return pl.pallas_call(
        functools.partial(_peer_dense_kernel, tb=tb, eb=eb, final_norm=final_norm),
        grid=(rows // tb, ne // eb),
        in_specs=[
            pl.BlockSpec((d, tb), lambda i, j: (0, i)),
            pl.BlockSpec((eb, d), lambda i, j: (j, 0)),
            pl.BlockSpec((d, eb), lambda i, j: (0, j)),
            pl.BlockSpec((arow, tb), lambda i, j: (j, i)),
            pl.BlockSpec((arow, tb), lambda i, j: (j, i)),
            pl.BlockSpec((nrow, tb), lambda i, j: (0, i)),
            pl.BlockSpec((nrow, tb), lambda i, j: (0, i)),
            pl.BlockSpec((tb, d), lambda i, j: (i, 0)),
            pl.BlockSpec((1, d), lambda i, j: (0, 0)),
        ],
        out_specs=pl.BlockSpec((tb, d), lambda i, j: (i, 0)),
        out_shape=jax.ShapeDtypeStruct((rows, d), F32),
        scratch_shapes=[
            pltpu.VMEM((d, tb), F32),
            pltpu.VMEM((eb, tb), F32),
            pltpu.VMEM((eb, tb), BF16),
        ],
        compiler_params=_cparams(("parallel", "arbitrary")),
        name="peer_dense",
    )(xnt, u, vt, cnt, e1, rank, e2, x2, gn.reshape(1, d))


def _t5_bucket(dist):
    max_exact = N_BUCKETS // 2
    large = max_exact + (jnp.log(jnp.maximum(dist, 1).astype(F32) / max_exact)
                         / math.log(BUCKET_MAX_DIST / max_exact) * (N_BUCKETS - max_exact)).astype(jnp.int32)
    large = jnp.minimum(large, N_BUCKETS - 1)
    return jnp.where(dist < max_exact, dist, large)


def _rel_bias(table_cols, dilation):
    qi = jnp.arange(BLOCK)[:, None]
    kj = jnp.arange(2 * BLOCK)[None, :]
    dist = jnp.maximum(qi + BLOCK - kj, 0) * dilation
    return jnp.moveaxis(table_cols[_t5_bucket(dist)], -1, 0).astype(F32)


def _split_w_in(w):
    hd = HEAD_DIM
    sizes = [A_Q_HEADS * hd, A_KV_HEADS * hd, A_KV_HEADS * hd] + [B_HEADS * hd] * 9 + [
        C_HEADS * C_DK, C_HEADS * C_DK, C_HEADS * C_DV, C_GATE_RANK, C_HEADS * C_DV, N_BRANCHES * D_MODEL]
    offs = np.concatenate([[0], np.cumsum(sizes)])
    col = lambda a, b: w[:, int(offs[a]):int(offs[b])]
    w_a = col(0, 3)
    w_b = col(3, 12)
    w_c = col(12, 15)
    pad = jnp.zeros((w.shape[0], G_WIDTH - G_ALR_COL - C_GATE_RANK), w.dtype)
    w_g = jnp.concatenate([col(17, 18), col(16, 17), col(15, 16), pad], axis=1)
    return [t.astype(BF16) for t in (w_a, w_b, w_c, w_g)]


def _peer_key_matrix(keys):
    nh, _, nk, c = keys.shape
    eye = jnp.eye(nh, dtype=keys.dtype)
    big = jnp.einsum('hpnc,hg->pnhgc', keys, eye)
    return big.reshape(2, nk * nh, nh * c).astype(BF16)


def kernel(x, w_in, attn_sinks, gla_alpha_w, gla_alpha_b, gla_head_norm, w_branch, w_out, norm_mix,
           norm_ffn, peer_wq, peer_keys, peer_u, peer_v, rel_bias_table, norm_final):
    batch, seq, d = x.shape
    rows = batch * seq
    depth = w_in.shape[0]
    hw = B_HEADS * HEAD_DIM

    bias_a = _rel_bias(rel_bias_table[:, :A_Q_HEADS], 1)
    bias_b = [_rel_bias(rel_bias_table[:, A_Q_HEADS + i * B_HEADS:A_Q_HEADS + (i + 1) * B_HEADS], dil)
              for i, (_, dil) in enumerate(B_PATTERNS)]

    x2 = x.reshape(rows, d)
    for l in range(depth):
        w_a, w_b, w_c, w_g = _split_w_in(w_in[l])
        a_qkv = _norm_matmul(x2, norm_mix[l], w_a, 768, BF16, "proj_a")
        b_qkv = _norm_matmul(x2, norm_mix[l], w_b, 1152, BF16, "proj_b")
        c_qkv = _norm_matmul(x2, norm_mix[l], w_c, 1024, BF16, "proj_c")
        g_all = _norm_matmul(x2, norm_mix[l], w_g, 1280, F32, "proj_g")

        o_a = _banded_attention(
            a_qkv, bias_a, attn_sinks[l], batch=batch, seq=seq, dil=1, width=A_WIDTH,
            q_col=0, k_col=A_Q_HEADS * HEAD_DIM, v_col=(A_Q_HEADS + A_KV_HEADS) * HEAD_DIM,
            hq=A_Q_HEADS, grp=A_Q_HEADS // A_KV_HEADS, max_dist=A_WINDOW - 1,
            want_lse=False, out_dtype=BF16, name="attn_a")
        o_b, lse_b = [], []
        for gi, (window, dil) in enumerate(B_PATTERNS):
            o, lse = _banded_attention(
                b_qkv, bias_b[gi], None, batch=batch, seq=seq, dil=dil, width=B_WIDTH,
                q_col=gi * 3 * hw, k_col=gi * 3 * hw + hw, v_col=gi * 3 * hw + 2 * hw,
                hq=B_HEADS, grp=1, max_dist=window // dil,
                want_lse=True, out_dtype=F32, name="attn_b%d" % gi)
            o_b.append(o)
            lse_b.append(lse)

        aw = jnp.zeros((LANES, C_HEADS * C_DK), BF16).at[:C_GATE_RANK].set(gla_alpha_w[l].astype(BF16))
        o_c = _gla(c_qkv, g_all, aw, gla_alpha_b[l].reshape(1, -1), gla_head_norm[l].reshape(1, -1),
                   batch=batch, seq=seq)

        x2, xnt = _merge(o_a, o_b, lse_b, o_c, g_all, x2, w_branch[l].astype(BF16),
                         w_out[l].astype(BF16), norm_ffn[l])

        wq_t = peer_wq[l].reshape(d, PEER_HEADS, 2, PEER_DKEY // 2).transpose(2, 1, 3, 0)
        wq_t = wq_t.reshape(2, PEER_HEADS * (PEER_DKEY // 2), d).astype(BF16)
        cnt, e1, rank, e2 = _peer_select(xnt, wq_t, _peer_key_matrix(peer_keys[l]))
        last = l == depth - 1
        x2 = _peer_dense(xnt, peer_u[l].astype(BF16), peer_v[l].T.astype(BF16), cnt, e1, rank, e2,
                         x2, norm_final, final_norm=last)
    return x2.reshape(batch, seq, d)
```

```python
import functools
import math

import numpy as np
import jax
import jax.numpy as jnp
from jax import lax
from jax.experimental import pallas as pl
from jax.experimental.pallas import tpu as pltpu

F32 = jnp.float32
BF16 = jnp.bfloat16

D_MODEL = 1024
HEAD_DIM = 64
BLOCK = 128
A_Q_HEADS = 8
A_KV_HEADS = 2
A_WINDOW = 128
B_PATTERNS = ((128, 1), (512, 4), (2048, 16))
B_HEADS = 8
C_HEADS = 4
C_DK = 64
C_DV = 128
C_GATE_RANK = 16
C_GATE_TAU = 16.0
C_CHUNK = 64
BRANCH_WIDTH = 512
N_BRANCHES = 3
N_BUCKETS = 32
BUCKET_MAX_DIST = 2048
PEER_HEADS = 8
PEER_KEYS = 128
PEER_N_EXPERTS = PEER_KEYS * PEER_KEYS
PEER_TOPK = 16
PEER_DKEY = 256
RMS_EPS = 1e-6
NEG_INF = -1e30

LANES = 128
VMEM_LIMIT = 56 * 1024 * 1024

A_WIDTH = (A_Q_HEADS + 2 * A_KV_HEADS) * HEAD_DIM
B_WIDTH = len(B_PATTERNS) * 3 * B_HEADS * HEAD_DIM
C_WIDTH = 2 * C_HEADS * C_DK + C_HEADS * C_DV
G_GATES = N_BRANCHES * D_MODEL
G_WIDTH = G_GATES + C_HEADS * C_DV + 2 * LANES
G_R_COL = G_GATES
G_ALR_COL = G_GATES + C_HEADS * C_DV


def _cparams(sem):
    return pltpu.CompilerParams(dimension_semantics=sem, vmem_limit_bytes=VMEM_LIMIT)


def _norm_matmul_kernel(x_ref, g_ref, w_ref, o_ref, h_ref):
    @pl.when(pl.program_id(1) == 0)
    def _():
        x = x_ref[...]
        ms = jnp.mean(x * x, axis=-1, keepdims=True)
        h_ref[...] = (x * lax.rsqrt(ms + RMS_EPS) * g_ref[...]).astype(BF16)

    o_ref[...] = jnp.dot(h_ref[...], w_ref[...], preferred_element_type=F32).astype(o_ref.dtype)


def _norm_matmul(x2, g, w, tn, out_dtype, name):
    rows, d = x2.shape
    n = w.shape[1]
    tm = min(1024, rows)
    return pl.pallas_call(
        _norm_matmul_kernel,
        grid=(rows // tm, n // tn),
        in_specs=[
            pl.BlockSpec((tm, d), lambda i, j: (i, 0)),
            pl.BlockSpec((1, d), lambda i, j: (0, 0)),
            pl.BlockSpec((d, tn), lambda i, j: (0, j)),
        ],
        out_specs=pl.BlockSpec((tm, tn), lambda i, j: (i, j)),
        out_shape=jax.ShapeDtypeStruct((rows, n), out_dtype),
        scratch_shapes=[pltpu.VMEM((tm, d), BF16)],
        compiler_params=_cparams(("parallel", "arbitrary")),
        name=name,
    )(x2, g.reshape(1, d), w)


def _attn_kernel(*refs, hq, grp, max_dist, has_sink, want_lse):
    q_ref, kp_ref, kc_ref, vp_ref, vc_ref, bias_ref = refs[:6]
    pos = 6
    sink_ref = None
    if has_sink:
        sink_ref = refs[pos]
        pos += 1
    o_ref = refs[pos]
    lse_ref = refs[pos + 1] if want_lse else None

    blk = pl.program_id(2)
    q = q_ref[0]
    k = jnp.concatenate([kp_ref[0], kc_ref[0]], axis=0)
    v = jnp.concatenate([vp_ref[0], vc_ref[0]], axis=0)
    qi = lax.broadcasted_iota(jnp.int32, (BLOCK, 2 * BLOCK), 0)
    kj = lax.broadcasted_iota(jnp.int32, (BLOCK, 2 * BLOCK), 1)
    dist = qi + BLOCK - kj
    first_key = jnp.where(blk > 0, 0, BLOCK)
    mask = (dist >= 0) & (dist <= max_dist) & (kj >= first_key)

    outs, lses = [], []
    for h in range(hq):
        kvh = h // grp
        qh = q[:, h * HEAD_DIM:(h + 1) * HEAD_DIM]
        kh = k[:, kvh * HEAD_DIM:(kvh + 1) * HEAD_DIM]
        vh = v[:, kvh * HEAD_DIM:(kvh + 1) * HEAD_DIM]
        s = lax.dot_general(qh, kh, (((1,), (1,)), ((), ())), preferred_element_type=F32)
        s = s * (HEAD_DIM ** -0.5)
        s = jnp.where(mask, s + bias_ref[h], NEG_INF)
        m = jnp.max(s, axis=-1, keepdims=True)
        if has_sink:
            m = jnp.maximum(m, sink_ref[h])
        p = jnp.exp(s - m)
        denom = jnp.sum(p, axis=-1, keepdims=True)
        if has_sink:
            denom = denom + jnp.exp(sink_ref[h] - m)
        o = jnp.dot(p.astype(BF16), vh, preferred_element_type=F32) / denom
        outs.append(o)
        if want_lse:
            lses.append(jnp.broadcast_to(m + jnp.log(denom), (BLOCK, HEAD_DIM)))
    o_ref[0] = jnp.concatenate(outs, axis=-1).astype(o_ref.dtype)
    if want_lse:
        lse_ref[0] = jnp.concatenate(lses, axis=-1)


def _banded_attention(qkv, bias, sink, *, batch, seq, dil, width, q_col, k_col, v_col,
                      hq, grp, max_dist, want_lse, out_dtype, name):
    sub = seq // dil
    nb = sub // BLOCK
    qw = hq * HEAD_DIM
    kw = (hq // grp) * HEAD_DIM
    view = qkv.reshape(batch, sub, dil * width)
    q_blk, k_blk, v_blk = q_col // qw, k_col // kw, v_col // kw
    q_per, k_per = width // qw, width // kw
    assert q_col % qw == 0 and k_col % kw == 0 and v_col % kw == 0
    assert dil == 1 or (width % qw == 0 and width % kw == 0)

    in_specs = [
        pl.BlockSpec((1, BLOCK, qw), lambda b, r, i: (b, i, r * q_per + q_blk)),
        pl.BlockSpec((1, BLOCK, kw), lambda b, r, i: (b, jnp.maximum(i - 1, 0), r * k_per + k_blk)),
        pl.BlockSpec((1, BLOCK, kw), lambda b, r, i: (b, i, r * k_per + k_blk)),
        pl.BlockSpec((1, BLOCK, kw), lambda b, r, i: (b, jnp.maximum(i - 1, 0), r * k_per + v_blk)),
        pl.BlockSpec((1, BLOCK, kw), lambda b, r, i: (b, i, r * k_per + v_blk)),
        pl.BlockSpec((hq, BLOCK, 2 * BLOCK), lambda b, r, i: (0, 0, 0)),
    ]
    args = [view, view, view, view, view, bias]
    if sink is not None:
        in_specs.append(pl.BlockSpec(memory_space=pltpu.SMEM))
        args.append(sink)
    o_spec = pl.BlockSpec((1, BLOCK, qw), lambda b, r, i: (b, i, r))
    o_shape = jax.ShapeDtypeStruct((batch, sub, dil * qw), out_dtype)
    out_specs, out_shape = o_spec, o_shape
    if want_lse:
        out_specs = [o_spec, pl.BlockSpec((1, BLOCK, qw), lambda b, r, i: (b, i, r))]
        out_shape = [o_shape, jax.ShapeDtypeStruct((batch, sub, dil * qw), F32)]
    res = pl.pallas_call(
        functools.partial(_attn_kernel, hq=hq, grp=grp, max_dist=max_dist,
                          has_sink=sink is not None, want_lse=want_lse),
        grid=(batch, dil, nb),
        in_specs=in_specs,
        out_specs=out_specs,
        out_shape=out_shape,
        compiler_params=_cparams(("parallel", "parallel", "arbitrary")),
        name=name,
    )(*args)
    if want_lse:
        return res[0].reshape(batch * seq, qw), res[1].reshape(batch * seq, qw)
    return res.reshape(batch * seq, qw)


def _gla_kernel(q_ref, k_ref, v_ref, alr_ref, r_ref, aw_ref, ab_ref, hn_ref, o_ref, st_ref):
    @pl.when(pl.program_id(1) == 0)
    def _():
        st_ref[...] = jnp.zeros_like(st_ref)

    ck = C_CHUNK
    z = jnp.dot(alr_ref[0].astype(BF16), aw_ref[...], preferred_element_type=F32) + ab_ref[...]
    log_a = (jnp.minimum(z, 0.0) - jnp.log1p(jnp.exp(-jnp.abs(z)))) * (1.0 / C_GATE_TAU)
    row = lax.broadcasted_iota(jnp.int32, log_a.shape, 0)
    cum = log_a
    shift = 1
    while shift < ck:
        cum = cum + jnp.where(row >= shift, pltpu.roll(cum, shift, 0), 0.0)
        shift *= 2
    last = cum[ck - 1:ck, :]
    q = q_ref[0].astype(F32) * (C_DK ** -0.5)
    k = k_ref[0].astype(F32)
    q_dec = (q * jnp.exp(cum)).astype(BF16)
    k_inv = (k * jnp.exp(-cum)).astype(BF16)
    k_end = (k * jnp.exp(last - cum)).astype(BF16)
    decay = jnp.exp(last)
    v = v_ref[0]
    r = r_ref[0]
    ti = lax.broadcasted_iota(jnp.int32, (ck, ck), 0)
    si = lax.broadcasted_iota(jnp.int32, (ck, ck), 1)
    causal = ti >= si
    nt = (((1,), (1,)), ((), ()))
    tn = (((0,), (0,)), ((), ()))
    outs = []
    for h in range(C_HEADS):
        ks = slice(h * C_DK, (h + 1) * C_DK)
        vs = slice(h * C_DV, (h + 1) * C_DV)
        qd, ki, ke, vh = q_dec[:, ks], k_inv[:, ks], k_end[:, ks], v[:, vs]
        att = jnp.where(causal, lax.dot_general(qd, ki, nt, preferred_element_type=F32), 0.0)
        st = st_ref[h]
        o = jnp.dot(att.astype(BF16), vh, preferred_element_type=F32)
        o = o + lax.dot_general(qd, st.astype(BF16), nt, preferred_element_type=F32)
        kv_t = lax.dot_general(vh, ke, tn, preferred_element_type=F32)
        st_ref[h] = st * decay[:, ks] + kv_t
        o = o * lax.rsqrt(jnp.mean(o * o, axis=-1, keepdims=True) + RMS_EPS) * hn_ref[...]
        rh = r[:, vs]
        outs.append(o * (rh * jax.nn.sigmoid(rh)))
    o_ref[0] = jnp.concatenate(outs, axis=-1).astype(o_ref.dtype)


def _gla(c_qkv, g_all, aw, ab, hn, *, batch, seq):
    nc = seq // C_CHUNK
    qk_w = C_HEADS * C_DK
    v_w = C_HEADS * C_DV
    cv = c_qkv.reshape(batch, seq, C_WIDTH)
    gv = g_all.reshape(batch, seq, G_WIDTH)
    return pl.pallas_call(
        _gla_kernel,
        grid=(batch, nc),
        in_specs=[
            pl.BlockSpec((1, C_CHUNK, qk_w), lambda b, c: (b, c, 0)),
            pl.BlockSpec((1, C_CHUNK, qk_w), lambda b, c: (b, c, 1)),
            pl.BlockSpec((1, C_CHUNK, v_w), lambda b, c: (b, c, 1)),
            pl.BlockSpec((1, C_CHUNK, LANES), lambda b, c: (b, c, G_ALR_COL // LANES)),
            pl.BlockSpec((1, C_CHUNK, v_w), lambda b, c: (b, c, G_R_COL // v_w)),
            pl.BlockSpec((LANES, qk_w), lambda b, c: (0, 0)),
            pl.BlockSpec((1, qk_w), lambda b, c: (0, 0)),
            pl.BlockSpec((1, C_DV), lambda b, c: (0, 0)),
        ],
        out_specs=pl.BlockSpec((1, C_CHUNK, v_w), lambda b, c: (b, c, 0)),
        out_shape=jax.ShapeDtypeStruct((batch, seq, v_w), BF16),
        scratch_shapes=[pltpu.VMEM((C_HEADS, C_DV, C_DK), F32)],
        compiler_params=_cparams(("parallel", "arbitrary")),
        name="gla",
    )(cv, cv, cv, gv, gv, aw, ab, hn).reshape(batch * seq, v_w)


def _merge_kernel(oa_ref, ob1_ref, ob2_ref, ob3_ref, l1_ref, l2_ref, l3_ref, oc_ref, gates_ref,
                  x_ref, wb_ref, wo_ref, gn_ref, xo_ref, xnt_ref):
    l1, l2, l3 = l1_ref[...], l2_ref[...], l3_ref[...]
    lm = jnp.maximum(jnp.maximum(l1, l2), l3)
    e1, e2, e3 = jnp.exp(l1 - lm), jnp.exp(l2 - lm), jnp.exp(l3 - lm)
    ob = (e1 * ob1_ref[...] + e2 * ob2_ref[...] + e3 * ob3_ref[...]) / (e1 + e2 + e3)
    branches = (oa_ref[...], ob.astype(BF16), oc_ref[...])
    merged = None
    for n in range(N_BRANCHES):
        proj = jnp.dot(branches[n], wb_ref[n], preferred_element_type=F32)
        term = jax.nn.sigmoid(gates_ref[:, n * D_MODEL:(n + 1) * D_MODEL]) * proj
        merged = term if merged is None else merged + term
    x = x_ref[...] + jnp.dot(merged.astype(BF16), wo_ref[...], preferred_element_type=F32)
    xo_ref[...] = x
    xn = x * lax.rsqrt(jnp.mean(x * x, axis=-1, keepdims=True) + RMS_EPS) * gn_ref[...]
    xnt_ref[...] = xn.T.astype(BF16)


def _merge(o_a, o_b, lse_b, o_c, g_all, x2, wb, wo, gn):
    rows = x2.shape[0]
    tm = min(512, rows)
    row_spec = lambda w: pl.BlockSpec((tm, w), lambda i: (i, 0))
    return pl.pallas_call(
        _merge_kernel,
        grid=(rows // tm,),
        in_specs=[row_spec(BRANCH_WIDTH)] * 8 + [
            row_spec(G_GATES),
            row_spec(D_MODEL),
            pl.BlockSpec((N_BRANCHES, BRANCH_WIDTH, D_MODEL), lambda i: (0, 0, 0)),
            pl.BlockSpec((D_MODEL, D_MODEL), lambda i: (0, 0)),
            pl.BlockSpec((1, D_MODEL), lambda i: (0, 0)),
        ],
        out_specs=[row_spec(D_MODEL), pl.BlockSpec((D_MODEL, tm), lambda i: (0, i))],
        out_shape=[jax.ShapeDtypeStruct((rows, D_MODEL), F32),
                   jax.ShapeDtypeStruct((D_MODEL, rows), BF16)],
        compiler_params=_cparams(("parallel",)),
        name="merge",
    )(o_a, o_b[0], o_b[1], o_b[2], lse_b[0], lse_b[1], lse_b[2], o_c, g_all, x2, wb, wo,
      gn.reshape(1, D_MODEL))


_PAIRS = tuple((i, j) for i in range(PEER_TOPK) for j in range(PEER_TOPK)
               if (i + 1) * (j + 1) <= PEER_TOPK)


def _sort16_desc(v):
    v = list(v)
    n = len(v)
    k = 2
    while k <= n:
        j = k // 2
        while j >= 1:
            for i in range(n):
                l = i ^ j
                if l > i:
                    hi, lo = jnp.maximum(v[i], v[l]), jnp.minimum(v[i], v[l])
                    v[i], v[l] = (hi, lo) if (i & k) == 0 else (lo, hi)
            j //= 2
        k *= 2
    return v


def _merge_top16(a, b):
    n = len(a)
    v = [jnp.maximum(a[i], b[n - 1 - i]) for i in range(n)]
    j = n // 2
    while j >= 1:
        for i in range(n):
            l = i ^ j
            if l > i:
                v[i], v[l] = jnp.maximum(v[i], v[l]), jnp.minimum(v[i], v[l])
        j //= 2
    return v


def _top16_of(vals):
    acc = None
    for g in range(len(vals) // PEER_TOPK):
        grp = _sort16_desc(vals[g * PEER_TOPK:(g + 1) * PEER_TOPK])
        acc = grp if acc is None else _merge_top16(acc, grp)
    return acc


def _peer_select_kernel(xnt_ref, wq_ref, kb_ref, cnt_ref, e1_ref, rank_ref, e2_ref,
                        sc_ref, rk_ref, eb_ref, *, tb):
    nk, nh, kk = PEER_KEYS, PEER_HEADS, PEER_TOPK
    xnt = xnt_ref[...]
    for p in range(2):
        qy = jnp.dot(wq_ref[p], xnt, preferred_element_type=F32).astype(BF16)
        sc_ref[p] = jnp.dot(kb_ref[p], qy, preferred_element_type=F32).reshape(nk, nh, tb)

    for c in range(tb // LANES):
        cs = slice(c * LANES, (c + 1) * LANES)
        s1s = _top16_of([sc_ref[0, n, :, cs] for n in range(nk)])
        s2s = _top16_of([sc_ref[1, n, :, cs] for n in range(nk)])
        cands = [s1s[i] + s2s[j] for (i, j) in _PAIRS]
        pad = [jnp.full_like(cands[0], -jnp.inf)] * (-len(cands) % kk)
        tau = _top16_of(cands + pad)[kk - 1]
        top = cands[0]
        zsum = None
        counts = [None] * kk
        for (i, j), cd in zip(_PAIRS, cands):
            sel = cd >= tau
            term = jnp.where(sel, jnp.exp(cd - top), 0.0)
            zsum = term if zsum is None else zsum + term
            one = jnp.where(sel, 1.0, 0.0)
            counts[i] = one if counts[i] is None else counts[i] + one
        inv_z = 1.0 / zsum

        def first_half(n, carry):
            s1 = sc_ref[0, n, :, cs]
            cnt = jnp.zeros_like(s1)
            for i in range(kk):
                cnt = jnp.where(s1 == s1s[i], counts[i], cnt)
            r0 = pl.multiple_of(n * nh, nh)
            cnt_ref[pl.ds(r0, nh), cs] = cnt
            e1_ref[pl.ds(r0, nh), cs] = jnp.exp(s1 - s1s[0])
            return carry

        def second_half(n, carry):
            s2 = sc_ref[1, n, :, cs]
            rank = jnp.full_like(s2, float(kk))
            for j in range(kk):
                rank = jnp.where(s2 == s2s[j], float(j), rank)
            r0 = pl.multiple_of(n * nh, nh)
            rk_ref[pl.ds(r0, nh), :] = rank
            eb_ref[pl.ds(r0, nh), :] = jnp.exp(s2 - s2s[0]) * inv_z
            return carry

        lax.fori_loop(0, nk, first_half, 0, unroll=4)
        lax.fori_loop(0, nk, second_half, 0, unroll=4)
        for h in range(nh):
            rank_ref[h * nk:(h + 1) * nk, cs] = rk_ref[pl.ds(h, nk, stride=nh), :].astype(BF16)
            e2_ref[h * nk:(h + 1) * nk, cs] = eb_ref[pl.ds(h, nk, stride=nh), :].astype(BF16)


def _peer_select(xnt, wq_t, k_big):
    d, rows = xnt.shape
    tb = min(256, rows)
    nrow = PEER_KEYS * PEER_HEADS
    tab_spec = pl.BlockSpec((nrow, tb), lambda i: (0, i))
    return pl.pallas_call(
        functools.partial(_peer_select_kernel, tb=tb),
        grid=(rows // tb,),
        in_specs=[
            pl.BlockSpec((d, tb), lambda i: (0, i)),
            pl.BlockSpec((2, nrow, d), lambda i: (0, 0, 0)),
            pl.BlockSpec((2, nrow, nrow), lambda i: (0, 0, 0)),
        ],
        out_specs=[tab_spec] * 4,
        out_shape=[jax.ShapeDtypeStruct((nrow, rows), dt) for dt in (F32, F32, BF16, BF16)],
        scratch_shapes=[
            pltpu.VMEM((2, PEER_KEYS, PEER_HEADS, tb), F32),
            pltpu.VMEM((nrow, LANES), F32),
            pltpu.VMEM((nrow, LANES), F32),
        ],
        compiler_params=_cparams(("parallel",)),
        name="peer_select",
    )(xnt, wq_t, k_big)


def _gelu_tanh(x):
    c = math.sqrt(2.0 / math.pi)
    return 0.5 * x * (1.0 + jnp.tanh(c * (x + 0.044715 * (x * x * x))))


def _peer_dense_kernel(xnt_ref, u_ref, vt_ref, cnt_ref, e1_ref, rank_ref, e2_ref, x_ref, gn_ref,
                       o_ref, acc_ref, ht_ref, w_ref, *, tb, eb, slab, final_norm):
    j = pl.program_id(1)
    nk, nh = PEER_KEYS, PEER_HEADS
    pk = 16
    n_slab = eb // slab

    @pl.when(j == 0)
    def _():
        acc_ref[...] = jnp.zeros_like(acc_ref)

    def hidden(s):
        rows = slice(s * slab, (s + 1) * slab)
        ht_ref[rows, :] = jnp.dot(u_ref[rows, :], xnt_ref[...], preferred_element_type=F32)

    def gate(s):
        for al in range(s * slab // nk, (s + 1) * slab // nk):
            rows = slice(al * nk, (al + 1) * nk)
            for c in range(tb // LANES):
                cs = slice(c * LANES, (c + 1) * LANES)
                cnt8 = cnt_ref[al * nh:(al + 1) * nh, cs]
                e18 = e1_ref[al * nh:(al + 1) * nh, cs]
                act = _gelu_tanh(ht_ref[rows, cs]).astype(BF16).reshape(nk // pk, pk, LANES)
                g = jnp.zeros((nk // pk, pk, LANES), BF16)
                for h in range(nh):
                    cb = jnp.broadcast_to(cnt8[h:h + 1, :], (pk, LANES)).astype(BF16)
                    eb_ = jnp.broadcast_to(e18[h:h + 1, :], (pk, LANES)).astype(BF16)
                    rk = rank_ref[h * nk:(h + 1) * nk, cs].reshape(nk // pk, pk, LANES)
                    e2 = e2_ref[h * nk:(h + 1) * nk, cs].reshape(nk // pk, pk, LANES)
                    g = g + jnp.where(rk < cb[None], e2, jnp.zeros_like(e2)) * eb_[None]
                w_ref[rows, cs] = (act * g).reshape(nk, LANES)

    def project(s):
        rows = slice(s * slab, (s + 1) * slab)
        acc_ref[...] += jnp.dot(vt_ref[:, rows], w_ref[rows, :], preferred_element_type=F32)

    hidden(0)
    for s in range(n_slab):
        if s + 1 < n_slab:
            hidden(s + 1)
        gate(s)
        if s >= 1:
            project(s - 1)
    project(n_slab - 1)

    @pl.when(j == pl.num_programs(1) - 1)
    def _():
        x = x_ref[...] + acc_ref[...].T
        if final_norm:
            x = x * lax.rsqrt(jnp.mean(x * x, axis=-1, keepdims=True) + RMS_EPS) * gn_ref[...]
        o_ref[...] = x


def _peer_dense(xnt, u, vt, cnt, e1, rank, e2, x2, gn, final_norm):
    d, rows = xnt.shape
    ne = u.shape[0]
    tb = min(512, rows)
    eb = 1024
    slab = 256
    nrow = PEER_KEYS * PEER_HEADS
    arow = (eb // PEER_KEYS) * PEER_HEADS
    return pl.pallas_call(
        functools.partial(_peer_dense_kernel, tb=tb, eb=eb, slab=slab, final_norm=final_norm),
        grid=(rows // tb, ne // eb),
        in_specs=[
            pl.BlockSpec((d, tb), lambda i, j: (0, i)),
            pl.BlockSpec((eb, d), lambda i, j: (j, 0)),
            pl.BlockSpec((d, eb), lambda i, j: (0, j)),
            pl.BlockSpec((arow, tb), lambda i, j: (j, i)),
            pl.BlockSpec((arow, tb), lambda i, j: (j, i)),
            pl.BlockSpec((nrow, tb), lambda i, j: (0, i)),
            pl.BlockSpec((nrow, tb), lambda i, j: (0, i)),
            pl.BlockSpec((tb, d), lambda i, j: (i, 0)),
            pl.BlockSpec((1, d), lambda i, j: (0, 0)),
        ],
        out_specs=pl.BlockSpec((tb, d), lambda i, j: (i, 0)),
        out_shape=jax.ShapeDtypeStruct((rows, d), F32),
        scratch_shapes=[
            pltpu.VMEM((d, tb), F32),
            pltpu.VMEM((eb, tb), F32),
            pltpu.VMEM((eb, tb), BF16),
        ],
        compiler_params=_cparams(("parallel", "arbitrary")),
        name="peer_dense",
    )(xnt, u, vt, cnt, e1, rank, e2, x2, gn.reshape(1, d))


def _t5_bucket(dist):
    max_exact = N_BUCKETS // 2
    large = max_exact + (jnp.log(jnp.maximum(dist, 1).astype(F32) / max_exact)
                         / math.log(BUCKET_MAX_DIST / max_exact) * (N_BUCKETS - max_exact)).astype(jnp.int32)
    large = jnp.minimum(large, N_BUCKETS - 1)
    return jnp.where(dist < max_exact, dist, large)


def _rel_bias(table_cols, dilation):
    qi = jnp.arange(BLOCK)[:, None]
    kj = jnp.arange(2 * BLOCK)[None, :]
    dist = jnp.maximum(qi + BLOCK - kj, 0) * dilation
    return jnp.moveaxis(table_cols[_t5_bucket(dist)], -1, 0).astype(F32)


def _split_w_in(w):
    hd = HEAD_DIM
    sizes = [A_Q_HEADS * hd, A_KV_HEADS * hd, A_KV_HEADS * hd] + [B_HEADS * hd] * 9 + [
        C_HEADS * C_DK, C_HEADS * C_DK, C_HEADS * C_DV, C_GATE_RANK, C_HEADS * C_DV, N_BRANCHES * D_MODEL]
    offs = np.concatenate([[0], np.cumsum(sizes)])
    col = lambda a, b: w[:, int(offs[a]):int(offs[b])]
    w_a = col(0, 3)
    w_b = col(3, 12)
    w_c = col(12, 15)
    pad = jnp.zeros((w.shape[0], G_WIDTH - G_ALR_COL - C_GATE_RANK), w.dtype)
    w_g = jnp.concatenate([col(17, 18), col(16, 17), col(15, 16), pad], axis=1)
    return [t.astype(BF16) for t in (w_a, w_b, w_c, w_g)]


def _peer_key_matrix(keys):
    nh, _, nk, c = keys.shape
    eye = jnp.eye(nh, dtype=keys.dtype)
    big = jnp.einsum('hpnc,hg->pnhgc', keys, eye)
    return big.reshape(2, nk * nh, nh * c).astype(BF16)


def kernel(x, w_in, attn_sinks, gla_alpha_w, gla_alpha_b, gla_head_norm, w_branch, w_out, norm_mix,
           norm_ffn, peer_wq, peer_keys, peer_u, peer_v, rel_bias_table, norm_final):
    batch, seq, d = x.shape
    rows = batch * seq
    depth = w_in.shape[0]
    hw = B_HEADS * HEAD_DIM

    bias_a = _rel_bias(rel_bias_table[:, :A_Q_HEADS], 1)
    bias_b = [_rel_bias(rel_bias_table[:, A_Q_HEADS + i * B_HEADS:A_Q_HEADS + (i + 1) * B_HEADS], dil)
              for i, (_, dil) in enumerate(B_PATTERNS)]

    x2 = x.reshape(rows, d)
    for l in range(depth):
        w_a, w_b, w_c, w_g = _split_w_in(w_in[l])
        a_qkv = _norm_matmul(x2, norm_mix[l], w_a, 768, BF16, "proj_a")
        b_qkv = _norm_matmul(x2, norm_mix[l], w_b, 1152, BF16, "proj_b")
        c_qkv = _norm_matmul(x2, norm_mix[l], w_c, 1024, BF16, "proj_c")
        g_all = _norm_matmul(x2, norm_mix[l], w_g, 1280, F32, "proj_g")

        o_a = _banded_attention(
            a_qkv, bias_a, attn_sinks[l], batch=batch, seq=seq, dil=1, width=A_WIDTH,
            q_col=0, k_col=A_Q_HEADS * HEAD_DIM, v_col=(A_Q_HEADS + A_KV_HEADS) * HEAD_DIM,
            hq=A_Q_HEADS, grp=A_Q_HEADS // A_KV_HEADS, max_dist=A_WINDOW - 1,
            want_lse=False, out_dtype=BF16, name="attn_a")
        o_b, lse_b = [], []
        for gi, (window, dil) in enumerate(B_PATTERNS):
            o, lse = _banded_attention(
                b_qkv, bias_b[gi], None, batch=batch, seq=seq, dil=dil, width=B_WIDTH,
                q_col=gi * 3 * hw, k_col=gi * 3 * hw + hw, v_col=gi * 3 * hw + 2 * hw,
                hq=B_HEADS, grp=1, max_dist=window // dil,
                want_lse=True, out_dtype=F32, name="attn_b%d" % gi)
            o_b.append(o)
            lse_b.append(lse)

        aw = jnp.zeros((LANES, C_HEADS * C_DK), BF16).at[:C_GATE_RANK].set(gla_alpha_w[l].astype(BF16))
        o_c = _gla(c_qkv, g_all, aw, gla_alpha_b[l].reshape(1, -1), gla_head_norm[l].reshape(1, -1),
                   batch=batch, seq=seq)

        x2, xnt = _merge(o_a, o_b, lse_b, o_c, g_all, x2, w_branch[l].astype(BF16),
                         w_out[l].astype(BF16), norm_ffn[l])

        wq_t = peer_wq[l].reshape(d, PEER_HEADS, 2, PEER_DKEY // 2).transpose(2, 1, 3, 0)
        wq_t = wq_t.reshape(2, PEER_HEADS * (PEER_DKEY // 2), d).astype(BF16)
        cnt, e1, rank, e2 = _peer_select(xnt, wq_t, _peer_key_matrix(peer_keys[l]))
        last = l == depth - 1
        x2 = _peer_dense(xnt, peer_u[l].astype(BF16), peer_v[l].T.astype(BF16), cnt, e1, rank, e2,
                         x2, norm_final, final_norm=last)
    return x2.reshape(batch, seq, d)
```

```python
import functools
import math

import numpy as np
import jax
import jax.numpy as jnp
from jax import lax
from jax.experimental import pallas as pl
from jax.experimental.pallas import tpu as pltpu

F32 = jnp.float32
BF16 = jnp.bfloat16

D_MODEL = 1024
HEAD_DIM = 64
BLOCK = 128
A_Q_HEADS = 8
A_KV_HEADS = 2
A_WINDOW = 128
B_PATTERNS = ((128, 1), (512, 4), (2048, 16))
B_HEADS = 8
C_HEADS = 4
C_DK = 64
C_DV = 128
C_GATE_RANK = 16
C_GATE_TAU = 16.0
C_CHUNK = 64
BRANCH_WIDTH = 512
N_BRANCHES = 3
N_BUCKETS = 32
BUCKET_MAX_DIST = 2048
PEER_HEADS = 8
PEER_KEYS = 128
PEER_N_EXPERTS = PEER_KEYS * PEER_KEYS
PEER_TOPK = 16
PEER_DKEY = 256
RMS_EPS = 1e-6
NEG_INF = -1e30

LANES = 128
VMEM_LIMIT = 56 * 1024 * 1024

A_WIDTH = (A_Q_HEADS + 2 * A_KV_HEADS) * HEAD_DIM
B_WIDTH = len(B_PATTERNS) * 3 * B_HEADS * HEAD_DIM
C_WIDTH = 2 * C_HEADS * C_DK + C_HEADS * C_DV
G_GATES = N_BRANCHES * D_MODEL
G_WIDTH = C_HEADS * C_DV + LANES
G_R_COL = 0
G_ALR_COL = C_HEADS * C_DV
GLA_CHUNKS_PER_STEP = 4


def _cparams(sem):
    return pltpu.CompilerParams(dimension_semantics=sem, vmem_limit_bytes=VMEM_LIMIT)


def _norm_matmul_kernel(x_ref, g_ref, w_ref, o_ref, h_ref):
    @pl.when(pl.program_id(1) == 0)
    def _():
        x = x_ref[...]
        ms = jnp.mean(x * x, axis=-1, keepdims=True)
        h_ref[...] = (x * lax.rsqrt(ms + RMS_EPS) * g_ref[...]).astype(BF16)

    o_ref[...] = jnp.dot(h_ref[...], w_ref[...], preferred_element_type=F32).astype(o_ref.dtype)


def _norm_matmul(x2, g, w, tn, out_dtype, name):
    rows, d = x2.shape
    n = w.shape[1]
    tm = min(1024, rows)
    return pl.pallas_call(
        _norm_matmul_kernel,
        grid=(rows // tm, n // tn),
        in_specs=[
            pl.BlockSpec((tm, d), lambda i, j: (i, 0)),
            pl.BlockSpec((1, d), lambda i, j: (0, 0)),
            pl.BlockSpec((d, tn), lambda i, j: (0, j)),
        ],
        out_specs=pl.BlockSpec((tm, tn), lambda i, j: (i, j)),
        out_shape=jax.ShapeDtypeStruct((rows, n), out_dtype),
        scratch_shapes=[pltpu.VMEM((tm, d), BF16)],
        compiler_params=_cparams(("parallel", "arbitrary")),
        name=name,
    )(x2, g.reshape(1, d), w)


def _attn_kernel(*refs, hq, grp, has_sink, want_lse):
    q_ref, kp_ref, kc_ref, vp_ref, vc_ref, bias_ref = refs[:6]
    pos = 6
    sink_ref = None
    if has_sink:
        sink_ref = refs[pos]
        pos += 1
    o_ref = refs[pos]
    lse_ref = refs[pos + 1] if want_lse else None

    q = q_ref[0] * (HEAD_DIM ** -0.5)
    k = jnp.concatenate([kp_ref[0], kc_ref[0]], axis=0)
    v = jnp.concatenate([vp_ref[0], vc_ref[0]], axis=0)

    outs, lses = [], []
    for h in range(hq):
        kvh = h // grp
        qh = q[:, h * HEAD_DIM:(h + 1) * HEAD_DIM]
        kh = k[:, kvh * HEAD_DIM:(kvh + 1) * HEAD_DIM]
        vh = v[:, kvh * HEAD_DIM:(kvh + 1) * HEAD_DIM]
        s = lax.dot_general(qh, kh, (((1,), (1,)), ((), ())), preferred_element_type=F32)
        s = s + bias_ref[0, h]
        m = jnp.max(s, axis=-1, keepdims=True)
        if has_sink:
            m = jnp.maximum(m, sink_ref[h])
        p = jnp.exp(s - m)
        denom = jnp.sum(p, axis=-1, keepdims=True)
        if has_sink:
            denom = denom + jnp.exp(sink_ref[h] - m)
        o = jnp.dot(p.astype(BF16), vh, preferred_element_type=F32) / denom
        outs.append(o)
        if want_lse:
            lses.append(jnp.broadcast_to(m + jnp.log(denom), (BLOCK, HEAD_DIM)))
    o_ref[0] = jnp.concatenate(outs, axis=-1).astype(o_ref.dtype)
    if want_lse:
        lse_ref[0] = jnp.concatenate(lses, axis=-1)


def _banded_bias(bias, max_dist):
    qi = np.arange(BLOCK)[:, None]
    kj = np.arange(2 * BLOCK)[None, :]
    dist = qi + BLOCK - kj
    in_window = (dist >= 0) & (dist <= max_dist)
    first = in_window & (kj >= BLOCK)
    return jnp.stack([jnp.where(first, bias, NEG_INF), jnp.where(in_window, bias, NEG_INF)])


def _banded_attention(qkv, bias, sink, *, batch, seq, dil, width, q_col, k_col, v_col,
                      hq, grp, want_lse, out_dtype, name):
    sub = seq // dil
    nb = sub // BLOCK
    qw = hq * HEAD_DIM
    kw = (hq // grp) * HEAD_DIM
    view = qkv.reshape(batch * dil, sub, width)
    q_blk, k_blk, v_blk = q_col // qw, k_col // kw, v_col // kw
    assert q_col % qw == 0 and k_col % kw == 0 and v_col % kw == 0

    in_specs = [
        pl.BlockSpec((1, BLOCK, qw), lambda b, r, i: (b * dil + r, i, q_blk)),
        pl.BlockSpec((1, BLOCK, kw), lambda b, r, i: (b * dil + r, jnp.maximum(i - 1, 0), k_blk)),
        pl.BlockSpec((1, BLOCK, kw), lambda b, r, i: (b * dil + r, i, k_blk)),
        pl.BlockSpec((1, BLOCK, kw), lambda b, r, i: (b * dil + r, jnp.maximum(i - 1, 0), v_blk)),
        pl.BlockSpec((1, BLOCK, kw), lambda b, r, i: (b * dil + r, i, v_blk)),
        pl.BlockSpec((1, hq, BLOCK, 2 * BLOCK), lambda b, r, i: (jnp.minimum(i, 1), 0, 0, 0)),
    ]
    args = [view, view, view, view, view, bias]
    if sink is not None:
        in_specs.append(pl.BlockSpec(memory_space=pltpu.SMEM))
        args.append(sink)
    o_spec = pl.BlockSpec((1, BLOCK, qw), lambda b, r, i: (b, i, r))
    o_shape = jax.ShapeDtypeStruct((batch, sub, dil * qw), out_dtype)
    out_specs, out_shape = o_spec, o_shape
    if want_lse:
        out_specs = [o_spec, pl.BlockSpec((1, BLOCK, qw), lambda b, r, i: (b, i, r))]
        out_shape = [o_shape, jax.ShapeDtypeStruct((batch, sub, dil * qw), F32)]
    res = pl.pallas_call(
        functools.partial(_attn_kernel, hq=hq, grp=grp, has_sink=sink is not None, want_lse=want_lse),
        grid=(batch, dil, nb),
        in_specs=in_specs,
        out_specs=out_specs,
        out_shape=out_shape,
        compiler_params=_cparams(("parallel", "parallel", "arbitrary")),
        name=name,
    )(*args)
    if want_lse:
        return res[0].reshape(batch * seq, qw), res[1].reshape(batch * seq, qw)
    return res.reshape(batch * seq, qw)


def _gla_kernel(q_ref, k_ref, v_ref, alr_ref, r_ref, aw_ref, ab_ref, hn_ref, o_ref, st_ref):
    @pl.when(pl.program_id(1) == 0)
    def _():
        st_ref[...] = jnp.zeros_like(st_ref)

    ck = C_CHUNK
    z = jnp.dot(alr_ref[0].astype(BF16), aw_ref[...], preferred_element_type=F32) + ab_ref[...]
    log_a = (jnp.minimum(z, 0.0) - jnp.log1p(jnp.exp(-jnp.abs(z)))) * (1.0 / C_GATE_TAU)
    row = lax.broadcasted_iota(jnp.int32, log_a.shape, 0) & (ck - 1)
    cum = log_a
    shift = 1
    while shift < ck:
        cum = cum + jnp.where(row >= shift, pltpu.roll(cum, shift, 0), 0.0)
        shift *= 2
    q_all = q_ref[0].astype(F32) * (C_DK ** -0.5)
    k_all = k_ref[0].astype(F32)
    q_dec_all = (q_all * jnp.exp(cum)).astype(BF16)
    k_inv_all = (k_all * jnp.exp(-cum)).astype(BF16)
    ti = lax.broadcasted_iota(jnp.int32, (ck, ck), 0)
    si = lax.broadcasted_iota(jnp.int32, (ck, ck), 1)
    causal = ti >= si
    nt = (((1,), (1,)), ((), ()))
    tn = (((0,), (0,)), ((), ()))
    for c in range(GLA_CHUNKS_PER_STEP):
        rs = slice(c * ck, (c + 1) * ck)
        cum_c = cum[rs]
        last = cum_c[ck - 1:ck, :]
        q_dec, k_inv = q_dec_all[rs], k_inv_all[rs]
        k_end = (k_all[rs] * jnp.exp(last - cum_c)).astype(BF16)
        decay = jnp.exp(last)
        v = v_ref[0, rs, :]
        r = r_ref[0, rs, :]
        outs = []
        for h in range(C_HEADS):
            ks = slice(h * C_DK, (h + 1) * C_DK)
            vs = slice(h * C_DV, (h + 1) * C_DV)
            qd, ki, ke, vh = q_dec[:, ks], k_inv[:, ks], k_end[:, ks], v[:, vs]
            att = jnp.where(causal, lax.dot_general(qd, ki, nt, preferred_element_type=F32), 0.0)
            st = st_ref[h]
            o = jnp.dot(att.astype(BF16), vh, preferred_element_type=F32)
            o = o + lax.dot_general(qd, st.astype(BF16), nt, preferred_element_type=F32)
            kv_t = lax.dot_general(vh, ke, tn, preferred_element_type=F32)
            st_ref[h] = st * decay[:, ks] + kv_t
            o = o * lax.rsqrt(jnp.mean(o * o, axis=-1, keepdims=True) + RMS_EPS) * hn_ref[...]
            rh = r[:, vs]
            outs.append(o * (rh * jax.nn.sigmoid(rh)))
        o_ref[0, rs, :] = jnp.concatenate(outs, axis=-1).astype(o_ref.dtype)


def _gla(c_qkv, g_all, aw, ab, hn, *, batch, seq):
    rows = C_CHUNK * GLA_CHUNKS_PER_STEP
    qk_w = C_HEADS * C_DK
    v_w = C_HEADS * C_DV
    cv = c_qkv.reshape(batch, seq, C_WIDTH)
    gv = g_all.reshape(batch, seq, G_WIDTH)
    return pl.pallas_call(
        _gla_kernel,
        grid=(batch, seq // rows),
        in_specs=[
            pl.BlockSpec((1, rows, qk_w), lambda b, c: (b, c, 0)),
            pl.BlockSpec((1, rows, qk_w), lambda b, c: (b, c, 1)),
            pl.BlockSpec((1, rows, v_w), lambda b, c: (b, c, 1)),
            pl.BlockSpec((1, rows, LANES), lambda b, c: (b, c, G_ALR_COL // LANES)),
            pl.BlockSpec((1, rows, v_w), lambda b, c: (b, c, G_R_COL // v_w)),
            pl.BlockSpec((LANES, qk_w), lambda b, c: (0, 0)),
            pl.BlockSpec((1, qk_w), lambda b, c: (0, 0)),
            pl.BlockSpec((1, C_DV), lambda b, c: (0, 0)),
        ],
        out_specs=pl.BlockSpec((1, rows, v_w), lambda b, c: (b, c, 0)),
        out_shape=jax.ShapeDtypeStruct((batch, seq, v_w), BF16),
        scratch_shapes=[pltpu.VMEM((C_HEADS, C_DV, C_DK), F32)],
        compiler_params=_cparams(("parallel", "arbitrary")),
        name="gla",
    )(cv, cv, cv, gv, gv, aw, ab, hn).reshape(batch * seq, v_w)


def _merge_kernel(oa_ref, ob1_ref, ob2_ref, ob3_ref, l1_ref, l2_ref, l3_ref, oc_ref, x_ref,
                  gm_ref, wg_ref, wb_ref, wo_ref, gn_ref, xo_ref, xnt_ref):
    l1, l2, l3 = l1_ref[...], l2_ref[...], l3_ref[...]
    lm = jnp.maximum(jnp.maximum(l1, l2), l3)
    e1, e2, e3 = jnp.exp(l1 - lm), jnp.exp(l2 - lm), jnp.exp(l3 - lm)
    ob = (e1 * ob1_ref[...] + e2 * ob2_ref[...] + e3 * ob3_ref[...]) / (e1 + e2 + e3)
    branches = (oa_ref[...], ob.astype(BF16), oc_ref[...])
    x = x_ref[...]
    h = (x * lax.rsqrt(jnp.mean(x * x, axis=-1, keepdims=True) + RMS_EPS) * gm_ref[...]).astype(BF16)
    merged = None
    for n in range(N_BRANCHES):
        gate = jnp.dot(h, wg_ref[:, n * D_MODEL:(n + 1) * D_MODEL], preferred_element_type=F32)
        proj = jnp.dot(branches[n], wb_ref[n], preferred_element_type=F32)
        term = jax.nn.sigmoid(gate) * proj
        merged = term if merged is None else merged + term
    x = x + jnp.dot(merged.astype(BF16), wo_ref[...], preferred_element_type=F32)
    xo_ref[...] = x
    xn = x * lax.rsqrt(jnp.mean(x * x, axis=-1, keepdims=True) + RMS_EPS) * gn_ref[...]
    xnt_ref[...] = xn.T.astype(BF16)


def _merge(o_a, o_b, lse_b, o_c, x2, gm, wg, wb, wo, gn):
    rows = x2.shape[0]
    tm = min(512, rows)
    row_spec = lambda w: pl.BlockSpec((tm, w), lambda i: (i, 0))
    vec_spec = pl.BlockSpec((1, D_MODEL), lambda i: (0, 0))
    return pl.pallas_call(
        _merge_kernel,
        grid=(rows // tm,),
        in_specs=[row_spec(BRANCH_WIDTH)] * 8 + [
            row_spec(D_MODEL),
            vec_spec,
            pl.BlockSpec((D_MODEL, G_GATES), lambda i: (0, 0)),
            pl.BlockSpec((N_BRANCHES, BRANCH_WIDTH, D_MODEL), lambda i: (0, 0, 0)),
            pl.BlockSpec((D_MODEL, D_MODEL), lambda i: (0, 0)),
            vec_spec,
        ],
        out_specs=[row_spec(D_MODEL), pl.BlockSpec((D_MODEL, tm), lambda i: (0, i))],
        out_shape=[jax.ShapeDtypeStruct((rows, D_MODEL), F32),
                   jax.ShapeDtypeStruct((D_MODEL, rows), BF16)],
        compiler_params=_cparams(("parallel",)),
        name="merge",
    )(o_a, o_b[0], o_b[1], o_b[2], lse_b[0], lse_b[1], lse_b[2], o_c, x2, gm.reshape(1, D_MODEL),
      wg, wb, wo, gn.reshape(1, D_MODEL))


_PAIRS = tuple((i, j) for i in range(PEER_TOPK) for j in range(PEER_TOPK)
               if (i + 1) * (j + 1) <= PEER_TOPK)


def _sort16_desc(v):
    v = list(v)
    n = len(v)
    k = 2
    while k <= n:
        j = k // 2
        while j >= 1:
            for i in range(n):
                l = i ^ j
                if l > i:
                    hi, lo = jnp.maximum(v[i], v[l]), jnp.minimum(v[i], v[l])
                    v[i], v[l] = (hi, lo) if (i & k) == 0 else (lo, hi)
            j //= 2
        k *= 2
    return v


def _merge_top16(a, b):
    n = len(a)
    v = [jnp.maximum(a[i], b[n - 1 - i]) for i in range(n)]
    j = n // 2
    while j >= 1:
        for i in range(n):
            l = i ^ j
            if l > i:
                v[i], v[l] = jnp.maximum(v[i], v[l]), jnp.minimum(v[i], v[l])
        j //= 2
    return v


def _top16_of(vals):
    acc = None
    for g in range(len(vals) // PEER_TOPK):
        grp = _sort16_desc(vals[g * PEER_TOPK:(g + 1) * PEER_TOPK])
        acc = grp if acc is None else _merge_top16(acc, grp)
    return acc


def _peer_select_kernel(xnt_ref, wq_ref, kb_ref, cnt_ref, e1_ref, rank_ref, e2_ref,
                        sc_ref, rk_ref, eb_ref, *, tb):
    nk, nh, kk = PEER_KEYS, PEER_HEADS, PEER_TOPK
    xnt = xnt_ref[...]
    for p in range(2):
        qy = jnp.dot(wq_ref[p], xnt, preferred_element_type=F32).astype(BF16)
        sc_ref[p] = jnp.dot(kb_ref[p], qy, preferred_element_type=F32).reshape(nk, nh, tb)

    for c in range(tb // LANES):
        cs = slice(c * LANES, (c + 1) * LANES)
        s1s = _top16_of([sc_ref[0, n, :, cs] for n in range(nk)])
        s2s = _top16_of([sc_ref[1, n, :, cs] for n in range(nk)])
        cands = [s1s[i] + s2s[j] for (i, j) in _PAIRS]
        pad = [jnp.full_like(cands[0], -jnp.inf)] * (-len(cands) % kk)
        tau = _top16_of(cands + pad)[kk - 1]
        top = cands[0]
        zsum = None
        counts = [None] * kk
        for (i, j), cd in zip(_PAIRS, cands):
            sel = cd >= tau
            term = jnp.where(sel, jnp.exp(cd - top), 0.0)
            zsum = term if zsum is None else zsum + term
            one = jnp.where(sel, 1.0, 0.0)
            counts[i] = one if counts[i] is None else counts[i] + one
        inv_z = 1.0 / zsum

        def first_half(n, carry):
            s1 = sc_ref[0, n, :, cs]
            cnt = jnp.zeros_like(s1)
            for i in range(kk):
                cnt = jnp.where(s1 == s1s[i], counts[i], cnt)
            r0 = pl.multiple_of(n * nh, nh)
            cnt_ref[pl.ds(r0, nh), cs] = cnt
            e1_ref[pl.ds(r0, nh), cs] = jnp.exp(s1 - s1s[0])
            return carry

        def second_half(n, carry):
            s2 = sc_ref[1, n, :, cs]
            rank = jnp.full_like(s2, float(kk))
            for j in range(kk):
                rank = jnp.where(s2 == s2s[j], float(j), rank)
            r0 = pl.multiple_of(n * nh, nh)
            rk_ref[pl.ds(r0, nh), :] = rank
            eb_ref[pl.ds(r0, nh), :] = jnp.exp(s2 - s2s[0]) * inv_z
            return carry

        lax.fori_loop(0, nk, first_half, 0, unroll=4)
        lax.fori_loop(0, nk, second_half, 0, unroll=4)
        for h in range(nh):
            rank_ref[h * nk:(h + 1) * nk, cs] = rk_ref[pl.ds(h, nk, stride=nh), :].astype(BF16)
            e2_ref[h * nk:(h + 1) * nk, cs] = eb_ref[pl.ds(h, nk, stride=nh), :].astype(BF16)


def _peer_select(xnt, wq_t, k_big):
    d, rows = xnt.shape
    tb = min(256, rows)
    nrow = PEER_KEYS * PEER_HEADS
    tab_spec = pl.BlockSpec((nrow, tb), lambda i: (0, i))
    return pl.pallas_call(
        functools.partial(_peer_select_kernel, tb=tb),
        grid=(rows // tb,),
        in_specs=[
            pl.BlockSpec((d, tb), lambda i: (0, i)),
            pl.BlockSpec((2, nrow, d), lambda i: (0, 0, 0)),
            pl.BlockSpec((2, nrow, nrow), lambda i: (0, 0, 0)),
        ],
        out_specs=[tab_spec] * 4,
        out_shape=[jax.ShapeDtypeStruct((nrow, rows), dt) for dt in (F32, F32, BF16, BF16)],
        scratch_shapes=[
            pltpu.VMEM((2, PEER_KEYS, PEER_HEADS, tb), F32),
            pltpu.VMEM((nrow, LANES), F32),
            pltpu.VMEM((nrow, LANES), F32),
        ],
        compiler_params=_cparams(("parallel",)),
        name="peer_select",
    )(xnt, wq_t, k_big)


def _gelu_tanh(x):
    c = math.sqrt(2.0 / math.pi)
    return 0.5 * x * (1.0 + jnp.tanh(c * (x + 0.044715 * (x * x * x))))


def _peer_dense_kernel(xnt_ref, u_ref, vt_ref, cnt_ref, e1_ref, rank_ref, e2_ref, x_ref, gn_ref,
                       o_ref, acc_ref, ht_ref, w_ref, *, tb, eb, slab, final_norm):
    j = pl.program_id(1)
    nk, nh = PEER_KEYS, PEER_HEADS
    pk = 16
    n_slab = eb // slab

    @pl.when(j == 0)
    def _():
        acc_ref[...] = jnp.zeros_like(acc_ref)

    def hidden(s):
        rows = slice(s * slab, (s + 1) * slab)
        ht_ref[rows, :] = jnp.dot(u_ref[rows, :], xnt_ref[...], preferred_element_type=F32)

    def gate(s):
        for al in range(s * slab // nk, (s + 1) * slab // nk):
            rows = slice(al * nk, (al + 1) * nk)
            for c in range(tb // LANES):
                cs = slice(c * LANES, (c + 1) * LANES)
                cnt8 = cnt_ref[al * nh:(al + 1) * nh, cs]
                e18 = e1_ref[al * nh:(al + 1) * nh, cs]
                act = _gelu_tanh(ht_ref[rows, cs].astype(BF16)).reshape(nk // pk, pk, LANES)
                g = jnp.zeros((nk // pk, pk, LANES), BF16)
                for h in range(nh):
                    cb = jnp.broadcast_to(cnt8[h:h + 1, :], (pk, LANES)).astype(BF16)
                    eb_ = jnp.broadcast_to(e18[h:h + 1, :], (pk, LANES)).astype(BF16)
                    rk = rank_ref[h * nk:(h + 1) * nk, cs].reshape(nk // pk, pk, LANES)
                    e2 = e2_ref[h * nk:(h + 1) * nk, cs].reshape(nk // pk, pk, LANES)
                    g = g + jnp.where(rk < cb[None], e2, jnp.zeros_like(e2)) * eb_[None]
                w_ref[rows, cs] = (act * g).reshape(nk, LANES)

    def project(s):
        rows = slice(s * slab, (s + 1) * slab)
        acc_ref[...] += jnp.dot(vt_ref[:, rows], w_ref[rows, :], preferred_element_type=F32)

    hidden(0)
    for s in range(n_slab):
        if s + 1 < n_slab:
            hidden(s + 1)
        gate(s)
        if s >= 1:
            project(s - 1)
    project(n_slab - 1)

    @pl.when(j == pl.num_programs(1) - 1)
    def _():
        x = x_ref[...] + acc_ref[...].T
        if final_norm:
            x = x * lax.rsqrt(jnp.mean(x * x, axis=-1, keepdims=True) + RMS_EPS) * gn_ref[...]
        o_ref[...] = x


def _peer_dense(xnt, u, vt, cnt, e1, rank, e2, x2, gn, final_norm):
    d, rows = xnt.shape
    ne = u.shape[0]
    tb = min(512, rows)
    eb = 1024
    slab = 512
    nrow = PEER_KEYS * PEER_HEADS
    arow = (eb // PEER_KEYS) * PEER_HEADS
    return pl.pallas_call(
        functools.partial(_peer_dense_kernel, tb=tb, eb=eb, slab=slab, final_norm=final_norm),
        grid=(rows // tb, ne // eb),
        in_specs=[
            pl.BlockSpec((d, tb), lambda i, j: (0, i)),
            pl.BlockSpec((eb, d), lambda i, j: (j, 0)),
            pl.BlockSpec((d, eb), lambda i, j: (0, j)),
            pl.BlockSpec((arow, tb), lambda i, j: (j, i)),
            pl.BlockSpec((arow, tb), lambda i, j: (j, i)),
            pl.BlockSpec((nrow, tb), lambda i, j: (0, i)),
            pl.BlockSpec((nrow, tb), lambda i, j: (0, i)),
            pl.BlockSpec((tb, d), lambda i, j: (i, 0)),
            pl.BlockSpec((1, d), lambda i, j: (0, 0)),
        ],
        out_specs=pl.BlockSpec((tb, d), lambda i, j: (i, 0)),
        out_shape=jax.ShapeDtypeStruct((rows, d), F32),
        scratch_shapes=[
            pltpu.VMEM((d, tb), F32),
            pltpu.VMEM((eb, tb), F32),
            pltpu.VMEM((eb, tb), BF16),
        ],
        compiler_params=_cparams(("parallel", "arbitrary")),
        name="peer_dense",
    )(xnt, u, vt, cnt, e1, rank, e2, x2, gn.reshape(1, d))


def _t5_bucket(dist):
    max_exact = N_BUCKETS // 2
    large = max_exact + (jnp.log(jnp.maximum(dist, 1).astype(F32) / max_exact)
                         / math.log(BUCKET_MAX_DIST / max_exact) * (N_BUCKETS - max_exact)).astype(jnp.int32)
    large = jnp.minimum(large, N_BUCKETS - 1)
    return jnp.where(dist < max_exact, dist, large)


def _rel_bias(table_cols, dilation):
    qi = jnp.arange(BLOCK)[:, None]
    kj = jnp.arange(2 * BLOCK)[None, :]
    dist = jnp.maximum(qi + BLOCK - kj, 0) * dilation
    onehot = (_t5_bucket(dist)[..., None] == jnp.arange(N_BUCKETS)).astype(F32)
    return jnp.einsum('qkb,bh->hqk', onehot, table_cols.astype(F32), precision=lax.Precision.HIGHEST)


def _split_w_in(w):
    hd = HEAD_DIM
    sizes = [A_Q_HEADS * hd, A_KV_HEADS * hd, A_KV_HEADS * hd] + [B_HEADS * hd] * 9 + [
        C_HEADS * C_DK, C_HEADS * C_DK, C_HEADS * C_DV, C_GATE_RANK, C_HEADS * C_DV, N_BRANCHES * D_MODEL]
    offs = np.concatenate([[0], np.cumsum(sizes)])
    col = lambda a, b: w[:, int(offs[a]):int(offs[b])].astype(BF16)
    pad = jnp.zeros((w.shape[0], G_WIDTH - G_ALR_COL - C_GATE_RANK), BF16)
    w_r = jnp.concatenate([col(16, 17), col(15, 16), pad], axis=1)
    return col(0, 3), [col(3 + 3 * g, 6 + 3 * g) for g in range(3)], col(12, 15), w_r, col(17, 18)


def _peer_key_matrix(keys):
    nh, _, nk, c = keys.shape
    eye = jnp.eye(nh, dtype=keys.dtype)
    big = jnp.einsum('hpnc,hg->pnhgc', keys, eye)
    return big.reshape(2, nk * nh, nh * c).astype(BF16)


def _residue_major(x2, batch, seq, dil):
    if dil == 1:
        return x2
    d = x2.shape[-1]
    return x2.reshape(batch, seq // dil, dil, d).transpose(0, 2, 1, 3).reshape(batch * seq, d)


def kernel(x, w_in, attn_sinks, gla_alpha_w, gla_alpha_b, gla_head_norm, w_branch, w_out, norm_mix,
           norm_ffn, peer_wq, peer_keys, peer_u, peer_v, rel_bias_table, norm_final):
    batch, seq, d = x.shape
    rows = batch * seq
    depth = w_in.shape[0]
    hw = B_HEADS * HEAD_DIM

    bias_a = _banded_bias(_rel_bias(rel_bias_table[:, :A_Q_HEADS], 1), A_WINDOW - 1)
    bias_b = [_banded_bias(
        _rel_bias(rel_bias_table[:, A_Q_HEADS + i * B_HEADS:A_Q_HEADS + (i + 1) * B_HEADS], dil), window // dil)
        for i, (window, dil) in enumerate(B_PATTERNS)]

    x2 = x.reshape(rows, d)
    for l in range(depth):
        w_a, w_bs, w_c, w_r, w_gates = _split_w_in(w_in[l])
        a_qkv = _norm_matmul(x2, norm_mix[l], w_a, A_WIDTH, BF16, "proj_a")
        c_qkv = _norm_matmul(x2, norm_mix[l], w_c, C_WIDTH, BF16, "proj_c")
        g_all = _norm_matmul(x2, norm_mix[l], w_r, G_WIDTH, F32, "proj_r")

        o_a = _banded_attention(
            a_qkv, bias_a, attn_sinks[l], batch=batch, seq=seq, dil=1, width=A_WIDTH,
            q_col=0, k_col=A_Q_HEADS * HEAD_DIM, v_col=(A_Q_HEADS + A_KV_HEADS) * HEAD_DIM,
            hq=A_Q_HEADS, grp=A_Q_HEADS // A_KV_HEADS, want_lse=False, out_dtype=BF16, name="attn_a")
        o_b, lse_b = [], []
        for gi, (window, dil) in enumerate(B_PATTERNS):
            b_qkv = _norm_matmul(_residue_major(x2, batch, seq, dil), norm_mix[l], w_bs[gi], 3 * hw, BF16,
                                 "proj_b%d" % gi)
            o, lse = _banded_attention(
                b_qkv, bias_b[gi], None, batch=batch, seq=seq, dil=dil, width=3 * hw,
                q_col=0, k_col=hw, v_col=2 * hw, hq=B_HEADS, grp=1,
                want_lse=True, out_dtype=F32, name="attn_b%d" % gi)
            o_b.append(o)
            lse_b.append(lse)

        aw = jnp.zeros((LANES, C_HEADS * C_DK), BF16).at[:C_GATE_RANK].set(gla_alpha_w[l].astype(BF16))
        o_c = _gla(c_qkv, g_all, aw, gla_alpha_b[l].reshape(1, -1), gla_head_norm[l].reshape(1, -1),
                   batch=batch, seq=seq)

        x2, xnt = _merge(o_a, o_b, lse_b, o_c, x2, norm_mix[l], w_gates, w_branch[l].astype(BF16),
                         w_out[l].astype(BF16), norm_ffn[l])

        wq_t = peer_wq[l].reshape(d, PEER_HEADS, 2, PEER_DKEY // 2).transpose(2, 1, 3, 0)
        wq_t = wq_t.reshape(2, PEER_HEADS * (PEER_DKEY // 2), d).astype(BF16)
        cnt, e1, rank, e2 = _peer_select(xnt, wq_t, _peer_key_matrix(peer_keys[l]))
        last = l == depth - 1
        x2 = _peer_dense(xnt, peer_u[l].astype(BF16), peer_v[l].T.astype(BF16), cnt, e1, rank, e2,
                         x2, norm_final, final_norm=last)
    return x2.reshape(batch, seq, d)
```

```python
import functools
import math

import numpy as np
import jax
import jax.numpy as jnp
from jax import lax
from jax.experimental import pallas as pl
from jax.experimental.pallas import tpu as pltpu

F32 = jnp.float32
BF16 = jnp.bfloat16

D_MODEL = 1024
HEAD_DIM = 64
BLOCK = 128
A_Q_HEADS = 8
A_KV_HEADS = 2
A_WINDOW = 128
B_PATTERNS = ((128, 1), (512, 4), (2048, 16))
B_HEADS = 8
C_HEADS = 4
C_DK = 64
C_DV = 128
C_GATE_RANK = 16
C_GATE_TAU = 16.0
C_CHUNK = 64
BRANCH_WIDTH = 512
N_BRANCHES = 3
N_BUCKETS = 32
BUCKET_MAX_DIST = 2048
PEER_HEADS = 8
PEER_KEYS = 128
PEER_N_EXPERTS = PEER_KEYS * PEER_KEYS
PEER_TOPK = 16
PEER_DKEY = 256
RMS_EPS = 1e-6
NEG_INF = -1e30

LANES = 128
VMEM_LIMIT = 56 * 1024 * 1024

A_WIDTH = (A_Q_HEADS + 2 * A_KV_HEADS) * HEAD_DIM
B_WIDTH = len(B_PATTERNS) * 3 * B_HEADS * HEAD_DIM
C_WIDTH = 2 * C_HEADS * C_DK + C_HEADS * C_DV
G_GATES = N_BRANCHES * D_MODEL
G_WIDTH = C_HEADS * C_DV + LANES
G_R_COL = 0
G_ALR_COL = C_HEADS * C_DV
GLA_CHUNKS_PER_STEP = 4


def _cparams(sem):
    return pltpu.CompilerParams(dimension_semantics=sem, vmem_limit_bytes=VMEM_LIMIT)


def _norm_matmul_kernel(x_ref, g_ref, w_ref, o_ref, h_ref):
    @pl.when(pl.program_id(1) == 0)
    def _():
        x = x_ref[...]
        ms = jnp.mean(x * x, axis=-1, keepdims=True)
        h_ref[...] = (x * lax.rsqrt(ms + RMS_EPS) * g_ref[...]).astype(BF16)

    o_ref[...] = jnp.dot(h_ref[...], w_ref[...], preferred_element_type=F32).astype(o_ref.dtype)


def _norm_matmul(x2, g, w, tn, out_dtype, name):
    rows, d = x2.shape
    n = w.shape[1]
    tm = min(1024, rows)
    return pl.pallas_call(
        _norm_matmul_kernel,
        grid=(rows // tm, n // tn),
        in_specs=[
            pl.BlockSpec((tm, d), lambda i, j: (i, 0)),
            pl.BlockSpec((1, d), lambda i, j: (0, 0)),
            pl.BlockSpec((d, tn), lambda i, j: (0, j)),
        ],
        out_specs=pl.BlockSpec((tm, tn), lambda i, j: (i, j)),
        out_shape=jax.ShapeDtypeStruct((rows, n), out_dtype),
        scratch_shapes=[pltpu.VMEM((tm, d), BF16)],
        compiler_params=_cparams(("parallel", "arbitrary")),
        name=name,
    )(x2, g.reshape(1, d), w)


def _attn_kernel(*refs, hq, grp, has_sink, want_lse):
    q_ref, kp_ref, kc_ref, vp_ref, vc_ref, bias_ref = refs[:6]
    pos = 6
    sink_ref = None
    if has_sink:
        sink_ref = refs[pos]
        pos += 1
    o_ref = refs[pos]
    lse_ref = refs[pos + 1] if want_lse else None

    q = q_ref[0] * (HEAD_DIM ** -0.5)
    k = jnp.concatenate([kp_ref[0], kc_ref[0]], axis=0)
    v = jnp.concatenate([vp_ref[0], vc_ref[0]], axis=0)

    outs, lses = [], []
    for h in range(hq):
        kvh = h // grp
        qh = q[:, h * HEAD_DIM:(h + 1) * HEAD_DIM]
        kh = k[:, kvh * HEAD_DIM:(kvh + 1) * HEAD_DIM]
        vh = v[:, kvh * HEAD_DIM:(kvh + 1) * HEAD_DIM]
        s = lax.dot_general(qh, kh, (((1,), (1,)), ((), ())), preferred_element_type=F32)
        s = s + bias_ref[0, h]
        m = jnp.max(s, axis=-1, keepdims=True)
        if has_sink:
            m = jnp.maximum(m, sink_ref[h])
        p = jnp.exp(s - m)
        denom = jnp.sum(p, axis=-1, keepdims=True)
        if has_sink:
            denom = denom + jnp.exp(sink_ref[h] - m)
        o = jnp.dot(p.astype(BF16), vh, preferred_element_type=F32) / denom
        outs.append(o)
        if want_lse:
            lses.append(jnp.broadcast_to(m + jnp.log(denom), (BLOCK, HEAD_DIM)))
    o_ref[0] = jnp.concatenate(outs, axis=-1).astype(o_ref.dtype)
    if want_lse:
        lse_ref[0] = jnp.concatenate(lses, axis=-1)


def _banded_bias(bias, max_dist):
    qi = np.arange(BLOCK)[:, None]
    kj = np.arange(2 * BLOCK)[None, :]
    dist = qi + BLOCK - kj
    in_window = (dist >= 0) & (dist <= max_dist)
    first = in_window & (kj >= BLOCK)
    return jnp.stack([jnp.where(first, bias, NEG_INF), jnp.where(in_window, bias, NEG_INF)])


def _banded_attention(qkv, bias, sink, *, batch, seq, dil, width, q_col, k_col, v_col,
                      hq, grp, want_lse, out_dtype, name):
    sub = seq // dil
    nb = sub // BLOCK
    qw = hq * HEAD_DIM
    kw = (hq // grp) * HEAD_DIM
    view = qkv.reshape(batch * dil, sub, width)
    q_blk, k_blk, v_blk = q_col // qw, k_col // kw, v_col // kw
    assert q_col % qw == 0 and k_col % kw == 0 and v_col % kw == 0

    in_specs = [
        pl.BlockSpec((1, BLOCK, qw), lambda b, r, i: (b * dil + r, i, q_blk)),
        pl.BlockSpec((1, BLOCK, kw), lambda b, r, i: (b * dil + r, jnp.maximum(i - 1, 0), k_blk)),
        pl.BlockSpec((1, BLOCK, kw), lambda b, r, i: (b * dil + r, i, k_blk)),
        pl.BlockSpec((1, BLOCK, kw), lambda b, r, i: (b * dil + r, jnp.maximum(i - 1, 0), v_blk)),
        pl.BlockSpec((1, BLOCK, kw), lambda b, r, i: (b * dil + r, i, v_blk)),
        pl.BlockSpec((1, hq, BLOCK, 2 * BLOCK), lambda b, r, i: (jnp.minimum(i, 1), 0, 0, 0)),
    ]
    args = [view, view, view, view, view, bias]
    if sink is not None:
        in_specs.append(pl.BlockSpec(memory_space=pltpu.SMEM))
        args.append(sink)
    o_spec = pl.BlockSpec((1, BLOCK, qw), lambda b, r, i: (b, i, r))
    o_shape = jax.ShapeDtypeStruct((batch, sub, dil * qw), out_dtype)
    out_specs, out_shape = o_spec, o_shape
    if want_lse:
        out_specs = [o_spec, pl.BlockSpec((1, BLOCK, qw), lambda b, r, i: (b, i, r))]
        out_shape = [o_shape, jax.ShapeDtypeStruct((batch, sub, dil * qw), F32)]
    res = pl.pallas_call(
        functools.partial(_attn_kernel, hq=hq, grp=grp, has_sink=sink is not None, want_lse=want_lse),
        grid=(batch, dil, nb),
        in_specs=in_specs,
        out_specs=out_specs,
        out_shape=out_shape,
        compiler_params=_cparams(("parallel", "parallel", "arbitrary")),
        name=name,
    )(*args)
    if want_lse:
        return res[0].reshape(batch * seq, qw), res[1].reshape(batch * seq, qw)
    return res.reshape(batch * seq, qw)


def _gla_kernel(q_ref, k_ref, v_ref, alr_ref, r_ref, aw_ref, ab_ref, hn_ref, o_ref, st_ref):
    @pl.when(pl.program_id(1) == 0)
    def _():
        st_ref[...] = jnp.zeros_like(st_ref)

    ck = C_CHUNK
    z = jnp.dot(alr_ref[0].astype(BF16), aw_ref[...], preferred_element_type=F32) + ab_ref[...]
    log_a = (jnp.minimum(z, 0.0) - jnp.log1p(jnp.exp(-jnp.abs(z)))) * (1.0 / C_GATE_TAU)
    row = lax.broadcasted_iota(jnp.int32, log_a.shape, 0) & (ck - 1)
    cum = log_a
    shift = 1
    while shift < ck:
        cum = cum + jnp.where(row >= shift, pltpu.roll(cum, shift, 0), 0.0)
        shift *= 2
    q_all = q_ref[0].astype(F32) * (C_DK ** -0.5)
    k_all = k_ref[0].astype(F32)
    q_dec_all = (q_all * jnp.exp(cum)).astype(BF16)
    k_inv_all = (k_all * jnp.exp(-cum)).astype(BF16)
    ti = lax.broadcasted_iota(jnp.int32, (ck, ck), 0)
    si = lax.broadcasted_iota(jnp.int32, (ck, ck), 1)
    causal = ti >= si
    nt = (((1,), (1,)), ((), ()))
    tn = (((0,), (0,)), ((), ()))
    for c in range(GLA_CHUNKS_PER_STEP):
        rs = slice(c * ck, (c + 1) * ck)
        cum_c = cum[rs]
        last = cum_c[ck - 1:ck, :]
        q_dec, k_inv = q_dec_all[rs], k_inv_all[rs]
        k_end = (k_all[rs] * jnp.exp(last - cum_c)).astype(BF16)
        decay = jnp.exp(last)
        v = v_ref[0, rs, :]
        r = r_ref[0, rs, :]
        outs = []
        for h in range(C_HEADS):
            ks = slice(h * C_DK, (h + 1) * C_DK)
            vs = slice(h * C_DV, (h + 1) * C_DV)
            qd, ki, ke, vh = q_dec[:, ks], k_inv[:, ks], k_end[:, ks], v[:, vs]
            att = jnp.where(causal, lax.dot_general(qd, ki, nt, preferred_element_type=F32), 0.0)
            st = st_ref[h]
            o = jnp.dot(att.astype(BF16), vh, preferred_element_type=F32)
            o = o + lax.dot_general(qd, st.astype(BF16), nt, preferred_element_type=F32)
            kv_t = lax.dot_general(vh, ke, tn, preferred_element_type=F32)
            st_ref[h] = st * decay[:, ks] + kv_t
            o = o * lax.rsqrt(jnp.mean(o * o, axis=-1, keepdims=True) + RMS_EPS) * hn_ref[...]
            rh = r[:, vs]
            outs.append(o * (rh * jax.nn.sigmoid(rh)))
        o_ref[0, rs, :] = jnp.concatenate(outs, axis=-1).astype(o_ref.dtype)


def _gla(c_qkv, g_all, aw, ab, hn, *, batch, seq):
    rows = C_CHUNK * GLA_CHUNKS_PER_STEP
    qk_w = C_HEADS * C_DK
    v_w = C_HEADS * C_DV
    cv = c_qkv.reshape(batch, seq, C_WIDTH)
    gv = g_all.reshape(batch, seq, G_WIDTH)
    return pl.pallas_call(
        _gla_kernel,
        grid=(batch, seq // rows),
        in_specs=[
            pl.BlockSpec((1, rows, qk_w), lambda b, c: (b, c, 0)),
            pl.BlockSpec((1, rows, qk_w), lambda b, c: (b, c, 1)),
            pl.BlockSpec((1, rows, v_w), lambda b, c: (b, c, 1)),
            pl.BlockSpec((1, rows, LANES), lambda b, c: (b, c, G_ALR_COL // LANES)),
            pl.BlockSpec((1, rows, v_w), lambda b, c: (b, c, G_R_COL // v_w)),
            pl.BlockSpec((LANES, qk_w), lambda b, c: (0, 0)),
            pl.BlockSpec((1, qk_w), lambda b, c: (0, 0)),
            pl.BlockSpec((1, C_DV), lambda b, c: (0, 0)),
        ],
        out_specs=pl.BlockSpec((1, rows, v_w), lambda b, c: (b, c, 0)),
        out_shape=jax.ShapeDtypeStruct((batch, seq, v_w), BF16),
        scratch_shapes=[pltpu.VMEM((C_HEADS, C_DV, C_DK), F32)],
        compiler_params=_cparams(("parallel", "arbitrary")),
        name="gla",
    )(cv, cv, cv, gv, gv, aw, ab, hn).reshape(batch * seq, v_w)


def _merge_kernel(oa_ref, ob1_ref, ob2_ref, ob3_ref, l1_ref, l2_ref, l3_ref, oc_ref, x_ref,
                  gm_ref, wg_ref, wb_ref, wo_ref, gn_ref, xo_ref, xnt_ref):
    l1, l2, l3 = l1_ref[...], l2_ref[...], l3_ref[...]
    lm = jnp.maximum(jnp.maximum(l1, l2), l3)
    e1, e2, e3 = jnp.exp(l1 - lm), jnp.exp(l2 - lm), jnp.exp(l3 - lm)
    ob = (e1 * ob1_ref[...] + e2 * ob2_ref[...] + e3 * ob3_ref[...]) / (e1 + e2 + e3)
    branches = (oa_ref[...], ob.astype(BF16), oc_ref[...])
    x = x_ref[...]
    h = (x * lax.rsqrt(jnp.mean(x * x, axis=-1, keepdims=True) + RMS_EPS) * gm_ref[...]).astype(BF16)
    merged = None
    for n in range(N_BRANCHES):
        gate = jnp.dot(h, wg_ref[:, n * D_MODEL:(n + 1) * D_MODEL], preferred_element_type=F32)
        proj = jnp.dot(branches[n], wb_ref[n], preferred_element_type=F32)
        term = jax.nn.sigmoid(gate) * proj
        merged = term if merged is None else merged + term
    x = x + jnp.dot(merged.astype(BF16), wo_ref[...], preferred_element_type=F32)
    xo_ref[...] = x
    xn = x * lax.rsqrt(jnp.mean(x * x, axis=-1, keepdims=True) + RMS_EPS) * gn_ref[...]
    xnt_ref[...] = xn.T.astype(BF16)


def _merge(o_a, o_b, lse_b, o_c, x2, gm, wg, wb, wo, gn):
    rows = x2.shape[0]
    tm = min(512, rows)
    row_spec = lambda w: pl.BlockSpec((tm, w), lambda i: (i, 0))
    vec_spec = pl.BlockSpec((1, D_MODEL), lambda i: (0, 0))
    return pl.pallas_call(
        _merge_kernel,
        grid=(rows // tm,),
        in_specs=[row_spec(BRANCH_WIDTH)] * 8 + [
            row_spec(D_MODEL),
            vec_spec,
            pl.BlockSpec((D_MODEL, G_GATES), lambda i: (0, 0)),
            pl.BlockSpec((N_BRANCHES, BRANCH_WIDTH, D_MODEL), lambda i: (0, 0, 0)),
            pl.BlockSpec((D_MODEL, D_MODEL), lambda i: (0, 0)),
            vec_spec,
        ],
        out_specs=[row_spec(D_MODEL), pl.BlockSpec((D_MODEL, tm), lambda i: (0, i))],
        out_shape=[jax.ShapeDtypeStruct((rows, D_MODEL), F32),
                   jax.ShapeDtypeStruct((D_MODEL, rows), BF16)],
        compiler_params=_cparams(("parallel",)),
        name="merge",
    )(o_a, o_b[0], o_b[1], o_b[2], lse_b[0], lse_b[1], lse_b[2], o_c, x2, gm.reshape(1, D_MODEL),
      wg, wb, wo, gn.reshape(1, D_MODEL))


_PAIRS = tuple((i, j) for i in range(PEER_TOPK) for j in range(PEER_TOPK)
               if (i + 1) * (j + 1) <= PEER_TOPK)


def _sort16_desc(v):
    v = list(v)
    n = len(v)
    k = 2
    while k <= n:
        j = k // 2
        while j >= 1:
            for i in range(n):
                l = i ^ j
                if l > i:
                    hi, lo = jnp.maximum(v[i], v[l]), jnp.minimum(v[i], v[l])
                    v[i], v[l] = (hi, lo) if (i & k) == 0 else (lo, hi)
            j //= 2
        k *= 2
    return v


def _merge_top16(a, b):
    n = len(a)
    v = [jnp.maximum(a[i], b[n - 1 - i]) for i in range(n)]
    j = n // 2
    while j >= 1:
        for i in range(n):
            l = i ^ j
            if l > i:
                v[i], v[l] = jnp.maximum(v[i], v[l]), jnp.minimum(v[i], v[l])
        j //= 2
    return v


def _top16_of(vals):
    acc = None
    for g in range(len(vals) // PEER_TOPK):
        grp = _sort16_desc(vals[g * PEER_TOPK:(g + 1) * PEER_TOPK])
        acc = grp if acc is None else _merge_top16(acc, grp)
    return acc


def _peer_select_kernel(xnt_ref, wq_ref, kb_ref, cnt_ref, e1_ref, rank_ref, e2_ref,
                        sc_ref, rk_ref, eb_ref, *, tb):
    nk, nh, kk = PEER_KEYS, PEER_HEADS, PEER_TOPK
    xnt = xnt_ref[...]
    for p in range(2):
        qy = jnp.dot(wq_ref[p], xnt, preferred_element_type=F32).astype(BF16)
        sc_ref[p] = jnp.dot(kb_ref[p], qy, preferred_element_type=F32).reshape(nk, nh, tb)

    for c in range(tb // LANES):
        cs = slice(c * LANES, (c + 1) * LANES)
        s1s = _top16_of([sc_ref[0, n, :, cs] for n in range(nk)])
        s2s = _top16_of([sc_ref[1, n, :, cs] for n in range(nk)])
        cands = [s1s[i] + s2s[j] for (i, j) in _PAIRS]
        pad = [jnp.full_like(cands[0], -jnp.inf)] * (-len(cands) % kk)
        tau = _top16_of(cands + pad)[kk - 1]
        top = cands[0]
        zsum = None
        counts = [None] * kk
        for (i, j), cd in zip(_PAIRS, cands):
            sel = cd >= tau
            term = jnp.where(sel, jnp.exp(cd - top), 0.0)
            zsum = term if zsum is None else zsum + term
            one = jnp.where(sel, 1.0, 0.0)
            counts[i] = one if counts[i] is None else counts[i] + one
        inv_z = 1.0 / zsum

        def first_half(n, carry):
            s1 = sc_ref[0, n, :, cs]
            cnt = jnp.zeros_like(s1)
            for i in range(kk):
                cnt = jnp.where(s1 == s1s[i], counts[i], cnt)
            r0 = pl.multiple_of(n * nh, nh)
            cnt_ref[pl.ds(r0, nh), cs] = cnt
            e1_ref[pl.ds(r0, nh), cs] = jnp.exp(s1 - s1s[0])
            return carry

        def second_half(n, carry):
            s2 = sc_ref[1, n, :, cs]
            rank = jnp.full_like(s2, float(kk))
            for j in range(kk):
                rank = jnp.where(s2 == s2s[j], float(j), rank)
            r0 = pl.multiple_of(n * nh, nh)
            rk_ref[pl.ds(r0, nh), :] = rank
            eb_ref[pl.ds(r0, nh), :] = jnp.exp(s2 - s2s[0]) * inv_z
            return carry

        lax.fori_loop(0, nk, first_half, 0, unroll=4)
        lax.fori_loop(0, nk, second_half, 0, unroll=4)
        for h in range(nh):
            rank_ref[h * nk:(h + 1) * nk, cs] = rk_ref[pl.ds(h, nk, stride=nh), :].astype(BF16)
            e2_ref[h * nk:(h + 1) * nk, cs] = eb_ref[pl.ds(h, nk, stride=nh), :].astype(BF16)


def _peer_select(xnt, wq_t, k_big):
    d, rows = xnt.shape
    tb = min(256, rows)
    nrow = PEER_KEYS * PEER_HEADS
    tab_spec = pl.BlockSpec((nrow, tb), lambda i: (0, i))
    return pl.pallas_call(
        functools.partial(_peer_select_kernel, tb=tb),
        grid=(rows // tb,),
        in_specs=[
            pl.BlockSpec((d, tb), lambda i: (0, i)),
            pl.BlockSpec((2, nrow, d), lambda i: (0, 0, 0)),
            pl.BlockSpec((2, nrow, nrow), lambda i: (0, 0, 0)),
        ],
        out_specs=[tab_spec] * 4,
        out_shape=[jax.ShapeDtypeStruct((nrow, rows), dt) for dt in (F32, F32, BF16, BF16)],
        scratch_shapes=[
            pltpu.VMEM((2, PEER_KEYS, PEER_HEADS, tb), F32),
            pltpu.VMEM((nrow, LANES), F32),
            pltpu.VMEM((nrow, LANES), F32),
        ],
        compiler_params=_cparams(("parallel",)),
        name="peer_select",
    )(xnt, wq_t, k_big)


def _gelu_tanh(x):
    c = math.sqrt(2.0 / math.pi)
    return 0.5 * x * (1.0 + jnp.tanh(c * (x + 0.044715 * (x * x * x))))


def _peer_dense_kernel(xnt_ref, u_ref, vt_ref, cnt_ref, e1_ref, rank_ref, e2_ref, x_ref, gn_ref,
                       o_ref, acc_ref, ht_ref, wn_ref, wo_ref, *, tb, eb, nj, final_norm):
    s = pl.program_id(0)
    nk, nh = PEER_KEYS, PEER_HEADS
    pk = 16
    d = acc_ref.shape[0]
    pj = s - 1
    hidden_rows = (0, eb // 4, eb // 2, eb)
    out_halves = 2

    @pl.when(s == 0)
    def _():
        wo_ref[...] = jnp.zeros_like(wo_ref)

    @pl.when(s > 0)
    def _():
        wo_ref[...] = wn_ref[...]

    @pl.when((s == 0) | (lax.rem(jnp.maximum(pj, 0), nj) == 0))
    def _():
        acc_ref[...] = jnp.zeros_like(acc_ref)

    def hidden(i):
        rows = slice(hidden_rows[i], hidden_rows[i + 1])
        ht_ref[rows, :] = jnp.dot(u_ref[rows, :], xnt_ref[...], preferred_element_type=F32)

    def gate(al):
        rows = slice(al * nk, (al + 1) * nk)
        for c in range(tb // LANES):
            cs = slice(c * LANES, (c + 1) * LANES)
            cnt8 = cnt_ref[al * nh:(al + 1) * nh, cs]
            e18 = e1_ref[al * nh:(al + 1) * nh, cs]
            act = _gelu_tanh(ht_ref[rows, cs].astype(BF16)).reshape(nk // pk, pk, LANES)
            g = jnp.zeros((nk // pk, pk, LANES), BF16)
            for h in range(nh):
                cb = jnp.broadcast_to(cnt8[h:h + 1, :], (pk, LANES)).astype(BF16)
                eb_ = jnp.broadcast_to(e18[h:h + 1, :], (pk, LANES)).astype(BF16)
                rk = rank_ref[h * nk:(h + 1) * nk, cs].reshape(nk // pk, pk, LANES)
                e2 = e2_ref[h * nk:(h + 1) * nk, cs].reshape(nk // pk, pk, LANES)
                g = g + jnp.where(rk < cb[None], e2, jnp.zeros_like(e2)) * eb_[None]
            wn_ref[rows, cs] = (act * g).reshape(nk, LANES)

    def project(i):
        rows = slice(i * d // out_halves, (i + 1) * d // out_halves)
        acc_ref[rows, :] += jnp.dot(vt_ref[rows, :], wo_ref[...], preferred_element_type=F32)

    slabs = eb // nk
    hidden(0)
    for i in range(len(hidden_rows) - 1):
        if i + 2 < len(hidden_rows):
            hidden(i + 1)
        for al in range(hidden_rows[i] // nk, hidden_rows[i + 1] // nk):
            gate(al)
            if (al + 1) % (slabs // out_halves) == 0:
                project(al // (slabs // out_halves))

    @pl.when((s >= 1) & (lax.rem(jnp.maximum(pj, 0), nj) == nj - 1))
    def _():
        x = x_ref[...] + acc_ref[...].T
        if final_norm:
            x = x * lax.rsqrt(jnp.mean(x * x, axis=-1, keepdims=True) + RMS_EPS) * gn_ref[...]
        o_ref[...] = x


def _peer_dense(xnt, u, vt, cnt, e1, rank, e2, x2, gn, final_norm):
    d, rows = xnt.shape
    ne = u.shape[0]
    tb = min(512, rows)
    eb = 1024
    nj = ne // eb
    npairs = (rows // tb) * nj
    nrow = PEER_KEYS * PEER_HEADS
    arow = (eb // PEER_KEYS) * PEER_HEADS
    cur = lambda s: jnp.minimum(s, npairs - 1)
    prev = lambda s: jnp.maximum(s - 1, 0)
    return pl.pallas_call(
        functools.partial(_peer_dense_kernel, tb=tb, eb=eb, nj=nj, final_norm=final_norm),
        grid=(npairs + 1,),
        in_specs=[
            pl.BlockSpec((d, tb), lambda s: (0, cur(s) // nj)),
            pl.BlockSpec((eb, d), lambda s: (cur(s) % nj, 0)),
            pl.BlockSpec((d, eb), lambda s: (0, prev(s) % nj)),
            pl.BlockSpec((arow, tb), lambda s: (cur(s) % nj, cur(s) // nj)),
            pl.BlockSpec((arow, tb), lambda s: (cur(s) % nj, cur(s) // nj)),
            pl.BlockSpec((nrow, tb), lambda s: (0, cur(s) // nj)),
            pl.BlockSpec((nrow, tb), lambda s: (0, cur(s) // nj)),
            pl.BlockSpec((tb, d), lambda s: (prev(s) // nj, 0)),
            pl.BlockSpec((1, d), lambda s: (0, 0)),
        ],
        out_specs=pl.BlockSpec((tb, d), lambda s: (prev(s) // nj, 0)),
        out_shape=jax.ShapeDtypeStruct((rows, d), F32),
        scratch_shapes=[
            pltpu.VMEM((d, tb), F32),
            pltpu.VMEM((eb, tb), F32),
            pltpu.VMEM((eb, tb), BF16),
            pltpu.VMEM((eb, tb), BF16),
        ],
        compiler_params=_cparams(("arbitrary",)),
        name="peer_dense",
    )(xnt, u, vt, cnt, e1, rank, e2, x2, gn.reshape(1, d))


def _t5_bucket(dist):
    max_exact = N_BUCKETS // 2
    large = max_exact + (jnp.log(jnp.maximum(dist, 1).astype(F32) / max_exact)
                         / math.log(BUCKET_MAX_DIST / max_exact) * (N_BUCKETS - max_exact)).astype(jnp.int32)
    large = jnp.minimum(large, N_BUCKETS - 1)
    return jnp.where(dist < max_exact, dist, large)


def _rel_bias(table_cols, dilation):
    qi = jnp.arange(BLOCK)[:, None]
    kj = jnp.arange(2 * BLOCK)[None, :]
    dist = jnp.maximum(qi + BLOCK - kj, 0) * dilation
    onehot = (_t5_bucket(dist)[..., None] == jnp.arange(N_BUCKETS)).astype(F32)
    return jnp.einsum('qkb,bh->hqk', onehot, table_cols.astype(F32), precision=lax.Precision.HIGHEST)


def _split_w_in(w):
    hd = HEAD_DIM
    sizes = [A_Q_HEADS * hd, A_KV_HEADS * hd, A_KV_HEADS * hd] + [B_HEADS * hd] * 9 + [
        C_HEADS * C_DK, C_HEADS * C_DK, C_HEADS * C_DV, C_GATE_RANK, C_HEADS * C_DV, N_BRANCHES * D_MODEL]
    offs = np.concatenate([[0], np.cumsum(sizes)])
    col = lambda a, b: w[:, int(offs[a]):int(offs[b])].astype(BF16)
    pad = jnp.zeros((w.shape[0], G_WIDTH - G_ALR_COL - C_GATE_RANK), BF16)
    w_r = jnp.concatenate([col(16, 17), col(15, 16), pad], axis=1)
    return col(0, 3), [col(3 + 3 * g, 6 + 3 * g) for g in range(3)], col(12, 15), w_r, col(17, 18)


def _peer_key_matrix(keys):
    nh, _, nk, c = keys.shape
    eye = jnp.eye(nh, dtype=keys.dtype)
    big = jnp.einsum('hpnc,hg->pnhgc', keys, eye)
    return big.reshape(2, nk * nh, nh * c).astype(BF16)


def _residue_major(x2, batch, seq, dil):
    if dil == 1:
        return x2
    d = x2.shape[-1]
    return x2.reshape(batch, seq // dil, dil, d).transpose(0, 2, 1, 3).reshape(batch * seq, d)


def kernel(x, w_in, attn_sinks, gla_alpha_w, gla_alpha_b, gla_head_norm, w_branch, w_out, norm_mix,
           norm_ffn, peer_wq, peer_keys, peer_u, peer_v, rel_bias_table, norm_final):
    batch, seq, d = x.shape
    rows = batch * seq
    depth = w_in.shape[0]
    hw = B_HEADS * HEAD_DIM

    bias_a = _banded_bias(_rel_bias(rel_bias_table[:, :A_Q_HEADS], 1), A_WINDOW - 1)
    bias_b = [_banded_bias(
        _rel_bias(rel_bias_table[:, A_Q_HEADS + i * B_HEADS:A_Q_HEADS + (i + 1) * B_HEADS], dil), window // dil)
        for i, (window, dil) in enumerate(B_PATTERNS)]

    x2 = x.reshape(rows, d)
    for l in range(depth):
        w_a, w_bs, w_c, w_r, w_gates = _split_w_in(w_in[l])
        a_qkv = _norm_matmul(x2, norm_mix[l], w_a, A_WIDTH, BF16, "proj_a")
        c_qkv = _norm_matmul(x2, norm_mix[l], w_c, C_WIDTH, BF16, "proj_c")
        g_all = _norm_matmul(x2, norm_mix[l], w_r, G_WIDTH, F32, "proj_r")

        o_a = _banded_attention(
            a_qkv, bias_a, attn_sinks[l], batch=batch, seq=seq, dil=1, width=A_WIDTH,
            q_col=0, k_col=A_Q_HEADS * HEAD_DIM, v_col=(A_Q_HEADS + A_KV_HEADS) * HEAD_DIM,
            hq=A_Q_HEADS, grp=A_Q_HEADS // A_KV_HEADS, want_lse=False, out_dtype=BF16, name="attn_a")
        o_b, lse_b = [], []
        for gi, (window, dil) in enumerate(B_PATTERNS):
            b_qkv = _norm_matmul(_residue_major(x2, batch, seq, dil), norm_mix[l], w_bs[gi], 3 * hw, BF16,
                                 "proj_b%d" % gi)
            o, lse = _banded_attention(
                b_qkv, bias_b[gi], None, batch=batch, seq=seq, dil=dil, width=3 * hw,
                q_col=0, k_col=hw, v_col=2 * hw, hq=B_HEADS, grp=1,
                want_lse=True, out_dtype=F32, name="attn_b%d" % gi)
            o_b.append(o)
            lse_b.append(lse)

        aw = jnp.zeros((LANES, C_HEADS * C_DK), BF16).at[:C_GATE_RANK].set(gla_alpha_w[l].astype(BF16))
        o_c = _gla(c_qkv, g_all, aw, gla_alpha_b[l].reshape(1, -1), gla_head_norm[l].reshape(1, -1),
                   batch=batch, seq=seq)

        x2, xnt = _merge(o_a, o_b, lse_b, o_c, x2, norm_mix[l], w_gates, w_branch[l].astype(BF16),
                         w_out[l].astype(BF16), norm_ffn[l])

        wq_t = peer_wq[l].reshape(d, PEER_HEADS, 2, PEER_DKEY // 2).transpose(2, 1, 3, 0)
        wq_t = wq_t.reshape(2, PEER_HEADS * (PEER_DKEY // 2), d).astype(BF16)
        cnt, e1, rank, e2 = _peer_select(xnt, wq_t, _peer_key_matrix(peer_keys[l]))
        last = l == depth - 1
        x2 = _peer_dense(xnt, peer_u[l].astype(BF16), peer_v[l].T.astype(BF16), cnt, e1, rank, e2,
                         x2, norm_final, final_norm=last)
    return x2.reshape(batch, seq, d)
```

```python
import functools
import math

import numpy as np
import jax
import jax.numpy as jnp
from jax import lax
from jax.experimental import pallas as pl
from jax.experimental.pallas import tpu as pltpu

F32 = jnp.float32
BF16 = jnp.bfloat16

D_MODEL = 1024
HEAD_DIM = 64
BLOCK = 128
A_Q_HEADS = 8
A_KV_HEADS = 2
A_WINDOW = 128
B_PATTERNS = ((128, 1), (512, 4), (2048, 16))
B_HEADS = 8
C_HEADS = 4
C_DK = 64
C_DV = 128
C_GATE_RANK = 16
C_GATE_TAU = 16.0
C_CHUNK = 64
BRANCH_WIDTH = 512
N_BRANCHES = 3
N_BUCKETS = 32
BUCKET_MAX_DIST = 2048
PEER_HEADS = 8
PEER_KEYS = 128
PEER_N_EXPERTS = PEER_KEYS * PEER_KEYS
PEER_TOPK = 16
PEER_DKEY = 256
RMS_EPS = 1e-6
NEG_INF = -1e30

LANES = 128
VMEM_LIMIT = 56 * 1024 * 1024

A_WIDTH = (A_Q_HEADS + 2 * A_KV_HEADS) * HEAD_DIM
B_WIDTH = len(B_PATTERNS) * 3 * B_HEADS * HEAD_DIM
C_WIDTH = 2 * C_HEADS * C_DK + C_HEADS * C_DV
G_GATES = N_BRANCHES * D_MODEL
G_WIDTH = C_HEADS * C_DV + LANES
G_R_COL = 0
G_ALR_COL = C_HEADS * C_DV
GLA_CHUNKS_PER_STEP = 4
ATTN_ROWS_PER_STEP = 2 * BLOCK


def _cparams(sem):
    return pltpu.CompilerParams(dimension_semantics=sem, vmem_limit_bytes=VMEM_LIMIT)


def _norm_matmul_kernel(x_ref, g_ref, w_ref, o_ref, h_ref):
    @pl.when(pl.program_id(1) == 0)
    def _():
        x = x_ref[...]
        ms = jnp.mean(x * x, axis=-1, keepdims=True)
        h_ref[...] = (x * lax.rsqrt(ms + RMS_EPS) * g_ref[...]).astype(BF16)

    o_ref[...] = jnp.dot(h_ref[...], w_ref[...], preferred_element_type=F32).astype(o_ref.dtype)


def _norm_matmul(x2, g, w, tn, out_dtype, name):
    rows, d = x2.shape
    n = w.shape[1]
    tm = min(1024, rows)
    return pl.pallas_call(
        _norm_matmul_kernel,
        grid=(rows // tm, n // tn),
        in_specs=[
            pl.BlockSpec((tm, d), lambda i, j: (i, 0)),
            pl.BlockSpec((1, d), lambda i, j: (0, 0)),
            pl.BlockSpec((d, tn), lambda i, j: (0, j)),
        ],
        out_specs=pl.BlockSpec((tm, tn), lambda i, j: (i, j)),
        out_shape=jax.ShapeDtypeStruct((rows, n), out_dtype),
        scratch_shapes=[pltpu.VMEM((tm, d), BF16)],
        compiler_params=_cparams(("parallel", "arbitrary")),
        name=name,
    )(x2, g.reshape(1, d), w)


def _attn_kernel(*refs, hq, grp, has_sink, want_lse, nsb):
    q_ref, kp_ref, kc_ref, vp_ref, vc_ref, bias_ref = refs[:6]
    pos = 6
    sink_ref = None
    if has_sink:
        sink_ref = refs[pos]
        pos += 1
    o_ref = refs[pos]
    lse_ref = refs[pos + 1] if want_lse else None

    first = jnp.minimum(pl.program_id(2), 1)
    k = jnp.concatenate([kp_ref[0], kc_ref[0]], axis=0)
    v = jnp.concatenate([vp_ref[0], vc_ref[0]], axis=0)
    for sb in range(nsb):
        q = q_ref[0, sb * BLOCK:(sb + 1) * BLOCK, :] * (HEAD_DIM ** -0.5)
        ks = k[sb * BLOCK:(sb + 2) * BLOCK]
        vs = v[sb * BLOCK:(sb + 2) * BLOCK]
        outs, lses = [], []
        for h in range(hq):
            kvh = h // grp
            qh = q[:, h * HEAD_DIM:(h + 1) * HEAD_DIM]
            kh = ks[:, kvh * HEAD_DIM:(kvh + 1) * HEAD_DIM]
            vh = vs[:, kvh * HEAD_DIM:(kvh + 1) * HEAD_DIM]
            s = lax.dot_general(qh, kh, (((1,), (1,)), ((), ())), preferred_element_type=F32)
            bias = bias_ref[first, h] if sb == 0 else bias_ref[1, h]
            s = s + bias
            m = jnp.max(s, axis=-1, keepdims=True)
            if has_sink:
                m = jnp.maximum(m, sink_ref[h])
            p = jnp.exp(s - m)
            denom = jnp.sum(p, axis=-1, keepdims=True)
            if has_sink:
                denom = denom + jnp.exp(sink_ref[h] - m)
            o = jnp.dot(p.astype(BF16), vh, preferred_element_type=F32) / denom
            outs.append(o)
            if want_lse:
                lses.append(jnp.broadcast_to(m + jnp.log(denom), (BLOCK, HEAD_DIM)))
        o_ref[0, sb * BLOCK:(sb + 1) * BLOCK, :] = jnp.concatenate(outs, axis=-1).astype(o_ref.dtype)
        if want_lse:
            lse_ref[0, sb * BLOCK:(sb + 1) * BLOCK, :] = jnp.concatenate(lses, axis=-1)


def _banded_bias(bias, max_dist):
    qi = np.arange(BLOCK)[:, None]
    kj = np.arange(2 * BLOCK)[None, :]
    dist = qi + BLOCK - kj
    in_window = (dist >= 0) & (dist <= max_dist)
    first = in_window & (kj >= BLOCK)
    return jnp.stack([jnp.where(first, bias, NEG_INF), jnp.where(in_window, bias, NEG_INF)])


def _banded_attention(qkv, bias, sink, *, batch, seq, dil, width, q_col, k_col, v_col,
                      hq, grp, want_lse, out_dtype, name):
    sub = seq // dil
    rows = min(ATTN_ROWS_PER_STEP, sub)
    nsb = rows // BLOCK
    qw = hq * HEAD_DIM
    kw = (hq // grp) * HEAD_DIM
    view = qkv.reshape(batch * dil, sub, width)
    q_blk, k_blk, v_blk = q_col // qw, k_col // kw, v_col // kw
    assert q_col % qw == 0 and k_col % kw == 0 and v_col % kw == 0

    prev_spec = lambda c: pl.BlockSpec((1, BLOCK, kw), lambda b, r, i: (b * dil + r, jnp.maximum(i * nsb - 1, 0), c))
    cur_spec = lambda w, c: pl.BlockSpec((1, rows, w), lambda b, r, i: (b * dil + r, i, c))
    in_specs = [
        cur_spec(qw, q_blk), prev_spec(k_blk), cur_spec(kw, k_blk), prev_spec(v_blk), cur_spec(kw, v_blk),
        pl.BlockSpec((2, hq, BLOCK, 2 * BLOCK), lambda b, r, i: (0, 0, 0, 0)),
    ]
    args = [view, view, view, view, view, bias]
    if sink is not None:
        in_specs.append(pl.BlockSpec(memory_space=pltpu.SMEM))
        args.append(sink)
    o_spec = pl.BlockSpec((1, rows, qw), lambda b, r, i: (b, i, r))
    o_shape = jax.ShapeDtypeStruct((batch, sub, dil * qw), out_dtype)
    out_specs, out_shape = o_spec, o_shape
    if want_lse:
        out_specs = [o_spec, pl.BlockSpec((1, rows, qw), lambda b, r, i: (b, i, r))]
        out_shape = [o_shape, jax.ShapeDtypeStruct((batch, sub, dil * qw), F32)]
    res = pl.pallas_call(
        functools.partial(_attn_kernel, hq=hq, grp=grp, has_sink=sink is not None, want_lse=want_lse, nsb=nsb),
        grid=(batch, dil, sub // rows),
        in_specs=in_specs,
        out_specs=out_specs,
        out_shape=out_shape,
        compiler_params=_cparams(("parallel", "parallel", "arbitrary")),
        name=name,
    )(*args)
    if want_lse:
        return res[0].reshape(batch * seq, qw), res[1].reshape(batch * seq, qw)
    return res.reshape(batch * seq, qw)


def _gla_kernel(q_ref, k_ref, v_ref, alr_ref, r_ref, aw_ref, ab_ref, hn_ref, o_ref, st_ref):
    @pl.when(pl.program_id(1) == 0)
    def _():
        st_ref[...] = jnp.zeros_like(st_ref)

    ck = C_CHUNK
    z = jnp.dot(alr_ref[0].astype(BF16), aw_ref[...], preferred_element_type=F32) + ab_ref[...]
    log_a = (jnp.minimum(z, 0.0) - jnp.log1p(jnp.exp(-jnp.abs(z)))) * (1.0 / C_GATE_TAU)
    row = lax.broadcasted_iota(jnp.int32, log_a.shape, 0) & (ck - 1)
    cum = log_a
    shift = 1
    while shift < ck:
        cum = cum + jnp.where(row >= shift, pltpu.roll(cum, shift, 0), 0.0)
        shift *= 2
    q_all = q_ref[0].astype(F32) * (C_DK ** -0.5)
    k_all = k_ref[0].astype(F32)
    q_dec_all = (q_all * jnp.exp(cum)).astype(BF16)
    k_inv_all = (k_all * jnp.exp(-cum)).astype(BF16)
    ti = lax.broadcasted_iota(jnp.int32, (ck, ck), 0)
    si = lax.broadcasted_iota(jnp.int32, (ck, ck), 1)
    causal = ti >= si
    nt = (((1,), (1,)), ((), ()))
    tn = (((0,), (0,)), ((), ()))
    for c in range(GLA_CHUNKS_PER_STEP):
        rs = slice(c * ck, (c + 1) * ck)
        cum_c = cum[rs]
        last = cum_c[ck - 1:ck, :]
        q_dec, k_inv = q_dec_all[rs], k_inv_all[rs]
        k_end = (k_all[rs] * jnp.exp(last - cum_c)).astype(BF16)
        decay = jnp.exp(last)
        v = v_ref[0, rs, :]
        r = r_ref[0, rs, :]
        outs = []
        for h in range(C_HEADS):
            ks = slice(h * C_DK, (h + 1) * C_DK)
            vs = slice(h * C_DV, (h + 1) * C_DV)
            qd, ki, ke, vh = q_dec[:, ks], k_inv[:, ks], k_end[:, ks], v[:, vs]
            att = jnp.where(causal, lax.dot_general(qd, ki, nt, preferred_element_type=F32), 0.0)
            st = st_ref[h]
            o = jnp.dot(att.astype(BF16), vh, preferred_element_type=F32)
            o = o + lax.dot_general(qd, st.astype(BF16), nt, preferred_element_type=F32)
            kv_t = lax.dot_general(vh, ke, tn, preferred_element_type=F32)
            st_ref[h] = st * decay[:, ks] + kv_t
            o = o * lax.rsqrt(jnp.mean(o * o, axis=-1, keepdims=True) + RMS_EPS) * hn_ref[...]
            rh = r[:, vs]
            outs.append(o * (rh * jax.nn.sigmoid(rh)))
        o_ref[0, rs, :] = jnp.concatenate(outs, axis=-1).astype(o_ref.dtype)


def _gla(c_qkv, g_all, aw, ab, hn, *, batch, seq):
    rows = C_CHUNK * GLA_CHUNKS_PER_STEP
    qk_w = C_HEADS * C_DK
    v_w = C_HEADS * C_DV
    cv = c_qkv.reshape(batch, seq, C_WIDTH)
    gv = g_all.reshape(batch, seq, G_WIDTH)
    return pl.pallas_call(
        _gla_kernel,
        grid=(batch, seq // rows),
        in_specs=[
            pl.BlockSpec((1, rows, qk_w), lambda b, c: (b, c, 0)),
            pl.BlockSpec((1, rows, qk_w), lambda b, c: (b, c, 1)),
            pl.BlockSpec((1, rows, v_w), lambda b, c: (b, c, 1)),
            pl.BlockSpec((1, rows, LANES), lambda b, c: (b, c, G_ALR_COL // LANES)),
            pl.BlockSpec((1, rows, v_w), lambda b, c: (b, c, G_R_COL // v_w)),
            pl.BlockSpec((LANES, qk_w), lambda b, c: (0, 0)),
            pl.BlockSpec((1, qk_w), lambda b, c: (0, 0)),
            pl.BlockSpec((1, C_DV), lambda b, c: (0, 0)),
        ],
        out_specs=pl.BlockSpec((1, rows, v_w), lambda b, c: (b, c, 0)),
        out_shape=jax.ShapeDtypeStruct((batch, seq, v_w), BF16),
        scratch_shapes=[pltpu.VMEM((C_HEADS, C_DV, C_DK), F32)],
        compiler_params=_cparams(("parallel", "arbitrary")),
        name="gla",
    )(cv, cv, cv, gv, gv, aw, ab, hn).reshape(batch * seq, v_w)


def _merge_kernel(oa_ref, ob1_ref, ob2_ref, ob3_ref, l1_ref, l2_ref, l3_ref, oc_ref, x_ref,
                  gm_ref, wg_ref, wb_ref, wo_ref, gn_ref, xo_ref, xnt_ref):
    l1, l2, l3 = l1_ref[...], l2_ref[...], l3_ref[...]
    lm = jnp.maximum(jnp.maximum(l1, l2), l3)
    e1, e2, e3 = jnp.exp(l1 - lm), jnp.exp(l2 - lm), jnp.exp(l3 - lm)
    ob = (e1 * ob1_ref[...] + e2 * ob2_ref[...] + e3 * ob3_ref[...]) / (e1 + e2 + e3)
    branches = (oa_ref[...], ob.astype(BF16), oc_ref[...])
    x = x_ref[...]
    h = (x * lax.rsqrt(jnp.mean(x * x, axis=-1, keepdims=True) + RMS_EPS) * gm_ref[...]).astype(BF16)
    merged = None
    for n in range(N_BRANCHES):
        gate = jnp.dot(h, wg_ref[:, n * D_MODEL:(n + 1) * D_MODEL], preferred_element_type=F32)
        proj = jnp.dot(branches[n], wb_ref[n], preferred_element_type=F32)
        term = jax.nn.sigmoid(gate) * proj
        merged = term if merged is None else merged + term
    x = x + jnp.dot(merged.astype(BF16), wo_ref[...], preferred_element_type=F32)
    xo_ref[...] = x
    xn = x * lax.rsqrt(jnp.mean(x * x, axis=-1, keepdims=True) + RMS_EPS) * gn_ref[...]
    xnt_ref[...] = xn.T.astype(BF16)


def _merge(o_a, o_b, lse_b, o_c, x2, gm, wg, wb, wo, gn):
    rows = x2.shape[0]
    tm = min(512, rows)
    row_spec = lambda w: pl.BlockSpec((tm, w), lambda i: (i, 0))
    vec_spec = pl.BlockSpec((1, D_MODEL), lambda i: (0, 0))
    return pl.pallas_call(
        _merge_kernel,
        grid=(rows // tm,),
        in_specs=[row_spec(BRANCH_WIDTH)] * 8 + [
            row_spec(D_MODEL),
            vec_spec,
            pl.BlockSpec((D_MODEL, G_GATES), lambda i: (0, 0)),
            pl.BlockSpec((N_BRANCHES, BRANCH_WIDTH, D_MODEL), lambda i: (0, 0, 0)),
            pl.BlockSpec((D_MODEL, D_MODEL), lambda i: (0, 0)),
            vec_spec,
        ],
        out_specs=[row_spec(D_MODEL), pl.BlockSpec((D_MODEL, tm), lambda i: (0, i))],
        out_shape=[jax.ShapeDtypeStruct((rows, D_MODEL), F32),
                   jax.ShapeDtypeStruct((D_MODEL, rows), BF16)],
        compiler_params=_cparams(("parallel",)),
        name="merge",
    )(o_a, o_b[0], o_b[1], o_b[2], lse_b[0], lse_b[1], lse_b[2], o_c, x2, gm.reshape(1, D_MODEL),
      wg, wb, wo, gn.reshape(1, D_MODEL))


_PAIRS = tuple((i, j) for i in range(PEER_TOPK) for j in range(PEER_TOPK)
               if (i + 1) * (j + 1) <= PEER_TOPK)


def _sort16_desc(v):
    v = list(v)
    n = len(v)
    k = 2
    while k <= n:
        j = k // 2
        while j >= 1:
            for i in range(n):
                l = i ^ j
                if l > i:
                    hi, lo = jnp.maximum(v[i], v[l]), jnp.minimum(v[i], v[l])
                    v[i], v[l] = (hi, lo) if (i & k) == 0 else (lo, hi)
            j //= 2
        k *= 2
    return v


def _merge_top16(a, b):
    n = len(a)
    v = [jnp.maximum(a[i], b[n - 1 - i]) for i in range(n)]
    j = n // 2
    while j >= 1:
        for i in range(n):
            l = i ^ j
            if l > i:
                v[i], v[l] = jnp.maximum(v[i], v[l]), jnp.minimum(v[i], v[l])
        j //= 2
    return v


def _top16_of(vals):
    acc = None
    for g in range(len(vals) // PEER_TOPK):
        grp = _sort16_desc(vals[g * PEER_TOPK:(g + 1) * PEER_TOPK])
        acc = grp if acc is None else _merge_top16(acc, grp)
    return acc


def _peer_select_kernel(xnt_ref, wq_ref, kb_ref, cnt_ref, e1_ref, rank_ref, e2_ref,
                        sc_ref, rk_ref, eb_ref, *, tb):
    nk, nh, kk = PEER_KEYS, PEER_HEADS, PEER_TOPK
    xnt = xnt_ref[...]
    for p in range(2):
        qy = jnp.dot(wq_ref[p], xnt, preferred_element_type=F32).astype(BF16)
        sc_ref[p] = jnp.dot(kb_ref[p], qy, preferred_element_type=F32).reshape(nk, nh, tb)

    for c in range(tb // LANES):
        cs = slice(c * LANES, (c + 1) * LANES)
        s1s = _top16_of([sc_ref[0, n, :, cs] for n in range(nk)])
        s2s = _top16_of([sc_ref[1, n, :, cs] for n in range(nk)])
        cands = [s1s[i] + s2s[j] for (i, j) in _PAIRS]
        pad = [jnp.full_like(cands[0], -jnp.inf)] * (-len(cands) % kk)
        tau = _top16_of(cands + pad)[kk - 1]
        top = cands[0]
        zsum = None
        counts = [None] * kk
        for (i, j), cd in zip(_PAIRS, cands):
            sel = cd >= tau
            term = jnp.where(sel, jnp.exp(cd - top), 0.0)
            zsum = term if zsum is None else zsum + term
            one = jnp.where(sel, 1.0, 0.0)
            counts[i] = one if counts[i] is None else counts[i] + one
        inv_z = 1.0 / zsum

        def first_half(n, carry):
            s1 = sc_ref[0, n, :, cs]
            cnt = jnp.zeros_like(s1)
            for i in range(kk):
                cnt = jnp.where(s1 == s1s[i], counts[i], cnt)
            r0 = pl.multiple_of(n * nh, nh)
            cnt_ref[pl.ds(r0, nh), cs] = cnt
            e1_ref[pl.ds(r0, nh), cs] = jnp.exp(s1 - s1s[0])
            return carry

        def second_half(n, carry):
            s2 = sc_ref[1, n, :, cs]
            rank = jnp.full_like(s2, float(kk))
            for j in range(kk):
                rank = jnp.where(s2 == s2s[j], float(j), rank)
            r0 = pl.multiple_of(n * nh, nh)
            rk_ref[pl.ds(r0, nh), :] = rank
            eb_ref[pl.ds(r0, nh), :] = jnp.exp(s2 - s2s[0]) * inv_z
            return carry

        lax.fori_loop(0, nk, first_half, 0, unroll=4)
        lax.fori_loop(0, nk, second_half, 0, unroll=4)
        for h in range(nh):
            rank_ref[h * nk:(h + 1) * nk, cs] = rk_ref[pl.ds(h, nk, stride=nh), :].astype(BF16)
            e2_ref[h * nk:(h + 1) * nk, cs] = eb_ref[pl.ds(h, nk, stride=nh), :].astype(BF16)


def _peer_select(xnt, wq_t, k_big):
    d, rows = xnt.shape
    tb = min(256, rows)
    nrow = PEER_KEYS * PEER_HEADS
    tab_spec = pl.BlockSpec((nrow, tb), lambda i: (0, i))
    return pl.pallas_call(
        functools.partial(_peer_select_kernel, tb=tb),
        grid=(rows // tb,),
        in_specs=[
            pl.BlockSpec((d, tb), lambda i: (0, i)),
            pl.BlockSpec((2, nrow, d), lambda i: (0, 0, 0)),
            pl.BlockSpec((2, nrow, nrow), lambda i: (0, 0, 0)),
        ],
        out_specs=[tab_spec] * 4,
        out_shape=[jax.ShapeDtypeStruct((nrow, rows), dt) for dt in (F32, F32, BF16, BF16)],
        scratch_shapes=[
            pltpu.VMEM((2, PEER_KEYS, PEER_HEADS, tb), F32),
            pltpu.VMEM((nrow, LANES), F32),
            pltpu.VMEM((nrow, LANES), F32),
        ],
        compiler_params=_cparams(("parallel",)),
        name="peer_select",
    )(xnt, wq_t, k_big)


def _gelu_tanh(x):
    c = math.sqrt(2.0 / math.pi)
    return 0.5 * x * (1.0 + jnp.tanh(c * (x + 0.044715 * (x * x * x))))


def _peer_dense_kernel(xnt_ref, u_ref, vt_ref, cnt_ref, e1_ref, rank_ref, e2_ref, x_ref, gn_ref,
                       o_ref, acc_ref, ht_ref, w_ref, *, tb, eb, slab, final_norm):
    j = pl.program_id(1)
    nk, nh = PEER_KEYS, PEER_HEADS
    pk = 16
    n_slab = eb // slab

    @pl.when(j == 0)
    def _():
        acc_ref[...] = jnp.zeros_like(acc_ref)

    def hidden(s):
        rows = slice(s * slab, (s + 1) * slab)
        ht_ref[rows, :] = jnp.dot(u_ref[rows, :], xnt_ref[...], preferred_element_type=F32)

    def gate(s):
        for al in range(s * slab // nk, (s + 1) * slab // nk):
            rows = slice(al * nk, (al + 1) * nk)
            for c in range(tb // LANES):
                cs = slice(c * LANES, (c + 1) * LANES)
                cnt8 = cnt_ref[al * nh:(al + 1) * nh, cs]
                e18 = e1_ref[al * nh:(al + 1) * nh, cs]
                act = _gelu_tanh(ht_ref[rows, cs].astype(BF16)).reshape(nk // pk, pk, LANES)
                g = jnp.zeros((nk // pk, pk, LANES), BF16)
                for h in range(nh):
                    cb = jnp.broadcast_to(cnt8[h:h + 1, :], (pk, LANES)).astype(BF16)
                    eb_ = jnp.broadcast_to(e18[h:h + 1, :], (pk, LANES)).astype(BF16)
                    rk = rank_ref[h * nk:(h + 1) * nk, cs].reshape(nk // pk, pk, LANES)
                    e2 = e2_ref[h * nk:(h + 1) * nk, cs].reshape(nk // pk, pk, LANES)
                    g = g + jnp.where(rk < cb[None], e2, jnp.zeros_like(e2)) * eb_[None]
                w_ref[rows, cs] = (act * g).reshape(nk, LANES)

    def project(s):
        rows = slice(s * slab, (s + 1) * slab)
        acc_ref[...] += jnp.dot(vt_ref[:, rows], w_ref[rows, :], preferred_element_type=F32)

    hidden(0)
    for s in range(n_slab):
        if s + 1 < n_slab:
            hidden(s + 1)
        gate(s)
        if s >= 1:
            project(s - 1)
    project(n_slab - 1)

    @pl.when(j == pl.num_programs(1) - 1)
    def _():
        x = x_ref[...] + acc_ref[...].T
        if final_norm:
            x = x * lax.rsqrt(jnp.mean(x * x, axis=-1, keepdims=True) + RMS_EPS) * gn_ref[...]
        o_ref[...] = x


def _peer_dense(xnt, u, vt, cnt, e1, rank, e2, x2, gn, final_norm):
    d, rows = xnt.shape
    ne = u.shape[0]
    tb = min(512, rows)
    eb = 2048
    slab = 1024
    nrow = PEER_KEYS * PEER_HEADS
    arow = (eb // PEER_KEYS) * PEER_HEADS
    return pl.pallas_call(
        functools.partial(_peer_dense_kernel, tb=tb, eb=eb, slab=slab, final_norm=final_norm),
        grid=(rows // tb, ne // eb),
        in_specs=[
            pl.BlockSpec((d, tb), lambda i, j: (0, i)),
            pl.BlockSpec((eb, d), lambda i, j: (j, 0)),
            pl.BlockSpec((d, eb), lambda i, j: (0, j)),
            pl.BlockSpec((arow, tb), lambda i, j: (j, i)),
            pl.BlockSpec((arow, tb), lambda i, j: (j, i)),
            pl.BlockSpec((nrow, tb), lambda i, j: (0, i)),
            pl.BlockSpec((nrow, tb), lambda i, j: (0, i)),
            pl.BlockSpec((tb, d), lambda i, j: (i, 0)),
            pl.BlockSpec((1, d), lambda i, j: (0, 0)),
        ],
        out_specs=pl.BlockSpec((tb, d), lambda i, j: (i, 0)),
        out_shape=jax.ShapeDtypeStruct((rows, d), F32),
        scratch_shapes=[
            pltpu.VMEM((d, tb), F32),
            pltpu.VMEM((eb, tb), F32),
            pltpu.VMEM((eb, tb), BF16),
        ],
        compiler_params=_cparams(("parallel", "arbitrary")),
        name="peer_dense",
    )(xnt, u, vt, cnt, e1, rank, e2, x2, gn.reshape(1, d))


def _t5_bucket(dist):
    max_exact = N_BUCKETS // 2
    large = max_exact + (jnp.log(jnp.maximum(dist, 1).astype(F32) / max_exact)
                         / math.log(BUCKET_MAX_DIST / max_exact) * (N_BUCKETS - max_exact)).astype(jnp.int32)
    large = jnp.minimum(large, N_BUCKETS - 1)
    return jnp.where(dist < max_exact, dist, large)


def _rel_bias(table_cols, dilation):
    qi = jnp.arange(BLOCK)[:, None]
    kj = jnp.arange(2 * BLOCK)[None, :]
    dist = jnp.maximum(qi + BLOCK - kj, 0) * dilation
    onehot = (_t5_bucket(dist)[..., None] == jnp.arange(N_BUCKETS)).astype(F32)
    return jnp.einsum('qkb,bh->hqk', onehot, table_cols.astype(F32), precision=lax.Precision.HIGHEST)


def _split_w_in(w):
    hd = HEAD_DIM
    sizes = [A_Q_HEADS * hd, A_KV_HEADS * hd, A_KV_HEADS * hd] + [B_HEADS * hd] * 9 + [
        C_HEADS * C_DK, C_HEADS * C_DK, C_HEADS * C_DV, C_GATE_RANK, C_HEADS * C_DV, N_BRANCHES * D_MODEL]
    offs = np.concatenate([[0], np.cumsum(sizes)])
    col = lambda a, b: w[:, int(offs[a]):int(offs[b])].astype(BF16)
    pad = jnp.zeros((w.shape[0], G_WIDTH - G_ALR_COL - C_GATE_RANK), BF16)
    w_r = jnp.concatenate([col(16, 17), col(15, 16), pad], axis=1)
    return col(0, 3), [col(3 + 3 * g, 6 + 3 * g) for g in range(3)], col(12, 15), w_r, col(17, 18)


def _peer_key_matrix(keys):
    nh, _, nk, c = keys.shape
    eye = jnp.eye(nh, dtype=keys.dtype)
    big = jnp.einsum('hpnc,hg->pnhgc', keys, eye)
    return big.reshape(2, nk * nh, nh * c).astype(BF16)


def _residue_major(x2, batch, seq, dil):
    if dil == 1:
        return x2
    d = x2.shape[-1]
    return x2.reshape(batch, seq // dil, dil, d).transpose(0, 2, 1, 3).reshape(batch * seq, d)


def kernel(x, w_in, attn_sinks, gla_alpha_w, gla_alpha_b, gla_head_norm, w_branch, w_out, norm_mix,
           norm_ffn, peer_wq, peer_keys, peer_u, peer_v, rel_bias_table, norm_final):
    batch, seq, d = x.shape
    rows = batch * seq
    depth = w_in.shape[0]
    hw = B_HEADS * HEAD_DIM

    bias_a = _banded_bias(_rel_bias(rel_bias_table[:, :A_Q_HEADS], 1), A_WINDOW - 1)
    bias_b = [_banded_bias(
        _rel_bias(rel_bias_table[:, A_Q_HEADS + i * B_HEADS:A_Q_HEADS + (i + 1) * B_HEADS], dil), window // dil)
        for i, (window, dil) in enumerate(B_PATTERNS)]

    x2 = x.reshape(rows, d)
    for l in range(depth):
        w_a, w_bs, w_c, w_r, w_gates = _split_w_in(w_in[l])
        a_qkv = _norm_matmul(x2, norm_mix[l], w_a, A_WIDTH, BF16, "proj_a")
        c_qkv = _norm_matmul(x2, norm_mix[l], w_c, C_WIDTH, BF16, "proj_c")
        g_all = _norm_matmul(x2, norm_mix[l], w_r, G_WIDTH, F32, "proj_r")

        o_a = _banded_attention(
            a_qkv, bias_a, attn_sinks[l], batch=batch, seq=seq, dil=1, width=A_WIDTH,
            q_col=0, k_col=A_Q_HEADS * HEAD_DIM, v_col=(A_Q_HEADS + A_KV_HEADS) * HEAD_DIM,
            hq=A_Q_HEADS, grp=A_Q_HEADS // A_KV_HEADS, want_lse=False, out_dtype=BF16, name="attn_a")
        o_b, lse_b = [], []
        for gi, (window, dil) in enumerate(B_PATTERNS):
            b_qkv = _norm_matmul(_residue_major(x2, batch, seq, dil), norm_mix[l], w_bs[gi], 3 * hw, BF16,
                                 "proj_b%d" % gi)
            o, lse = _banded_attention(
                b_qkv, bias_b[gi], None, batch=batch, seq=seq, dil=dil, width=3 * hw,
                q_col=0, k_col=hw, v_col=2 * hw, hq=B_HEADS, grp=1,
                want_lse=True, out_dtype=F32, name="attn_b%d" % gi)
            o_b.append(o)
            lse_b.append(lse)

        aw = jnp.zeros((LANES, C_HEADS * C_DK), BF16).at[:C_GATE_RANK].set(gla_alpha_w[l].astype(BF16))
        o_c = _gla(c_qkv, g_all, aw, gla_alpha_b[l].reshape(1, -1), gla_head_norm[l].reshape(1, -1),
                   batch=batch, seq=seq)

        x2, xnt = _merge(o_a, o_b, lse_b, o_c, x2, norm_mix[l], w_gates, w_branch[l].astype(BF16),
                         w_out[l].astype(BF16), norm_ffn[l])

        wq_t = peer_wq[l].reshape(d, PEER_HEADS, 2, PEER_DKEY // 2).transpose(2, 1, 3, 0)
        wq_t = wq_t.reshape(2, PEER_HEADS * (PEER_DKEY // 2), d).astype(BF16)
        cnt, e1, rank, e2 = _peer_select(xnt, wq_t, _peer_key_matrix(peer_keys[l]))
        last = l == depth - 1
        x2 = _peer_dense(xnt, peer_u[l].astype(BF16), peer_v[l].T.astype(BF16), cnt, e1, rank, e2,
                         x2, norm_final, final_norm=last)
    return x2.reshape(batch, seq, d)
```

```python
import functools
import math

import numpy as np
import jax
import jax.numpy as jnp
from jax import lax
from jax.experimental import pallas as pl
from jax.experimental.pallas import tpu as pltpu

F32 = jnp.float32
BF16 = jnp.bfloat16

D_MODEL = 1024
HEAD_DIM = 64
BLOCK = 128
A_Q_HEADS = 8
A_KV_HEADS = 2
A_WINDOW = 128
B_PATTERNS = ((128, 1), (512, 4), (2048, 16))
B_HEADS = 8
C_HEADS = 4
C_DK = 64
C_DV = 128
C_GATE_RANK = 16
C_GATE_TAU = 16.0
C_CHUNK = 64
BRANCH_WIDTH = 512
N_BRANCHES = 3
N_BUCKETS = 32
BUCKET_MAX_DIST = 2048
PEER_HEADS = 8
PEER_KEYS = 128
PEER_N_EXPERTS = PEER_KEYS * PEER_KEYS
PEER_TOPK = 16
PEER_DKEY = 256
RMS_EPS = 1e-6
NEG_INF = -1e30

LANES = 128
VMEM_LIMIT = 56 * 1024 * 1024

GROUP_WIDTH = 3 * B_HEADS * HEAD_DIM
N_B0_COL = 0
N_CV_COL = GROUP_WIDTH
N_A_COL = N_CV_COL + C_HEADS * C_DV
N_CQ_COL = N_A_COL + (A_Q_HEADS + 2 * A_KV_HEADS) * HEAD_DIM
N_CK_COL = N_CQ_COL + C_HEADS * C_DK
N_WIDTH = N_CK_COL + C_HEADS * C_DK
G_GATES = N_BRANCHES * D_MODEL
G_WIDTH = C_HEADS * C_DV + LANES
G_R_COL = 0
G_ALR_COL = C_HEADS * C_DV
GLA_CHUNKS_PER_STEP = 4
ATTN_ROWS_PER_STEP = 2 * BLOCK


def _cparams(sem):
    return pltpu.CompilerParams(dimension_semantics=sem, vmem_limit_bytes=VMEM_LIMIT)


def _norm_matmul_kernel(x_ref, g_ref, *refs):
    n_out = len(refs) // 2
    x = x_ref[...]
    ms = jnp.mean(x * x, axis=-1, keepdims=True)
    h = (x * lax.rsqrt(ms + RMS_EPS) * g_ref[...]).astype(BF16)
    for w_ref, o_ref in zip(refs[:n_out], refs[n_out:]):
        o_ref[...] = jnp.dot(h, w_ref[...], preferred_element_type=F32).astype(o_ref.dtype)


def _norm_matmul(x2, g, ws, out_dtypes, name):
    rows, d = x2.shape
    tm = min(512, rows)
    res = pl.pallas_call(
        _norm_matmul_kernel,
        grid=(rows // tm,),
        in_specs=[pl.BlockSpec((tm, d), lambda i: (i, 0)), pl.BlockSpec((1, d), lambda i: (0, 0))]
        + [pl.BlockSpec(w.shape, lambda i: (0, 0)) for w in ws],
        out_specs=[pl.BlockSpec((tm, w.shape[1]), lambda i: (i, 0)) for w in ws],
        out_shape=[jax.ShapeDtypeStruct((rows, w.shape[1]), dt) for w, dt in zip(ws, out_dtypes)],
        compiler_params=_cparams(("parallel",)),
        name=name,
    )(x2, g.reshape(1, d), *ws)
    return res


def _attn_kernel(*refs, hq, grp, has_sink, want_lse, nsb):
    q_ref, kp_ref, kc_ref, vp_ref, vc_ref, bias_ref = refs[:6]
    pos = 6
    sink_ref = None
    if has_sink:
        sink_ref = refs[pos]
        pos += 1
    o_ref = refs[pos]
    lse_ref = refs[pos + 1] if want_lse else None

    first = jnp.minimum(pl.program_id(2), 1)
    k = jnp.concatenate([kp_ref[0], kc_ref[0]], axis=0)
    v = jnp.concatenate([vp_ref[0], vc_ref[0]], axis=0)
    for sb in range(nsb):
        q = q_ref[0, sb * BLOCK:(sb + 1) * BLOCK, :] * (HEAD_DIM ** -0.5)
        ks = k[sb * BLOCK:(sb + 2) * BLOCK]
        vs = v[sb * BLOCK:(sb + 2) * BLOCK]
        outs, lses = [], []
        for h in range(hq):
            kvh = h // grp
            qh = q[:, h * HEAD_DIM:(h + 1) * HEAD_DIM]
            kh = ks[:, kvh * HEAD_DIM:(kvh + 1) * HEAD_DIM]
            vh = vs[:, kvh * HEAD_DIM:(kvh + 1) * HEAD_DIM]
            s = lax.dot_general(qh, kh, (((1,), (1,)), ((), ())), preferred_element_type=F32)
            bias = bias_ref[first, h] if sb == 0 else bias_ref[1, h]
            s = s + bias
            m = jnp.max(s, axis=-1, keepdims=True)
            if has_sink:
                m = jnp.maximum(m, sink_ref[h])
            p = jnp.exp(s - m)
            denom = jnp.sum(p, axis=-1, keepdims=True)
            if has_sink:
                denom = denom + jnp.exp(sink_ref[h] - m)
            o = jnp.dot(p.astype(BF16), vh, preferred_element_type=F32) / denom
            outs.append(o)
            if want_lse:
                lses.append(jnp.broadcast_to(m + jnp.log(denom), (BLOCK, HEAD_DIM)))
        o_ref[0, sb * BLOCK:(sb + 1) * BLOCK, :] = jnp.concatenate(outs, axis=-1).astype(o_ref.dtype)
        if want_lse:
            lse_ref[0, sb * BLOCK:(sb + 1) * BLOCK, :] = jnp.concatenate(lses, axis=-1)


def _banded_bias(bias, max_dist):
    qi = np.arange(BLOCK)[:, None]
    kj = np.arange(2 * BLOCK)[None, :]
    dist = qi + BLOCK - kj
    in_window = (dist >= 0) & (dist <= max_dist)
    first = in_window & (kj >= BLOCK)
    return jnp.stack([jnp.where(first, bias, NEG_INF), jnp.where(in_window, bias, NEG_INF)])


def _banded_attention(qkv, bias, sink, *, batch, seq, dil, width, q_col, k_col, v_col,
                      hq, grp, want_lse, out_dtype, name):
    sub = seq // dil
    rows = min(ATTN_ROWS_PER_STEP, sub)
    nsb = rows // BLOCK
    qw = hq * HEAD_DIM
    kw = (hq // grp) * HEAD_DIM
    view = qkv.reshape(batch * dil, sub, width)
    q_blk, k_blk, v_blk = q_col // qw, k_col // kw, v_col // kw
    assert q_col % qw == 0 and k_col % kw == 0 and v_col % kw == 0

    prev_spec = lambda c: pl.BlockSpec((1, BLOCK, kw), lambda b, r, i: (b * dil + r, jnp.maximum(i * nsb - 1, 0), c))
    cur_spec = lambda w, c: pl.BlockSpec((1, rows, w), lambda b, r, i: (b * dil + r, i, c))
    in_specs = [
        cur_spec(qw, q_blk), prev_spec(k_blk), cur_spec(kw, k_blk), prev_spec(v_blk), cur_spec(kw, v_blk),
        pl.BlockSpec((2, hq, BLOCK, 2 * BLOCK), lambda b, r, i: (0, 0, 0, 0)),
    ]
    args = [view, view, view, view, view, bias]
    if sink is not None:
        in_specs.append(pl.BlockSpec(memory_space=pltpu.SMEM))
        args.append(sink)
    o_spec = pl.BlockSpec((1, rows, qw), lambda b, r, i: (b, i, r))
    o_shape = jax.ShapeDtypeStruct((batch, sub, dil * qw), out_dtype)
    out_specs, out_shape = o_spec, o_shape
    if want_lse:
        out_specs = [o_spec, pl.BlockSpec((1, rows, qw), lambda b, r, i: (b, i, r))]
        out_shape = [o_shape, jax.ShapeDtypeStruct((batch, sub, dil * qw), F32)]
    res = pl.pallas_call(
        functools.partial(_attn_kernel, hq=hq, grp=grp, has_sink=sink is not None, want_lse=want_lse, nsb=nsb),
        grid=(batch, dil, sub // rows),
        in_specs=in_specs,
        out_specs=out_specs,
        out_shape=out_shape,
        compiler_params=_cparams(("parallel", "parallel", "arbitrary")),
        name=name,
    )(*args)
    if want_lse:
        return res[0].reshape(batch * seq, qw), res[1].reshape(batch * seq, qw)
    return res.reshape(batch * seq, qw)


def _gla_kernel(q_ref, k_ref, v_ref, alr_ref, r_ref, aw_ref, ab_ref, hn_ref, o_ref, st_ref):
    @pl.when(pl.program_id(1) == 0)
    def _():
        st_ref[...] = jnp.zeros_like(st_ref)

    ck = C_CHUNK
    z = jnp.dot(alr_ref[0].astype(BF16), aw_ref[...], preferred_element_type=F32) + ab_ref[...]
    log_a = (jnp.minimum(z, 0.0) - jnp.log1p(jnp.exp(-jnp.abs(z)))) * (1.0 / C_GATE_TAU)
    row = lax.broadcasted_iota(jnp.int32, log_a.shape, 0) & (ck - 1)
    cum = log_a
    shift = 1
    while shift < ck:
        cum = cum + jnp.where(row >= shift, pltpu.roll(cum, shift, 0), 0.0)
        shift *= 2
    q_all = q_ref[0].astype(F32) * (C_DK ** -0.5)
    k_all = k_ref[0].astype(F32)
    q_dec_all = (q_all * jnp.exp(cum)).astype(BF16)
    k_inv_all = (k_all * jnp.exp(-cum)).astype(BF16)
    ti = lax.broadcasted_iota(jnp.int32, (ck, ck), 0)
    si = lax.broadcasted_iota(jnp.int32, (ck, ck), 1)
    causal = ti >= si
    nt = (((1,), (1,)), ((), ()))
    tn = (((0,), (0,)), ((), ()))
    for c in range(GLA_CHUNKS_PER_STEP):
        rs = slice(c * ck, (c + 1) * ck)
        cum_c = cum[rs]
        last = cum_c[ck - 1:ck, :]
        q_dec, k_inv = q_dec_all[rs], k_inv_all[rs]
        k_end = (k_all[rs] * jnp.exp(last - cum_c)).astype(BF16)
        decay = jnp.exp(last)
        v = v_ref[0, rs, :]
        r = r_ref[0, rs, :]
        outs = []
        for h in range(C_HEADS):
            ks = slice(h * C_DK, (h + 1) * C_DK)
            vs = slice(h * C_DV, (h + 1) * C_DV)
            qd, ki, ke, vh = q_dec[:, ks], k_inv[:, ks], k_end[:, ks], v[:, vs]
            att = jnp.where(causal, lax.dot_general(qd, ki, nt, preferred_element_type=F32), 0.0)
            st = st_ref[h]
            o = jnp.dot(att.astype(BF16), vh, preferred_element_type=F32)
            o = o + lax.dot_general(qd, st.astype(BF16), nt, preferred_element_type=F32)
            kv_t = lax.dot_general(vh, ke, tn, preferred_element_type=F32)
            st_ref[h] = st * decay[:, ks] + kv_t
            o = o * lax.rsqrt(jnp.mean(o * o, axis=-1, keepdims=True) + RMS_EPS) * hn_ref[...]
            rh = r[:, vs]
            outs.append(o * (rh * jax.nn.sigmoid(rh)))
        o_ref[0, rs, :] = jnp.concatenate(outs, axis=-1).astype(o_ref.dtype)


def _gla(c_qkv, g_all, aw, ab, hn, *, batch, seq):
    rows = C_CHUNK * GLA_CHUNKS_PER_STEP
    qk_w = C_HEADS * C_DK
    v_w = C_HEADS * C_DV
    cv = c_qkv.reshape(batch, seq, N_WIDTH)
    gv = g_all.reshape(batch, seq, G_WIDTH)
    return pl.pallas_call(
        _gla_kernel,
        grid=(batch, seq // rows),
        in_specs=[
            pl.BlockSpec((1, rows, qk_w), lambda b, c: (b, c, N_CQ_COL // qk_w)),
            pl.BlockSpec((1, rows, qk_w), lambda b, c: (b, c, N_CK_COL // qk_w)),
            pl.BlockSpec((1, rows, v_w), lambda b, c: (b, c, N_CV_COL // v_w)),
            pl.BlockSpec((1, rows, LANES), lambda b, c: (b, c, G_ALR_COL // LANES)),
            pl.BlockSpec((1, rows, v_w), lambda b, c: (b, c, G_R_COL // v_w)),
            pl.BlockSpec((LANES, qk_w), lambda b, c: (0, 0)),
            pl.BlockSpec((1, qk_w), lambda b, c: (0, 0)),
            pl.BlockSpec((1, C_DV), lambda b, c: (0, 0)),
        ],
        out_specs=pl.BlockSpec((1, rows, v_w), lambda b, c: (b, c, 0)),
        out_shape=jax.ShapeDtypeStruct((batch, seq, v_w), BF16),
        scratch_shapes=[pltpu.VMEM((C_HEADS, C_DV, C_DK), F32)],
        compiler_params=_cparams(("parallel", "arbitrary")),
        name="gla",
    )(cv, cv, cv, gv, gv, aw, ab, hn).reshape(batch * seq, v_w)


def _merge_kernel(oa_ref, ob1_ref, ob2_ref, ob3_ref, l1_ref, l2_ref, l3_ref, oc_ref, x_ref,
                  gm_ref, wg_ref, wb_ref, wo_ref, gn_ref, xo_ref, xnt_ref):
    l1, l2, l3 = l1_ref[...], l2_ref[...], l3_ref[...]
    lm = jnp.maximum(jnp.maximum(l1, l2), l3)
    e1, e2, e3 = jnp.exp(l1 - lm), jnp.exp(l2 - lm), jnp.exp(l3 - lm)
    ob = (e1 * ob1_ref[...] + e2 * ob2_ref[...] + e3 * ob3_ref[...]) / (e1 + e2 + e3)
    branches = (oa_ref[...], ob.astype(BF16), oc_ref[...])
    x = x_ref[...]
    h = (x * lax.rsqrt(jnp.mean(x * x, axis=-1, keepdims=True) + RMS_EPS) * gm_ref[...]).astype(BF16)
    merged = None
    for n in range(N_BRANCHES):
        gate = jnp.dot(h, wg_ref[:, n * D_MODEL:(n + 1) * D_MODEL], preferred_element_type=F32)
        proj = jnp.dot(branches[n], wb_ref[n], preferred_element_type=F32)
        term = jax.nn.sigmoid(gate) * proj
        merged = term if merged is None else merged + term
    x = x + jnp.dot(merged.astype(BF16), wo_ref[...], preferred_element_type=F32)
    xo_ref[...] = x
    xn = x * lax.rsqrt(jnp.mean(x * x, axis=-1, keepdims=True) + RMS_EPS) * gn_ref[...]
    xnt_ref[...] = xn.T.astype(BF16)


def _merge(o_a, o_b, lse_b, o_c, x2, gm, wg, wb, wo, gn):
    rows = x2.shape[0]
    tm = min(512, rows)
    row_spec = lambda w: pl.BlockSpec((tm, w), lambda i: (i, 0))
    vec_spec = pl.BlockSpec((1, D_MODEL), lambda i: (0, 0))
    return pl.pallas_call(
        _merge_kernel,
        grid=(rows // tm,),
        in_specs=[row_spec(BRANCH_WIDTH)] * 8 + [
            row_spec(D_MODEL),
            vec_spec,
            pl.BlockSpec((D_MODEL, G_GATES), lambda i: (0, 0)),
            pl.BlockSpec((N_BRANCHES, BRANCH_WIDTH, D_MODEL), lambda i: (0, 0, 0)),
            pl.BlockSpec((D_MODEL, D_MODEL), lambda i: (0, 0)),
            vec_spec,
        ],
        out_specs=[row_spec(D_MODEL), pl.BlockSpec((D_MODEL, tm), lambda i: (0, i))],
        out_shape=[jax.ShapeDtypeStruct((rows, D_MODEL), F32),
                   jax.ShapeDtypeStruct((D_MODEL, rows), BF16)],
        compiler_params=_cparams(("parallel",)),
        name="merge",
    )(o_a, o_b[0], o_b[1], o_b[2], lse_b[0], lse_b[1], lse_b[2], o_c, x2, gm.reshape(1, D_MODEL),
      wg, wb, wo, gn.reshape(1, D_MODEL))


_PAIRS = tuple((i, j) for i in range(PEER_TOPK) for j in range(PEER_TOPK)
               if (i + 1) * (j + 1) <= PEER_TOPK)


def _sort16_desc(v):
    v = list(v)
    n = len(v)
    k = 2
    while k <= n:
        j = k // 2
        while j >= 1:
            for i in range(n):
                l = i ^ j
                if l > i:
                    hi, lo = jnp.maximum(v[i], v[l]), jnp.minimum(v[i], v[l])
                    v[i], v[l] = (hi, lo) if (i & k) == 0 else (lo, hi)
            j //= 2
        k *= 2
    return v


def _merge_top16(a, b):
    n = len(a)
    v = [jnp.maximum(a[i], b[n - 1 - i]) for i in range(n)]
    j = n // 2
    while j >= 1:
        for i in range(n):
            l = i ^ j
            if l > i:
                v[i], v[l] = jnp.maximum(v[i], v[l]), jnp.minimum(v[i], v[l])
        j //= 2
    return v


def _top16_of(vals):
    acc = None
    for g in range(len(vals) // PEER_TOPK):
        grp = _sort16_desc(vals[g * PEER_TOPK:(g + 1) * PEER_TOPK])
        acc = grp if acc is None else _merge_top16(acc, grp)
    return acc


def _peer_select_kernel(xnt_ref, wq_ref, kb_ref, cnt_ref, e1_ref, rank_ref, e2_ref,
                        sc_ref, rk_ref, eb_ref, *, tb):
    nk, nh, kk = PEER_KEYS, PEER_HEADS, PEER_TOPK
    xnt = xnt_ref[...]
    for p in range(2):
        qy = jnp.dot(wq_ref[p], xnt, preferred_element_type=F32).astype(BF16)
        sc_ref[p] = jnp.dot(kb_ref[p], qy, preferred_element_type=F32).reshape(nk, nh, tb)

    for c in range(tb // LANES):
        cs = slice(c * LANES, (c + 1) * LANES)
        s1s = _top16_of([sc_ref[0, n, :, cs] for n in range(nk)])
        s2s = _top16_of([sc_ref[1, n, :, cs] for n in range(nk)])
        cands = [s1s[i] + s2s[j] for (i, j) in _PAIRS]
        pad = [jnp.full_like(cands[0], -jnp.inf)] * (-len(cands) % kk)
        tau = _top16_of(cands + pad)[kk - 1]
        top = cands[0]
        zsum = None
        counts = [None] * kk
        for (i, j), cd in zip(_PAIRS, cands):
            sel = cd >= tau
            term = jnp.where(sel, jnp.exp(cd - top), 0.0)
            zsum = term if zsum is None else zsum + term
            one = jnp.where(sel, 1.0, 0.0)
            counts[i] = one if counts[i] is None else counts[i] + one
        inv_z = 1.0 / zsum

        def first_half(n, carry):
            s1 = sc_ref[0, n, :, cs]
            cnt = jnp.zeros_like(s1)
            for i in range(kk):
                cnt = jnp.where(s1 == s1s[i], counts[i], cnt)
            r0 = pl.multiple_of(n * nh, nh)
            cnt_ref[pl.ds(r0, nh), cs] = cnt
            e1_ref[pl.ds(r0, nh), cs] = jnp.exp(s1 - s1s[0])
            return carry

        def second_half(n, carry):
            s2 = sc_ref[1, n, :, cs]
            rank = jnp.full_like(s2, float(kk))
            for j in range(kk):
                rank = jnp.where(s2 == s2s[j], float(j), rank)
            r0 = pl.multiple_of(n * nh, nh)
            rk_ref[pl.ds(r0, nh), :] = rank
            eb_ref[pl.ds(r0, nh), :] = jnp.exp(s2 - s2s[0]) * inv_z
            return carry

        lax.fori_loop(0, nk, first_half, 0, unroll=4)
        lax.fori_loop(0, nk, second_half, 0, unroll=4)
        for h in range(nh):
            rank_ref[h * nk:(h + 1) * nk, cs] = rk_ref[pl.ds(h, nk, stride=nh), :].astype(BF16)
            e2_ref[h * nk:(h + 1) * nk, cs] = eb_ref[pl.ds(h, nk, stride=nh), :].astype(BF16)


def _peer_select(xnt, wq_t, k_big):
    d, rows = xnt.shape
    tb = min(256, rows)
    nrow = PEER_KEYS * PEER_HEADS
    tab_spec = pl.BlockSpec((nrow, tb), lambda i: (0, i))
    return pl.pallas_call(
        functools.partial(_peer_select_kernel, tb=tb),
        grid=(rows // tb,),
        in_specs=[
            pl.BlockSpec((d, tb), lambda i: (0, i)),
            pl.BlockSpec((2, nrow, d), lambda i: (0, 0, 0)),
            pl.BlockSpec((2, nrow, nrow), lambda i: (0, 0, 0)),
        ],
        out_specs=[tab_spec] * 4,
        out_shape=[jax.ShapeDtypeStruct((nrow, rows), dt) for dt in (F32, F32, BF16, BF16)],
        scratch_shapes=[
            pltpu.VMEM((2, PEER_KEYS, PEER_HEADS, tb), F32),
            pltpu.VMEM((nrow, LANES), F32),
            pltpu.VMEM((nrow, LANES), F32),
        ],
        compiler_params=_cparams(("parallel",)),
        name="peer_select",
    )(xnt, wq_t, k_big)


def _gelu_tanh(x):
    c = math.sqrt(2.0 / math.pi)
    return 0.5 * x * (1.0 + jnp.tanh(c * (x + 0.044715 * (x * x * x))))


def _peer_dense_kernel(xnt_ref, u_ref, vt_ref, cnt_ref, e1_ref, rank_ref, e2_ref, x_ref, gn_ref,
                       o_ref, acc_ref, ht_ref, w_ref, *, tb, eb, slab, final_norm):
    j = pl.program_id(1)
    nk, nh = PEER_KEYS, PEER_HEADS
    pk = 16
    n_slab = eb // slab

    @pl.when(j == 0)
    def _():
        acc_ref[...] = jnp.zeros_like(acc_ref)

    def hidden(s):
        rows = slice(s * slab, (s + 1) * slab)
        ht_ref[rows, :] = jnp.dot(u_ref[rows, :], xnt_ref[...], preferred_element_type=F32)

    def gate(s):
        for al in range(s * slab // nk, (s + 1) * slab // nk):
            rows = slice(al * nk, (al + 1) * nk)
            for c in range(tb // LANES):
                cs = slice(c * LANES, (c + 1) * LANES)
                cnt8 = cnt_ref[al * nh:(al + 1) * nh, cs]
                e18 = e1_ref[al * nh:(al + 1) * nh, cs]
                act = _gelu_tanh(ht_ref[rows, cs].astype(BF16)).reshape(nk // pk, pk, LANES)
                g = jnp.zeros((nk // pk, pk, LANES), BF16)
                for h in range(nh):
                    cb = jnp.broadcast_to(cnt8[h:h + 1, :], (pk, LANES)).astype(BF16)
                    eb_ = jnp.broadcast_to(e18[h:h + 1, :], (pk, LANES)).astype(BF16)
                    rk = rank_ref[h * nk:(h + 1) * nk, cs].reshape(nk // pk, pk, LANES)
                    e2 = e2_ref[h * nk:(h + 1) * nk, cs].reshape(nk // pk, pk, LANES)
                    g = g + jnp.where(rk < cb[None], e2, jnp.zeros_like(e2)) * eb_[None]
                w_ref[rows, cs] = (act * g).reshape(nk, LANES)

    def project(s):
        rows = slice(s * slab, (s + 1) * slab)
        acc_ref[...] += jnp.dot(vt_ref[:, rows], w_ref[rows, :], preferred_element_type=F32)

    hidden(0)
    for s in range(n_slab):
        if s + 1 < n_slab:
            hidden(s + 1)
        gate(s)
        if s >= 1:
            project(s - 1)
    project(n_slab - 1)

    @pl.when(j == pl.num_programs(1) - 1)
    def _():
        x = x_ref[...] + acc_ref[...].T
        if final_norm:
            x = x * lax.rsqrt(jnp.mean(x * x, axis=-1, keepdims=True) + RMS_EPS) * gn_ref[...]
        o_ref[...] = x


def _peer_dense(xnt, u, vt, cnt, e1, rank, e2, x2, gn, final_norm):
    d, rows = xnt.shape
    ne = u.shape[0]
    tb = min(512, rows)
    eb = 2048
    slab = 1024
    nrow = PEER_KEYS * PEER_HEADS
    arow = (eb // PEER_KEYS) * PEER_HEADS
    return pl.pallas_call(
        functools.partial(_peer_dense_kernel, tb=tb, eb=eb, slab=slab, final_norm=final_norm),
        grid=(rows // tb, ne // eb),
        in_specs=[
            pl.BlockSpec((d, tb), lambda i, j: (0, i)),
            pl.BlockSpec((eb, d), lambda i, j: (j, 0)),
            pl.BlockSpec((d, eb), lambda i, j: (0, j)),
            pl.BlockSpec((arow, tb), lambda i, j: (j, i)),
            pl.BlockSpec((arow, tb), lambda i, j: (j, i)),
            pl.BlockSpec((nrow, tb), lambda i, j: (0, i)),
            pl.BlockSpec((nrow, tb), lambda i, j: (0, i)),
            pl.BlockSpec((tb, d), lambda i, j: (i, 0)),
            pl.BlockSpec((1, d), lambda i, j: (0, 0)),
        ],
        out_specs=pl.BlockSpec((tb, d), lambda i, j: (i, 0)),
        out_shape=jax.ShapeDtypeStruct((rows, d), F32),
        scratch_shapes=[
            pltpu.VMEM((d, tb), F32),
            pltpu.VMEM((eb, tb), F32),
            pltpu.VMEM((eb, tb), BF16),
        ],
        compiler_params=_cparams(("parallel", "arbitrary")),
        name="peer_dense",
    )(xnt, u, vt, cnt, e1, rank, e2, x2, gn.reshape(1, d))


def _t5_bucket(dist):
    max_exact = N_BUCKETS // 2
    large = max_exact + (jnp.log(jnp.maximum(dist, 1).astype(F32) / max_exact)
                         / math.log(BUCKET_MAX_DIST / max_exact) * (N_BUCKETS - max_exact)).astype(jnp.int32)
    large = jnp.minimum(large, N_BUCKETS - 1)
    return jnp.where(dist < max_exact, dist, large)


def _rel_bias(table_cols, dilation):
    qi = jnp.arange(BLOCK)[:, None]
    kj = jnp.arange(2 * BLOCK)[None, :]
    dist = jnp.maximum(qi + BLOCK - kj, 0) * dilation
    onehot = (_t5_bucket(dist)[..., None] == jnp.arange(N_BUCKETS)).astype(F32)
    return jnp.einsum('qkb,bh->hqk', onehot, table_cols.astype(F32), precision=lax.Precision.HIGHEST)


def _split_w_in(w):
    hd = HEAD_DIM
    sizes = [A_Q_HEADS * hd, A_KV_HEADS * hd, A_KV_HEADS * hd] + [B_HEADS * hd] * 9 + [
        C_HEADS * C_DK, C_HEADS * C_DK, C_HEADS * C_DV, C_GATE_RANK, C_HEADS * C_DV, N_BRANCHES * D_MODEL]
    offs = np.concatenate([[0], np.cumsum(sizes)])
    col = lambda a, b: w[:, int(offs[a]):int(offs[b])].astype(BF16)
    groups = [col(3 + 3 * g, 6 + 3 * g) for g in range(3)]
    w_nat = jnp.concatenate([groups[0], col(14, 15), col(0, 3), col(12, 14)], axis=1)
    pad = jnp.zeros((w.shape[0], G_WIDTH - G_ALR_COL - C_GATE_RANK), BF16)
    w_r = jnp.concatenate([col(16, 17), col(15, 16), pad], axis=1)
    return w_nat, w_r, groups, col(17, 18)


def _peer_key_matrix(keys):
    nh, _, nk, c = keys.shape
    eye = jnp.eye(nh, dtype=keys.dtype)
    big = jnp.einsum('hpnc,hg->pnhgc', keys, eye)
    return big.reshape(2, nk * nh, nh * c).astype(BF16)


def _residue_major(x2, batch, seq, dil):
    if dil == 1:
        return x2
    d = x2.shape[-1]
    return x2.reshape(batch, seq // dil, dil, d).transpose(0, 2, 1, 3).reshape(batch * seq, d)


def kernel(x, w_in, attn_sinks, gla_alpha_w, gla_alpha_b, gla_head_norm, w_branch, w_out, norm_mix,
           norm_ffn, peer_wq, peer_keys, peer_u, peer_v, rel_bias_table, norm_final):
    batch, seq, d = x.shape
    rows = batch * seq
    depth = w_in.shape[0]
    hw = B_HEADS * HEAD_DIM

    bias_a = _banded_bias(_rel_bias(rel_bias_table[:, :A_Q_HEADS], 1), A_WINDOW - 1)
    bias_b = [_banded_bias(
        _rel_bias(rel_bias_table[:, A_Q_HEADS + i * B_HEADS:A_Q_HEADS + (i + 1) * B_HEADS], dil), window // dil)
        for i, (window, dil) in enumerate(B_PATTERNS)]

    x2 = x.reshape(rows, d)
    for l in range(depth):
        w_nat, w_r, w_groups, w_gates = _split_w_in(w_in[l])
        nat, g_all = _norm_matmul(x2, norm_mix[l], [w_nat, w_r], [BF16, F32], "proj_nat")

        o_a = _banded_attention(
            nat, bias_a, attn_sinks[l], batch=batch, seq=seq, dil=1, width=N_WIDTH,
            q_col=N_A_COL, k_col=N_A_COL + A_Q_HEADS * HEAD_DIM,
            v_col=N_A_COL + (A_Q_HEADS + A_KV_HEADS) * HEAD_DIM,
            hq=A_Q_HEADS, grp=A_Q_HEADS // A_KV_HEADS, want_lse=False, out_dtype=BF16, name="attn_a")
        o_b, lse_b = [], []
        for gi, (window, dil) in enumerate(B_PATTERNS):
            if dil == 1:
                b_qkv, width, col0 = nat, N_WIDTH, N_B0_COL
            else:
                b_qkv, = _norm_matmul(_residue_major(x2, batch, seq, dil), norm_mix[l], [w_groups[gi]], [BF16],
                                      "proj_b%d" % gi)
                width, col0 = GROUP_WIDTH, 0
            o, lse = _banded_attention(
                b_qkv, bias_b[gi], None, batch=batch, seq=seq, dil=dil, width=width,
                q_col=col0, k_col=col0 + hw, v_col=col0 + 2 * hw, hq=B_HEADS, grp=1,
                want_lse=True, out_dtype=F32, name="attn_b%d" % gi)
            o_b.append(o)
            lse_b.append(lse)

        aw = jnp.zeros((LANES, C_HEADS * C_DK), BF16).at[:C_GATE_RANK].set(gla_alpha_w[l].astype(BF16))
        o_c = _gla(nat, g_all, aw, gla_alpha_b[l].reshape(1, -1), gla_head_norm[l].reshape(1, -1),
                   batch=batch, seq=seq)

        x2, xnt = _merge(o_a, o_b, lse_b, o_c, x2, norm_mix[l], w_gates, w_branch[l].astype(BF16),
                         w_out[l].astype(BF16), norm_ffn[l])

        wq_t = peer_wq[l].reshape(d, PEER_HEADS, 2, PEER_DKEY // 2).transpose(2, 1, 3, 0)
        wq_t = wq_t.reshape(2, PEER_HEADS * (PEER_DKEY // 2), d).astype(BF16)
        cnt, e1, rank, e2 = _peer_select(xnt, wq_t, _peer_key_matrix(peer_keys[l]))
        last = l == depth - 1
        x2 = _peer_dense(xnt, peer_u[l].astype(BF16), peer_v[l].T.astype(BF16), cnt, e1, rank, e2,
                         x2, norm_final, final_norm=last)
    return x2.reshape(batch, seq, d)
```

```python
import functools
import math

import numpy as np
import jax
import jax.numpy as jnp
from jax import lax
from jax.experimental import pallas as pl
from jax.experimental.pallas import tpu as pltpu

F32 = jnp.float32
BF16 = jnp.bfloat16

D_MODEL = 1024
HEAD_DIM = 64
BLOCK = 128
A_Q_HEADS = 8
A_KV_HEADS = 2
A_WINDOW = 128
B_PATTERNS = ((128, 1), (512, 4), (2048, 16))
B_HEADS = 8
C_HEADS = 4
C_DK = 64
C_DV = 128
C_GATE_RANK = 16
C_GATE_TAU = 16.0
C_CHUNK = 64
BRANCH_WIDTH = 512
N_BRANCHES = 3
N_BUCKETS = 32
BUCKET_MAX_DIST = 2048
PEER_HEADS = 8
PEER_KEYS = 128
PEER_N_EXPERTS = PEER_KEYS * PEER_KEYS
PEER_TOPK = 16
PEER_DKEY = 256
RMS_EPS = 1e-6
NEG_INF = -1e30

LANES = 128
VMEM_LIMIT = 56 * 1024 * 1024

GROUP_WIDTH = 3 * B_HEADS * HEAD_DIM
N_B0_COL = 0
N_CV_COL = GROUP_WIDTH
N_A_COL = N_CV_COL + C_HEADS * C_DV
N_CQ_COL = N_A_COL + (A_Q_HEADS + 2 * A_KV_HEADS) * HEAD_DIM
N_CK_COL = N_CQ_COL + C_HEADS * C_DK
N_WIDTH = N_CK_COL + C_HEADS * C_DK
G_GATES = N_BRANCHES * D_MODEL
G_WIDTH = C_HEADS * C_DV + LANES
G_R_COL = 0
G_ALR_COL = C_HEADS * C_DV
GLA_CHUNKS_PER_STEP = 4
ATTN_ROWS_PER_STEP = 2 * BLOCK
LSE_LANES = LANES // B_HEADS


def _cparams(sem):
    return pltpu.CompilerParams(dimension_semantics=sem, vmem_limit_bytes=VMEM_LIMIT)


def _norm_matmul_kernel(x_ref, g_ref, *refs):
    n_out = len(refs) // 2
    x = x_ref[...]
    ms = jnp.mean(x * x, axis=-1, keepdims=True)
    h = (x * lax.rsqrt(ms + RMS_EPS) * g_ref[...]).astype(BF16)
    for w_ref, o_ref in zip(refs[:n_out], refs[n_out:]):
        o_ref[...] = jnp.dot(h, w_ref[...], preferred_element_type=F32).astype(o_ref.dtype)


def _norm_matmul(x2, g, ws, out_dtypes, name):
    rows, d = x2.shape
    tm = min(512, rows)
    res = pl.pallas_call(
        _norm_matmul_kernel,
        grid=(rows // tm,),
        in_specs=[pl.BlockSpec((tm, d), lambda i: (i, 0)), pl.BlockSpec((1, d), lambda i: (0, 0))]
        + [pl.BlockSpec(w.shape, lambda i: (0, 0)) for w in ws],
        out_specs=[pl.BlockSpec((tm, w.shape[1]), lambda i: (i, 0)) for w in ws],
        out_shape=[jax.ShapeDtypeStruct((rows, w.shape[1]), dt) for w, dt in zip(ws, out_dtypes)],
        compiler_params=_cparams(("parallel",)),
        name=name,
    )(x2, g.reshape(1, d), *ws)
    return res


def _attn_kernel(*refs, hq, grp, has_sink, want_lse, nsb):
    q_ref, kp_ref, kc_ref, vp_ref, vc_ref, bias_ref = refs[:6]
    pos = 6
    sink_ref = None
    if has_sink:
        sink_ref = refs[pos]
        pos += 1
    o_ref = refs[pos]
    lse_ref = refs[pos + 1] if want_lse else None

    first = jnp.minimum(pl.program_id(2), 1)
    k = jnp.concatenate([kp_ref[0], kc_ref[0]], axis=0)
    v = jnp.concatenate([vp_ref[0], vc_ref[0]], axis=0)
    for sb in range(nsb):
        q = q_ref[0, sb * BLOCK:(sb + 1) * BLOCK, :] * (HEAD_DIM ** -0.5)
        ks = k[sb * BLOCK:(sb + 2) * BLOCK]
        vs = v[sb * BLOCK:(sb + 2) * BLOCK]
        outs, lses = [], []
        for h in range(hq):
            kvh = h // grp
            qh = q[:, h * HEAD_DIM:(h + 1) * HEAD_DIM]
            kh = ks[:, kvh * HEAD_DIM:(kvh + 1) * HEAD_DIM]
            vh = vs[:, kvh * HEAD_DIM:(kvh + 1) * HEAD_DIM]
            s = lax.dot_general(qh, kh, (((1,), (1,)), ((), ())), preferred_element_type=F32)
            bias = bias_ref[first, h] if sb == 0 else bias_ref[1, h]
            s = s + bias
            m = jnp.max(s, axis=-1, keepdims=True)
            if has_sink:
                m = jnp.maximum(m, sink_ref[h])
            p = jnp.exp(s - m)
            denom = jnp.sum(p, axis=-1, keepdims=True)
            if has_sink:
                denom = denom + jnp.exp(sink_ref[h] - m)
            o = jnp.dot(p.astype(BF16), vh, preferred_element_type=F32) / denom
            outs.append(o)
            if want_lse:
                lses.append(jnp.broadcast_to(m + jnp.log(denom), (BLOCK, LSE_LANES)))
        o_ref[0, sb * BLOCK:(sb + 1) * BLOCK, :] = jnp.concatenate(outs, axis=-1).astype(o_ref.dtype)
        if want_lse:
            lse_ref[0, sb * BLOCK:(sb + 1) * BLOCK, :] = jnp.concatenate(lses, axis=-1)


def _banded_bias(bias, max_dist):
    qi = np.arange(BLOCK)[:, None]
    kj = np.arange(2 * BLOCK)[None, :]
    dist = qi + BLOCK - kj
    in_window = (dist >= 0) & (dist <= max_dist)
    first = in_window & (kj >= BLOCK)
    return jnp.stack([jnp.where(first, bias, NEG_INF), jnp.where(in_window, bias, NEG_INF)])


def _banded_attention(qkv, bias, sink, *, batch, seq, dil, width, q_col, k_col, v_col,
                      hq, grp, want_lse, out_dtype, name):
    sub = seq // dil
    rows = min(ATTN_ROWS_PER_STEP, sub)
    nsb = rows // BLOCK
    qw = hq * HEAD_DIM
    kw = (hq // grp) * HEAD_DIM
    view = qkv.reshape(batch * dil, sub, width)
    q_blk, k_blk, v_blk = q_col // qw, k_col // kw, v_col // kw
    assert q_col % qw == 0 and k_col % kw == 0 and v_col % kw == 0

    prev_spec = lambda c: pl.BlockSpec((1, BLOCK, kw), lambda b, r, i: (b * dil + r, jnp.maximum(i * nsb - 1, 0), c))
    cur_spec = lambda w, c: pl.BlockSpec((1, rows, w), lambda b, r, i: (b * dil + r, i, c))
    in_specs = [
        cur_spec(qw, q_blk), prev_spec(k_blk), cur_spec(kw, k_blk), prev_spec(v_blk), cur_spec(kw, v_blk),
        pl.BlockSpec((2, hq, BLOCK, 2 * BLOCK), lambda b, r, i: (0, 0, 0, 0)),
    ]
    args = [view, view, view, view, view, bias]
    if sink is not None:
        in_specs.append(pl.BlockSpec(memory_space=pltpu.SMEM))
        args.append(sink)
    o_spec = pl.BlockSpec((1, rows, qw), lambda b, r, i: (b, i, r))
    o_shape = jax.ShapeDtypeStruct((batch, sub, dil * qw), out_dtype)
    out_specs, out_shape = o_spec, o_shape
    if want_lse:
        out_specs = [o_spec, pl.BlockSpec((1, rows, LANES), lambda b, r, i: (b, i, r))]
        out_shape = [o_shape, jax.ShapeDtypeStruct((batch, sub, dil * LANES), F32)]
    res = pl.pallas_call(
        functools.partial(_attn_kernel, hq=hq, grp=grp, has_sink=sink is not None, want_lse=want_lse, nsb=nsb),
        grid=(batch, dil, sub // rows),
        in_specs=in_specs,
        out_specs=out_specs,
        out_shape=out_shape,
        compiler_params=_cparams(("parallel", "parallel", "arbitrary")),
        name=name,
    )(*args)
    if want_lse:
        return res[0].reshape(batch * seq, qw), res[1].reshape(batch * seq, LANES)
    return res.reshape(batch * seq, qw)


def _gla_kernel(q_ref, k_ref, v_ref, alr_ref, r_ref, aw_ref, ab_ref, hn_ref, o_ref, st_ref):
    @pl.when(pl.program_id(1) == 0)
    def _():
        st_ref[...] = jnp.zeros_like(st_ref)

    ck = C_CHUNK
    z = jnp.dot(alr_ref[0].astype(BF16), aw_ref[...], preferred_element_type=F32) + ab_ref[...]
    log_a = (jnp.minimum(z, 0.0) - jnp.log1p(jnp.exp(-jnp.abs(z)))) * (1.0 / C_GATE_TAU)
    row = lax.broadcasted_iota(jnp.int32, log_a.shape, 0) & (ck - 1)
    cum = log_a
    shift = 1
    while shift < ck:
        cum = cum + jnp.where(row >= shift, pltpu.roll(cum, shift, 0), 0.0)
        shift *= 2
    q_all = q_ref[0].astype(F32) * (C_DK ** -0.5)
    k_all = k_ref[0].astype(F32)
    q_dec_all = (q_all * jnp.exp(cum)).astype(BF16)
    k_inv_all = (k_all * jnp.exp(-cum)).astype(BF16)
    ti = lax.broadcasted_iota(jnp.int32, (ck, ck), 0)
    si = lax.broadcasted_iota(jnp.int32, (ck, ck), 1)
    causal = ti >= si
    nt = (((1,), (1,)), ((), ()))
    tn = (((0,), (0,)), ((), ()))
    for c in range(GLA_CHUNKS_PER_STEP):
        rs = slice(c * ck, (c + 1) * ck)
        cum_c = cum[rs]
        last = cum_c[ck - 1:ck, :]
        q_dec, k_inv = q_dec_all[rs], k_inv_all[rs]
        k_end = (k_all[rs] * jnp.exp(last - cum_c)).astype(BF16)
        decay = jnp.exp(last)
        v = v_ref[0, rs, :]
        r = r_ref[0, rs, :]
        outs = []
        for h in range(C_HEADS):
            ks = slice(h * C_DK, (h + 1) * C_DK)
            vs = slice(h * C_DV, (h + 1) * C_DV)
            qd, ki, ke, vh = q_dec[:, ks], k_inv[:, ks], k_end[:, ks], v[:, vs]
            att = jnp.where(causal, lax.dot_general(qd, ki, nt, preferred_element_type=F32), 0.0)
            st = st_ref[h]
            o = jnp.dot(att.astype(BF16), vh, preferred_element_type=F32)
            o = o + lax.dot_general(qd, st.astype(BF16), nt, preferred_element_type=F32)
            kv_t = lax.dot_general(vh, ke, tn, preferred_element_type=F32)
            st_ref[h] = st * decay[:, ks] + kv_t
            o = o * lax.rsqrt(jnp.mean(o * o, axis=-1, keepdims=True) + RMS_EPS) * hn_ref[...]
            rh = r[:, vs]
            outs.append(o * (rh * jax.nn.sigmoid(rh)))
        o_ref[0, rs, :] = jnp.concatenate(outs, axis=-1).astype(o_ref.dtype)


def _gla(c_qkv, g_all, aw, ab, hn, *, batch, seq):
    rows = C_CHUNK * GLA_CHUNKS_PER_STEP
    qk_w = C_HEADS * C_DK
    v_w = C_HEADS * C_DV
    cv = c_qkv.reshape(batch, seq, N_WIDTH)
    gv = g_all.reshape(batch, seq, G_WIDTH)
    return pl.pallas_call(
        _gla_kernel,
        grid=(batch, seq // rows),
        in_specs=[
            pl.BlockSpec((1, rows, qk_w), lambda b, c: (b, c, N_CQ_COL // qk_w)),
            pl.BlockSpec((1, rows, qk_w), lambda b, c: (b, c, N_CK_COL // qk_w)),
            pl.BlockSpec((1, rows, v_w), lambda b, c: (b, c, N_CV_COL // v_w)),
            pl.BlockSpec((1, rows, LANES), lambda b, c: (b, c, G_ALR_COL // LANES)),
            pl.BlockSpec((1, rows, v_w), lambda b, c: (b, c, G_R_COL // v_w)),
            pl.BlockSpec((LANES, qk_w), lambda b, c: (0, 0)),
            pl.BlockSpec((1, qk_w), lambda b, c: (0, 0)),
            pl.BlockSpec((1, C_DV), lambda b, c: (0, 0)),
        ],
        out_specs=pl.BlockSpec((1, rows, v_w), lambda b, c: (b, c, 0)),
        out_shape=jax.ShapeDtypeStruct((batch, seq, v_w), BF16),
        scratch_shapes=[pltpu.VMEM((C_HEADS, C_DV, C_DK), F32)],
        compiler_params=_cparams(("parallel", "arbitrary")),
        name="gla",
    )(cv, cv, cv, gv, gv, aw, ab, hn).reshape(batch * seq, v_w)


def _merge_kernel(oa_ref, ob1_ref, ob2_ref, ob3_ref, l1_ref, l2_ref, l3_ref, oc_ref, x_ref,
                  gm_ref, ex_ref, wg_ref, wb_ref, wo_ref, gn_ref, xo_ref, xnt_ref):
    l1, l2, l3 = l1_ref[...], l2_ref[...], l3_ref[...]
    lm = jnp.maximum(jnp.maximum(l1, l2), l3)
    e1, e2, e3 = jnp.exp(l1 - lm), jnp.exp(l2 - lm), jnp.exp(l3 - lm)
    inv = 1.0 / (e1 + e2 + e3)

    def widen(wc):
        hi = wc.astype(BF16)
        r1 = wc - hi.astype(F32)
        mid = r1.astype(BF16)
        lo = (r1 - mid.astype(F32)).astype(BF16)
        ex = ex_ref[...]
        return (jnp.dot(hi, ex, preferred_element_type=F32) + jnp.dot(mid, ex, preferred_element_type=F32)
                + jnp.dot(lo, ex, preferred_element_type=F32))

    ob = (widen(e1 * inv) * ob1_ref[...].astype(F32) + widen(e2 * inv) * ob2_ref[...].astype(F32)
          + widen(e3 * inv) * ob3_ref[...].astype(F32))
    branches = (oa_ref[...], ob.astype(BF16), oc_ref[...])
    x = x_ref[...]
    h = (x * lax.rsqrt(jnp.mean(x * x, axis=-1, keepdims=True) + RMS_EPS) * gm_ref[...]).astype(BF16)
    merged = None
    for n in range(N_BRANCHES):
        gate = jnp.dot(h, wg_ref[:, n * D_MODEL:(n + 1) * D_MODEL], preferred_element_type=F32)
        proj = jnp.dot(branches[n], wb_ref[n], preferred_element_type=F32)
        term = jax.nn.sigmoid(gate) * proj
        merged = term if merged is None else merged + term
    x = x + jnp.dot(merged.astype(BF16), wo_ref[...], preferred_element_type=F32)
    xo_ref[...] = x
    xn = x * lax.rsqrt(jnp.mean(x * x, axis=-1, keepdims=True) + RMS_EPS) * gn_ref[...]
    xnt_ref[...] = xn.T.astype(BF16)


def _head_expander():
    src = np.arange(LANES)[:, None]
    dst = np.arange(BRANCH_WIDTH)[None, :]
    return jnp.asarray(src == (dst // HEAD_DIM) * LSE_LANES, dtype=BF16)


def _merge(o_a, o_b, lse_b, o_c, x2, gm, wg, wb, wo, gn):
    rows = x2.shape[0]
    tm = min(512, rows)
    row_spec = lambda w: pl.BlockSpec((tm, w), lambda i: (i, 0))
    vec_spec = pl.BlockSpec((1, D_MODEL), lambda i: (0, 0))
    return pl.pallas_call(
        _merge_kernel,
        grid=(rows // tm,),
        in_specs=[row_spec(BRANCH_WIDTH)] * 4 + [row_spec(LANES)] * 3 + [
            row_spec(BRANCH_WIDTH),
            row_spec(D_MODEL),
            vec_spec,
            pl.BlockSpec((LANES, BRANCH_WIDTH), lambda i: (0, 0)),
            pl.BlockSpec((D_MODEL, G_GATES), lambda i: (0, 0)),
            pl.BlockSpec((N_BRANCHES, BRANCH_WIDTH, D_MODEL), lambda i: (0, 0, 0)),
            pl.BlockSpec((D_MODEL, D_MODEL), lambda i: (0, 0)),
            vec_spec,
        ],
        out_specs=[row_spec(D_MODEL), pl.BlockSpec((D_MODEL, tm), lambda i: (0, i))],
        out_shape=[jax.ShapeDtypeStruct((rows, D_MODEL), F32),
                   jax.ShapeDtypeStruct((D_MODEL, rows), BF16)],
        compiler_params=_cparams(("parallel",)),
        name="merge",
    )(o_a, o_b[0], o_b[1], o_b[2], lse_b[0], lse_b[1], lse_b[2], o_c, x2, gm.reshape(1, D_MODEL),
      _head_expander(), wg, wb, wo, gn.reshape(1, D_MODEL))


_PAIRS = tuple((i, j) for i in range(PEER_TOPK) for j in range(PEER_TOPK)
               if (i + 1) * (j + 1) <= PEER_TOPK)


def _sort16_desc(v):
    v = list(v)
    n = len(v)
    k = 2
    while k <= n:
        j = k // 2
        while j >= 1:
            for i in range(n):
                l = i ^ j
                if l > i:
                    hi, lo = jnp.maximum(v[i], v[l]), jnp.minimum(v[i], v[l])
                    v[i], v[l] = (hi, lo) if (i & k) == 0 else (lo, hi)
            j //= 2
        k *= 2
    return v


def _merge_top16(a, b):
    n = len(a)
    v = [jnp.maximum(a[i], b[n - 1 - i]) for i in range(n)]
    j = n // 2
    while j >= 1:
        for i in range(n):
            l = i ^ j
            if l > i:
                v[i], v[l] = jnp.maximum(v[i], v[l]), jnp.minimum(v[i], v[l])
        j //= 2
    return v


def _top16_of(vals):
    acc = None
    for g in range(len(vals) // PEER_TOPK):
        grp = _sort16_desc(vals[g * PEER_TOPK:(g + 1) * PEER_TOPK])
        acc = grp if acc is None else _merge_top16(acc, grp)
    return acc


def _peer_select_kernel(xnt_ref, wq_ref, kb_ref, cnt_ref, e1_ref, rank_ref, e2_ref,
                        sc_ref, rk_ref, eb_ref, *, tb):
    nk, nh, kk = PEER_KEYS, PEER_HEADS, PEER_TOPK
    xnt = xnt_ref[...]
    for p in range(2):
        qy = jnp.dot(wq_ref[p], xnt, preferred_element_type=F32).astype(BF16)
        sc_ref[p] = jnp.dot(kb_ref[p], qy, preferred_element_type=F32).reshape(nk, nh, tb)

    for c in range(tb // LANES):
        cs = slice(c * LANES, (c + 1) * LANES)
        s1s = _top16_of([sc_ref[0, n, :, cs] for n in range(nk)])
        s2s = _top16_of([sc_ref[1, n, :, cs] for n in range(nk)])
        cands = [s1s[i] + s2s[j] for (i, j) in _PAIRS]
        pad = [jnp.full_like(cands[0], -jnp.inf)] * (-len(cands) % kk)
        tau = _top16_of(cands + pad)[kk - 1]
        top = cands[0]
        zsum = None
        counts = [None] * kk
        for (i, j), cd in zip(_PAIRS, cands):
            sel = cd >= tau
            term = jnp.where(sel, jnp.exp(cd - top), 0.0)
            zsum = term if zsum is None else zsum + term
            one = jnp.where(sel, 1.0, 0.0)
            counts[i] = one if counts[i] is None else counts[i] + one
        inv_z = 1.0 / zsum

        def first_half(n, carry):
            s1 = sc_ref[0, n, :, cs]
            cnt = jnp.zeros_like(s1)
            for i in range(kk):
                cnt = jnp.where(s1 == s1s[i], counts[i], cnt)
            r0 = pl.multiple_of(n * nh, nh)
            cnt_ref[pl.ds(r0, nh), cs] = cnt
            e1_ref[pl.ds(r0, nh), cs] = jnp.exp(s1 - s1s[0])
            return carry

        def second_half(n, carry):
            s2 = sc_ref[1, n, :, cs]
            rank = jnp.full_like(s2, float(kk))
            for j in range(kk):
                rank = jnp.where(s2 == s2s[j], float(j), rank)
            r0 = pl.multiple_of(n * nh, nh)
            rk_ref[pl.ds(r0, nh), :] = rank
            eb_ref[pl.ds(r0, nh), :] = jnp.exp(s2 - s2s[0]) * inv_z
            return carry

        lax.fori_loop(0, nk, first_half, 0, unroll=4)
        lax.fori_loop(0, nk, second_half, 0, unroll=4)
        for h in range(nh):
            rank_ref[h * nk:(h + 1) * nk, cs] = rk_ref[pl.ds(h, nk, stride=nh), :].astype(BF16)
            e2_ref[h * nk:(h + 1) * nk, cs] = eb_ref[pl.ds(h, nk, stride=nh), :].astype(BF16)


def _peer_select(xnt, wq_t, k_big):
    d, rows = xnt.shape
    tb = min(256, rows)
    nrow = PEER_KEYS * PEER_HEADS
    tab_spec = pl.BlockSpec((nrow, tb), lambda i: (0, i))
    return pl.pallas_call(
        functools.partial(_peer_select_kernel, tb=tb),
        grid=(rows // tb,),
        in_specs=[
            pl.BlockSpec((d, tb), lambda i: (0, i)),
            pl.BlockSpec((2, nrow, d), lambda i: (0, 0, 0)),
            pl.BlockSpec((2, nrow, nrow), lambda i: (0, 0, 0)),
        ],
        out_specs=[tab_spec] * 4,
        out_shape=[jax.ShapeDtypeStruct((nrow, rows), dt) for dt in (F32, F32, BF16, BF16)],
        scratch_shapes=[
            pltpu.VMEM((2, PEER_KEYS, PEER_HEADS, tb), F32),
            pltpu.VMEM((nrow, LANES), F32),
            pltpu.VMEM((nrow, LANES), F32),
        ],
        compiler_params=_cparams(("parallel",)),
        name="peer_select",
    )(xnt, wq_t, k_big)


def _gelu_tanh(x):
    c = math.sqrt(2.0 / math.pi)
    return 0.5 * x * (1.0 + jnp.tanh(c * (x + 0.044715 * (x * x * x))))


def _peer_dense_kernel(xnt_ref, u_ref, vt_ref, cnt_ref, e1_ref, rank_ref, e2_ref, x_ref, gn_ref,
                       o_ref, acc_ref, ht_ref, w_ref, *, tb, eb, slab, final_norm):
    j = pl.program_id(1)
    nk, nh = PEER_KEYS, PEER_HEADS
    pk = 16
    n_slab = eb // slab

    @pl.when(j == 0)
    def _():
        acc_ref[...] = jnp.zeros_like(acc_ref)

    def hidden(s):
        rows = slice(s * slab, (s + 1) * slab)
        ht_ref[rows, :] = jnp.dot(u_ref[rows, :], xnt_ref[...], preferred_element_type=F32)

    def gate(s):
        for al in range(s * slab // nk, (s + 1) * slab // nk):
            rows = slice(al * nk, (al + 1) * nk)
            for c in range(tb // LANES):
                cs = slice(c * LANES, (c + 1) * LANES)
                cnt8 = cnt_ref[al * nh:(al + 1) * nh, cs]
                e18 = e1_ref[al * nh:(al + 1) * nh, cs]
                act = _gelu_tanh(ht_ref[rows, cs].astype(BF16)).reshape(nk // pk, pk, LANES)
                g = jnp.zeros((nk // pk, pk, LANES), BF16)
                for h in range(nh):
                    cb = jnp.broadcast_to(cnt8[h:h + 1, :], (pk, LANES)).astype(BF16)
                    eb_ = jnp.broadcast_to(e18[h:h + 1, :], (pk, LANES)).astype(BF16)
                    rk = rank_ref[h * nk:(h + 1) * nk, cs].reshape(nk // pk, pk, LANES)
                    e2 = e2_ref[h * nk:(h + 1) * nk, cs].reshape(nk // pk, pk, LANES)
                    g = g + jnp.where(rk < cb[None], e2, jnp.zeros_like(e2)) * eb_[None]
                w_ref[rows, cs] = (act * g).reshape(nk, LANES)

    def project(s):
        rows = slice(s * slab, (s + 1) * slab)
        acc_ref[...] += jnp.dot(vt_ref[:, rows], w_ref[rows, :], preferred_element_type=F32)

    hidden(0)
    for s in range(n_slab):
        if s + 1 < n_slab:
            hidden(s + 1)
        gate(s)
        if s >= 1:
            project(s - 1)
    project(n_slab - 1)

    @pl.when(j == pl.num_programs(1) - 1)
    def _():
        x = x_ref[...] + acc_ref[...].T
        if final_norm:
            x = x * lax.rsqrt(jnp.mean(x * x, axis=-1, keepdims=True) + RMS_EPS) * gn_ref[...]
        o_ref[...] = x


def _peer_dense(xnt, u, vt, cnt, e1, rank, e2, x2, gn, final_norm):
    d, rows = xnt.shape
    ne = u.shape[0]
    tb = min(512, rows)
    eb = 2048
    slab = 1024
    nrow = PEER_KEYS * PEER_HEADS
    arow = (eb // PEER_KEYS) * PEER_HEADS
    return pl.pallas_call(
        functools.partial(_peer_dense_kernel, tb=tb, eb=eb, slab=slab, final_norm=final_norm),
        grid=(rows // tb, ne // eb),
        in_specs=[
            pl.BlockSpec((d, tb), lambda i, j: (0, i)),
            pl.BlockSpec((eb, d), lambda i, j: (j, 0)),
            pl.BlockSpec((d, eb), lambda i, j: (0, j)),
            pl.BlockSpec((arow, tb), lambda i, j: (j, i)),
            pl.BlockSpec((arow, tb), lambda i, j: (j, i)),
            pl.BlockSpec((nrow, tb), lambda i, j: (0, i)),
            pl.BlockSpec((nrow, tb), lambda i, j: (0, i)),
            pl.BlockSpec((tb, d), lambda i, j: (i, 0)),
            pl.BlockSpec((1, d), lambda i, j: (0, 0)),
        ],
        out_specs=pl.BlockSpec((tb, d), lambda i, j: (i, 0)),
        out_shape=jax.ShapeDtypeStruct((rows, d), F32),
        scratch_shapes=[
            pltpu.VMEM((d, tb), F32),
            pltpu.VMEM((eb, tb), F32),
            pltpu.VMEM((eb, tb), BF16),
        ],
        compiler_params=_cparams(("parallel", "arbitrary")),
        name="peer_dense",
    )(xnt, u, vt, cnt, e1, rank, e2, x2, gn.reshape(1, d))


def _t5_bucket(dist):
    max_exact = N_BUCKETS // 2
    large = max_exact + (jnp.log(jnp.maximum(dist, 1).astype(F32) / max_exact)
                         / math.log(BUCKET_MAX_DIST / max_exact) * (N_BUCKETS - max_exact)).astype(jnp.int32)
    large = jnp.minimum(large, N_BUCKETS - 1)
    return jnp.where(dist < max_exact, dist, large)


def _rel_bias(table_cols, dilation):
    qi = jnp.arange(BLOCK)[:, None]
    kj = jnp.arange(2 * BLOCK)[None, :]
    dist = jnp.maximum(qi + BLOCK - kj, 0) * dilation
    onehot = (_t5_bucket(dist)[..., None] == jnp.arange(N_BUCKETS)).astype(F32)
    return jnp.einsum('qkb,bh->hqk', onehot, table_cols.astype(F32), precision=lax.Precision.HIGHEST)


def _split_w_in(w):
    hd = HEAD_DIM
    sizes = [A_Q_HEADS * hd, A_KV_HEADS * hd, A_KV_HEADS * hd] + [B_HEADS * hd] * 9 + [
        C_HEADS * C_DK, C_HEADS * C_DK, C_HEADS * C_DV, C_GATE_RANK, C_HEADS * C_DV, N_BRANCHES * D_MODEL]
    offs = np.concatenate([[0], np.cumsum(sizes)])
    col = lambda a, b: w[:, int(offs[a]):int(offs[b])].astype(BF16)
    groups = [col(3 + 3 * g, 6 + 3 * g) for g in range(3)]
    w_nat = jnp.concatenate([groups[0], col(14, 15), col(0, 3), col(12, 14)], axis=1)
    pad = jnp.zeros((w.shape[0], G_WIDTH - G_ALR_COL - C_GATE_RANK), BF16)
    w_r = jnp.concatenate([col(16, 17), col(15, 16), pad], axis=1)
    return w_nat, w_r, groups, col(17, 18)


def _peer_key_matrix(keys):
    nh, _, nk, c = keys.shape
    eye = jnp.eye(nh, dtype=keys.dtype)
    big = jnp.einsum('hpnc,hg->pnhgc', keys, eye)
    return big.reshape(2, nk * nh, nh * c).astype(BF16)


def _residue_major(x2, batch, seq, dil):
    if dil == 1:
        return x2
    d = x2.shape[-1]
    return x2.reshape(batch, seq // dil, dil, d).transpose(0, 2, 1, 3).reshape(batch * seq, d)


def kernel(x, w_in, attn_sinks, gla_alpha_w, gla_alpha_b, gla_head_norm, w_branch, w_out, norm_mix,
           norm_ffn, peer_wq, peer_keys, peer_u, peer_v, rel_bias_table, norm_final):
    batch, seq, d = x.shape
    rows = batch * seq
    depth = w_in.shape[0]
    hw = B_HEADS * HEAD_DIM

    bias_a = _banded_bias(_rel_bias(rel_bias_table[:, :A_Q_HEADS], 1), A_WINDOW - 1)
    bias_b = [_banded_bias(
        _rel_bias(rel_bias_table[:, A_Q_HEADS + i * B_HEADS:A_Q_HEADS + (i + 1) * B_HEADS], dil), window // dil)
        for i, (window, dil) in enumerate(B_PATTERNS)]

    x2 = x.reshape(rows, d)
    for l in range(depth):
        w_nat, w_r, w_groups, w_gates = _split_w_in(w_in[l])
        nat, g_all = _norm_matmul(x2, norm_mix[l], [w_nat, w_r], [BF16, F32], "proj_nat")

        o_a = _banded_attention(
            nat, bias_a, attn_sinks[l], batch=batch, seq=seq, dil=1, width=N_WIDTH,
            q_col=N_A_COL, k_col=N_A_COL + A_Q_HEADS * HEAD_DIM,
            v_col=N_A_COL + (A_Q_HEADS + A_KV_HEADS) * HEAD_DIM,
            hq=A_Q_HEADS, grp=A_Q_HEADS // A_KV_HEADS, want_lse=False, out_dtype=BF16, name="attn_a")
        o_b, lse_b = [], []
        for gi, (window, dil) in enumerate(B_PATTERNS):
            if dil == 1:
                b_qkv, width, col0 = nat, N_WIDTH, N_B0_COL
            else:
                b_qkv, = _norm_matmul(_residue_major(x2, batch, seq, dil), norm_mix[l], [w_groups[gi]], [BF16],
                                      "proj_b%d" % gi)
                width, col0 = GROUP_WIDTH, 0
            o, lse = _banded_attention(
                b_qkv, bias_b[gi], None, batch=batch, seq=seq, dil=dil, width=width,
                q_col=col0, k_col=col0 + hw, v_col=col0 + 2 * hw, hq=B_HEADS, grp=1,
                want_lse=True, out_dtype=BF16, name="attn_b%d" % gi)
            o_b.append(o)
            lse_b.append(lse)

        aw = jnp.zeros((LANES, C_HEADS * C_DK), BF16).at[:C_GATE_RANK].set(gla_alpha_w[l].astype(BF16))
        o_c = _gla(nat, g_all, aw, gla_alpha_b[l].reshape(1, -1), gla_head_norm[l].reshape(1, -1),
                   batch=batch, seq=seq)

        x2, xnt = _merge(o_a, o_b, lse_b, o_c, x2, norm_mix[l], w_gates, w_branch[l].astype(BF16),
                         w_out[l].astype(BF16), norm_ffn[l])

        wq_t = peer_wq[l].reshape(d, PEER_HEADS, 2, PEER_DKEY // 2).transpose(2, 1, 3, 0)
        wq_t = wq_t.reshape(2, PEER_HEADS * (PEER_DKEY // 2), d).astype(BF16)
        cnt, e1, rank, e2 = _peer_select(xnt, wq_t, _peer_key_matrix(peer_keys[l]))
        last = l == depth - 1
        x2 = _peer_dense(xnt, peer_u[l].astype(BF16), peer_v[l].T.astype(BF16), cnt, e1, rank, e2,
                         x2, norm_final, final_norm=last)
    return x2.reshape(batch, seq, d)
```

```python
import functools
import math

import numpy as np
import jax
import jax.numpy as jnp
from jax import lax
from jax.experimental import pallas as pl
from jax.experimental.pallas import tpu as pltpu

F32 = jnp.float32
BF16 = jnp.bfloat16

D_MODEL = 1024
HEAD_DIM = 64
BLOCK = 128
A_Q_HEADS = 8
A_KV_HEADS = 2
A_WINDOW = 128
B_PATTERNS = ((128, 1), (512, 4), (2048, 16))
B_HEADS = 8
C_HEADS = 4
C_DK = 64
C_DV = 128
C_GATE_RANK = 16
C_GATE_TAU = 16.0
C_CHUNK = 64
BRANCH_WIDTH = 512
N_BRANCHES = 3
N_BUCKETS = 32
BUCKET_MAX_DIST = 2048
PEER_HEADS = 8
PEER_KEYS = 128
PEER_N_EXPERTS = PEER_KEYS * PEER_KEYS
PEER_TOPK = 16
PEER_DKEY = 256
RMS_EPS = 1e-6
NEG_INF = -1e30

LANES = 128
VMEM_LIMIT = 56 * 1024 * 1024

GROUP_WIDTH = 3 * B_HEADS * HEAD_DIM
N_B0_COL = 0
N_CV_COL = GROUP_WIDTH
N_A_COL = N_CV_COL + C_HEADS * C_DV
N_CQ_COL = N_A_COL + (A_Q_HEADS + 2 * A_KV_HEADS) * HEAD_DIM
N_CK_COL = N_CQ_COL + C_HEADS * C_DK
N_WIDTH = N_CK_COL + C_HEADS * C_DK
G_GATES = N_BRANCHES * D_MODEL
G_WIDTH = C_HEADS * C_DV + LANES
G_R_COL = 0
G_ALR_COL = C_HEADS * C_DV
GLA_CHUNKS_PER_STEP = 8
ATTN_ROWS_PER_STEP = 2 * BLOCK
LSE_LANES = LANES // B_HEADS


def _cparams(sem):
    return pltpu.CompilerParams(dimension_semantics=sem, vmem_limit_bytes=VMEM_LIMIT)


def _norm_matmul_kernel(x_ref, g_ref, *refs):
    n_out = len(refs) // 2
    x = x_ref[...]
    ms = jnp.mean(x * x, axis=-1, keepdims=True)
    h = (x * lax.rsqrt(ms + RMS_EPS) * g_ref[...]).astype(BF16)
    for w_ref, o_ref in zip(refs[:n_out], refs[n_out:]):
        o_ref[...] = jnp.dot(h, w_ref[...], preferred_element_type=F32).astype(o_ref.dtype)


def _norm_matmul(x2, g, ws, out_dtypes, name):
    rows, d = x2.shape
    tm = min(512, rows)
    res = pl.pallas_call(
        _norm_matmul_kernel,
        grid=(rows // tm,),
        in_specs=[pl.BlockSpec((tm, d), lambda i: (i, 0)), pl.BlockSpec((1, d), lambda i: (0, 0))]
        + [pl.BlockSpec(w.shape, lambda i: (0, 0)) for w in ws],
        out_specs=[pl.BlockSpec((tm, w.shape[1]), lambda i: (i, 0)) for w in ws],
        out_shape=[jax.ShapeDtypeStruct((rows, w.shape[1]), dt) for w, dt in zip(ws, out_dtypes)],
        compiler_params=_cparams(("parallel",)),
        name=name,
    )(x2, g.reshape(1, d), *ws)
    return res


def _attn_kernel(*refs, hq, grp, has_sink, want_lse, nsb, nseq):
    q_ref, kp_ref, kc_ref, vp_ref, vc_ref, bias_ref = refs[:6]
    pos = 6
    sink_ref = None
    if has_sink:
        sink_ref = refs[pos]
        pos += 1
    o_ref = refs[pos]
    lse_ref = refs[pos + 1] if want_lse else None

    first = jnp.minimum(pl.program_id(2), 1)
    qw = hq * HEAD_DIM
    for sq in range(nseq):
        k = jnp.concatenate([kp_ref[sq], kc_ref[sq]], axis=0)
        v = jnp.concatenate([vp_ref[sq], vc_ref[sq]], axis=0)
        for sb in range(nsb):
            q = q_ref[sq, sb * BLOCK:(sb + 1) * BLOCK, :] * (HEAD_DIM ** -0.5)
            ks = k[sb * BLOCK:(sb + 2) * BLOCK]
            vs = v[sb * BLOCK:(sb + 2) * BLOCK]
            outs, lses = [], []
            for h in range(hq):
                kvh = h // grp
                qh = q[:, h * HEAD_DIM:(h + 1) * HEAD_DIM]
                kh = ks[:, kvh * HEAD_DIM:(kvh + 1) * HEAD_DIM]
                vh = vs[:, kvh * HEAD_DIM:(kvh + 1) * HEAD_DIM]
                s = lax.dot_general(qh, kh, (((1,), (1,)), ((), ())), preferred_element_type=F32)
                bias = bias_ref[first, h] if sb == 0 else bias_ref[1, h]
                s = s + bias
                m = jnp.max(s, axis=-1, keepdims=True)
                if has_sink:
                    m = jnp.maximum(m, sink_ref[h])
                p = jnp.exp(s - m)
                denom = jnp.sum(p, axis=-1, keepdims=True)
                if has_sink:
                    denom = denom + jnp.exp(sink_ref[h] - m)
                o = jnp.dot(p.astype(BF16), vh, preferred_element_type=F32) / denom
                outs.append(o)
                if want_lse:
                    lses.append(jnp.broadcast_to(m + jnp.log(denom), (BLOCK, LSE_LANES)))
            rs = slice(sb * BLOCK, (sb + 1) * BLOCK)
            o_ref[0, rs, sq * qw:(sq + 1) * qw] = jnp.concatenate(outs, axis=-1).astype(o_ref.dtype)
            if want_lse:
                lse_ref[0, rs, sq * LANES:(sq + 1) * LANES] = jnp.concatenate(lses, axis=-1)


def _banded_bias(bias, max_dist):
    qi = np.arange(BLOCK)[:, None]
    kj = np.arange(2 * BLOCK)[None, :]
    dist = qi + BLOCK - kj
    in_window = (dist >= 0) & (dist <= max_dist)
    first = in_window & (kj >= BLOCK)
    return jnp.stack([jnp.where(first, bias, NEG_INF), jnp.where(in_window, bias, NEG_INF)])


def _banded_attention(qkv, bias, sink, *, batch, seq, dil, width, q_col, k_col, v_col,
                      hq, grp, want_lse, out_dtype, name):
    sub = seq // dil
    rows = min(ATTN_ROWS_PER_STEP, sub)
    nsb = rows // BLOCK
    nseq = min(ATTN_ROWS_PER_STEP // rows, dil)
    qw = hq * HEAD_DIM
    kw = (hq // grp) * HEAD_DIM
    view = qkv.reshape(batch * dil, sub, width)
    q_blk, k_blk, v_blk = q_col // qw, k_col // kw, v_col // kw
    assert q_col % qw == 0 and k_col % kw == 0 and v_col % kw == 0
    groups = dil // nseq

    prev_spec = lambda c: pl.BlockSpec((nseq, BLOCK, kw),
                                       lambda b, r, i: (b * groups + r, jnp.maximum(i * nsb - 1, 0), c))
    cur_spec = lambda w, c: pl.BlockSpec((nseq, rows, w), lambda b, r, i: (b * groups + r, i, c))
    in_specs = [
        cur_spec(qw, q_blk), prev_spec(k_blk), cur_spec(kw, k_blk), prev_spec(v_blk), cur_spec(kw, v_blk),
        pl.BlockSpec((2, hq, BLOCK, 2 * BLOCK), lambda b, r, i: (0, 0, 0, 0)),
    ]
    args = [view, view, view, view, view, bias]
    if sink is not None:
        in_specs.append(pl.BlockSpec(memory_space=pltpu.SMEM))
        args.append(sink)
    o_spec = pl.BlockSpec((1, rows, nseq * qw), lambda b, r, i: (b, i, r))
    o_shape = jax.ShapeDtypeStruct((batch, sub, dil * qw), out_dtype)
    out_specs, out_shape = o_spec, o_shape
    if want_lse:
        out_specs = [o_spec, pl.BlockSpec((1, rows, nseq * LANES), lambda b, r, i: (b, i, r))]
        out_shape = [o_shape, jax.ShapeDtypeStruct((batch, sub, dil * LANES), F32)]
    res = pl.pallas_call(
        functools.partial(_attn_kernel, hq=hq, grp=grp, has_sink=sink is not None, want_lse=want_lse, nsb=nsb,
                          nseq=nseq),
        grid=(batch, groups, sub // rows),
        in_specs=in_specs,
        out_specs=out_specs,
        out_shape=out_shape,
        compiler_params=_cparams(("parallel", "parallel", "arbitrary")),
        name=name,
    )(*args)
    if want_lse:
        return res[0].reshape(batch * seq, qw), res[1].reshape(batch * seq, LANES)
    return res.reshape(batch * seq, qw)


def _gla_kernel(q_ref, k_ref, v_ref, alr_ref, r_ref, aw_ref, ab_ref, hn_ref, o_ref, st_ref):
    @pl.when(pl.program_id(1) == 0)
    def _():
        st_ref[...] = jnp.zeros_like(st_ref)

    ck = C_CHUNK
    z = jnp.dot(alr_ref[0].astype(BF16), aw_ref[...], preferred_element_type=F32) + ab_ref[...]
    log_a = (jnp.minimum(z, 0.0) - jnp.log1p(jnp.exp(-jnp.abs(z)))) * (1.0 / C_GATE_TAU)
    row = lax.broadcasted_iota(jnp.int32, log_a.shape, 0) & (ck - 1)
    cum = log_a
    shift = 1
    while shift < ck:
        cum = cum + jnp.where(row >= shift, pltpu.roll(cum, shift, 0), 0.0)
        shift *= 2
    q_all = q_ref[0].astype(F32) * (C_DK ** -0.5)
    k_all = k_ref[0].astype(F32)
    q_dec_all = (q_all * jnp.exp(cum)).astype(BF16)
    k_inv_all = (k_all * jnp.exp(-cum)).astype(BF16)
    ti = lax.broadcasted_iota(jnp.int32, (ck, ck), 0)
    si = lax.broadcasted_iota(jnp.int32, (ck, ck), 1)
    causal = ti >= si
    nt = (((1,), (1,)), ((), ()))
    tn = (((0,), (0,)), ((), ()))
    for c in range(GLA_CHUNKS_PER_STEP):
        rs = slice(c * ck, (c + 1) * ck)
        cum_c = cum[rs]
        last = cum_c[ck - 1:ck, :]
        q_dec, k_inv = q_dec_all[rs], k_inv_all[rs]
        k_end = (k_all[rs] * jnp.exp(last - cum_c)).astype(BF16)
        decay = jnp.exp(last)
        v = v_ref[0, rs, :]
        r = r_ref[0, rs, :]
        outs = []
        for h in range(C_HEADS):
            ks = slice(h * C_DK, (h + 1) * C_DK)
            vs = slice(h * C_DV, (h + 1) * C_DV)
            qd, ki, ke, vh = q_dec[:, ks], k_inv[:, ks], k_end[:, ks], v[:, vs]
            att = jnp.where(causal, lax.dot_general(qd, ki, nt, preferred_element_type=F32), 0.0)
            st = st_ref[h]
            o = jnp.dot(att.astype(BF16), vh, preferred_element_type=F32)
            o = o + lax.dot_general(qd, st.astype(BF16), nt, preferred_element_type=F32)
            kv_t = lax.dot_general(vh, ke, tn, preferred_element_type=F32)
            st_ref[h] = st * decay[:, ks] + kv_t
            o = o * lax.rsqrt(jnp.mean(o * o, axis=-1, keepdims=True) + RMS_EPS) * hn_ref[...]
            rh = r[:, vs]
            outs.append(o * (rh * jax.nn.sigmoid(rh)))
        o_ref[0, rs, :] = jnp.concatenate(outs, axis=-1).astype(o_ref.dtype)


def _gla(c_qkv, g_all, aw, ab, hn, *, batch, seq):
    rows = C_CHUNK * GLA_CHUNKS_PER_STEP
    qk_w = C_HEADS * C_DK
    v_w = C_HEADS * C_DV
    cv = c_qkv.reshape(batch, seq, N_WIDTH)
    gv = g_all.reshape(batch, seq, G_WIDTH)
    return pl.pallas_call(
        _gla_kernel,
        grid=(batch, seq // rows),
        in_specs=[
            pl.BlockSpec((1, rows, qk_w), lambda b, c: (b, c, N_CQ_COL // qk_w)),
            pl.BlockSpec((1, rows, qk_w), lambda b, c: (b, c, N_CK_COL // qk_w)),
            pl.BlockSpec((1, rows, v_w), lambda b, c: (b, c, N_CV_COL // v_w)),
            pl.BlockSpec((1, rows, LANES), lambda b, c: (b, c, G_ALR_COL // LANES)),
            pl.BlockSpec((1, rows, v_w), lambda b, c: (b, c, G_R_COL // v_w)),
            pl.BlockSpec((LANES, qk_w), lambda b, c: (0, 0)),
            pl.BlockSpec((1, qk_w), lambda b, c: (0, 0)),
            pl.BlockSpec((1, C_DV), lambda b, c: (0, 0)),
        ],
        out_specs=pl.BlockSpec((1, rows, v_w), lambda b, c: (b, c, 0)),
        out_shape=jax.ShapeDtypeStruct((batch, seq, v_w), BF16),
        scratch_shapes=[pltpu.VMEM((C_HEADS, C_DV, C_DK), F32)],
        compiler_params=_cparams(("parallel", "arbitrary")),
        name="gla",
    )(cv, cv, cv, gv, gv, aw, ab, hn).reshape(batch * seq, v_w)


def _merge_kernel(oa_ref, ob1_ref, ob2_ref, ob3_ref, l1_ref, l2_ref, l3_ref, oc_ref, x_ref,
                  gm_ref, ex_ref, wg_ref, wb_ref, wo_ref, gn_ref, xo_ref, xnt_ref):
    l1, l2, l3 = l1_ref[...], l2_ref[...], l3_ref[...]
    lm = jnp.maximum(jnp.maximum(l1, l2), l3)
    e1, e2, e3 = jnp.exp(l1 - lm), jnp.exp(l2 - lm), jnp.exp(l3 - lm)
    inv = 1.0 / (e1 + e2 + e3)

    def widen(wc):
        hi = wc.astype(BF16)
        r1 = wc - hi.astype(F32)
        mid = r1.astype(BF16)
        lo = (r1 - mid.astype(F32)).astype(BF16)
        ex = ex_ref[...]
        return (jnp.dot(hi, ex, preferred_element_type=F32) + jnp.dot(mid, ex, preferred_element_type=F32)
                + jnp.dot(lo, ex, preferred_element_type=F32))

    ob = (widen(e1 * inv) * ob1_ref[...].astype(F32) + widen(e2 * inv) * ob2_ref[...].astype(F32)
          + widen(e3 * inv) * ob3_ref[...].astype(F32))
    branches = (oa_ref[...], ob.astype(BF16), oc_ref[...])
    x = x_ref[...]
    h = (x * lax.rsqrt(jnp.mean(x * x, axis=-1, keepdims=True) + RMS_EPS) * gm_ref[...]).astype(BF16)
    merged = None
    for n in range(N_BRANCHES):
        gate = jnp.dot(h, wg_ref[:, n * D_MODEL:(n + 1) * D_MODEL], preferred_element_type=F32)
        proj = jnp.dot(branches[n], wb_ref[n], preferred_element_type=F32)
        term = jax.nn.sigmoid(gate) * proj
        merged = term if merged is None else merged + term
    x = x + jnp.dot(merged.astype(BF16), wo_ref[...], preferred_element_type=F32)
    xo_ref[...] = x
    xn = x * lax.rsqrt(jnp.mean(x * x, axis=-1, keepdims=True) + RMS_EPS) * gn_ref[...]
    xnt_ref[...] = xn.T.astype(BF16)


def _head_expander():
    src = np.arange(LANES)[:, None]
    dst = np.arange(BRANCH_WIDTH)[None, :]
    return jnp.asarray(src == (dst // HEAD_DIM) * LSE_LANES, dtype=BF16)


def _merge(o_a, o_b, lse_b, o_c, x2, gm, wg, wb, wo, gn):
    rows = x2.shape[0]
    tm = min(512, rows)
    row_spec = lambda w: pl.BlockSpec((tm, w), lambda i: (i, 0))
    vec_spec = pl.BlockSpec((1, D_MODEL), lambda i: (0, 0))
    return pl.pallas_call(
        _merge_kernel,
        grid=(rows // tm,),
        in_specs=[row_spec(BRANCH_WIDTH)] * 4 + [row_spec(LANES)] * 3 + [
            row_spec(BRANCH_WIDTH),
            row_spec(D_MODEL),
            vec_spec,
            pl.BlockSpec((LANES, BRANCH_WIDTH), lambda i: (0, 0)),
            pl.BlockSpec((D_MODEL, G_GATES), lambda i: (0, 0)),
            pl.BlockSpec((N_BRANCHES, BRANCH_WIDTH, D_MODEL), lambda i: (0, 0, 0)),
            pl.BlockSpec((D_MODEL, D_MODEL), lambda i: (0, 0)),
            vec_spec,
        ],
        out_specs=[row_spec(D_MODEL), pl.BlockSpec((D_MODEL, tm), lambda i: (0, i))],
        out_shape=[jax.ShapeDtypeStruct((rows, D_MODEL), F32),
                   jax.ShapeDtypeStruct((D_MODEL, rows), BF16)],
        compiler_params=_cparams(("parallel",)),
        name="merge",
    )(o_a, o_b[0], o_b[1], o_b[2], lse_b[0], lse_b[1], lse_b[2], o_c, x2, gm.reshape(1, D_MODEL),
      _head_expander(), wg, wb, wo, gn.reshape(1, D_MODEL))


_PAIRS = tuple((i, j) for i in range(PEER_TOPK) for j in range(PEER_TOPK)
               if (i + 1) * (j + 1) <= PEER_TOPK)


def _sort16_desc(v):
    v = list(v)
    n = len(v)
    k = 2
    while k <= n:
        j = k // 2
        while j >= 1:
            for i in range(n):
                l = i ^ j
                if l > i:
                    hi, lo = jnp.maximum(v[i], v[l]), jnp.minimum(v[i], v[l])
                    v[i], v[l] = (hi, lo) if (i & k) == 0 else (lo, hi)
            j //= 2
        k *= 2
    return v


def _merge_top16(a, b):
    n = len(a)
    v = [jnp.maximum(a[i], b[n - 1 - i]) for i in range(n)]
    j = n // 2
    while j >= 1:
        for i in range(n):
            l = i ^ j
            if l > i:
                v[i], v[l] = jnp.maximum(v[i], v[l]), jnp.minimum(v[i], v[l])
        j //= 2
    return v


def _top16_of(vals):
    acc = None
    for g in range(len(vals) // PEER_TOPK):
        grp = _sort16_desc(vals[g * PEER_TOPK:(g + 1) * PEER_TOPK])
        acc = grp if acc is None else _merge_top16(acc, grp)
    return acc


def _peer_select_kernel(xnt_ref, wq_ref, kb_ref, cnt_ref, e1_ref, rank_ref, e2_ref,
                        sc_ref, rk_ref, eb_ref, *, tb):
    nk, nh, kk = PEER_KEYS, PEER_HEADS, PEER_TOPK
    xnt = xnt_ref[...]
    for p in range(2):
        qy = jnp.dot(wq_ref[p], xnt, preferred_element_type=F32).astype(BF16)
        sc_ref[p] = jnp.dot(kb_ref[p], qy, preferred_element_type=F32).reshape(nk, nh, tb)

    for c in range(tb // LANES):
        cs = slice(c * LANES, (c + 1) * LANES)
        s1s = _top16_of([sc_ref[0, n, :, cs] for n in range(nk)])
        s2s = _top16_of([sc_ref[1, n, :, cs] for n in range(nk)])
        cands = [s1s[i] + s2s[j] for (i, j) in _PAIRS]
        pad = [jnp.full_like(cands[0], -jnp.inf)] * (-len(cands) % kk)
        tau = _top16_of(cands + pad)[kk - 1]
        top = cands[0]
        zsum = None
        counts = [None] * kk
        for (i, j), cd in zip(_PAIRS, cands):
            sel = cd >= tau
            term = jnp.where(sel, jnp.exp(cd - top), 0.0)
            zsum = term if zsum is None else zsum + term
            one = jnp.where(sel, 1.0, 0.0)
            counts[i] = one if counts[i] is None else counts[i] + one
        inv_z = 1.0 / zsum

        def first_half(n, carry):
            s1 = sc_ref[0, n, :, cs]
            cnt = jnp.zeros_like(s1)
            for i in range(kk):
                cnt = jnp.where(s1 == s1s[i], counts[i], cnt)
            r0 = pl.multiple_of(n * nh, nh)
            cnt_ref[pl.ds(r0, nh), cs] = cnt
            e1_ref[pl.ds(r0, nh), cs] = jnp.exp(s1 - s1s[0])
            return carry

        def second_half(n, carry):
            s2 = sc_ref[1, n, :, cs]
            rank = jnp.full_like(s2, float(kk))
            for j in range(kk):
                rank = jnp.where(s2 == s2s[j], float(j), rank)
            r0 = pl.multiple_of(n * nh, nh)
            rk_ref[pl.ds(r0, nh), :] = rank
            eb_ref[pl.ds(r0, nh), :] = jnp.exp(s2 - s2s[0]) * inv_z
            return carry

        lax.fori_loop(0, nk, first_half, 0, unroll=4)
        lax.fori_loop(0, nk, second_half, 0, unroll=4)
        for h in range(nh):
            rank_ref[h * nk:(h + 1) * nk, cs] = rk_ref[pl.ds(h, nk, stride=nh), :].astype(BF16)
            e2_ref[h * nk:(h + 1) * nk, cs] = eb_ref[pl.ds(h, nk, stride=nh), :].astype(BF16)


def _peer_select(xnt, wq_t, k_big):
    d, rows = xnt.shape
    tb = min(256, rows)
    nrow = PEER_KEYS * PEER_HEADS
    tab_spec = pl.BlockSpec((nrow, tb), lambda i: (0, i))
    return pl.pallas_call(
        functools.partial(_peer_select_kernel, tb=tb),
        grid=(rows // tb,),
        in_specs=[
            pl.BlockSpec((d, tb), lambda i: (0, i)),
            pl.BlockSpec((2, nrow, d), lambda i: (0, 0, 0)),
            pl.BlockSpec((2, nrow, nrow), lambda i: (0, 0, 0)),
        ],
        out_specs=[tab_spec] * 4,
        out_shape=[jax.ShapeDtypeStruct((nrow, rows), dt) for dt in (F32, F32, BF16, BF16)],
        scratch_shapes=[
            pltpu.VMEM((2, PEER_KEYS, PEER_HEADS, tb), F32),
            pltpu.VMEM((nrow, LANES), F32),
            pltpu.VMEM((nrow, LANES), F32),
        ],
        compiler_params=_cparams(("parallel",)),
        name="peer_select",
    )(xnt, wq_t, k_big)


def _gelu_tanh(x):
    c = math.sqrt(2.0 / math.pi)
    return 0.5 * x * (1.0 + jnp.tanh(c * (x + 0.044715 * (x * x * x))))


def _peer_dense_kernel(xnt_ref, u_ref, vt_ref, cnt_ref, e1_ref, rank_ref, e2_ref, x_ref, gn_ref,
                       o_ref, acc_ref, ht_ref, w_ref, *, tb, eb, slab, final_norm):
    j = pl.program_id(1)
    nk, nh = PEER_KEYS, PEER_HEADS
    pk = 16
    n_slab = eb // slab

    @pl.when(j == 0)
    def _():
        acc_ref[...] = jnp.zeros_like(acc_ref)

    def hidden(s):
        rows = slice(s * slab, (s + 1) * slab)
        ht_ref[rows, :] = jnp.dot(u_ref[rows, :], xnt_ref[...], preferred_element_type=F32)

    def gate(s):
        for al in range(s * slab // nk, (s + 1) * slab // nk):
            rows = slice(al * nk, (al + 1) * nk)
            for c in range(tb // LANES):
                cs = slice(c * LANES, (c + 1) * LANES)
                cnt8 = cnt_ref[al * nh:(al + 1) * nh, cs]
                e18 = e1_ref[al * nh:(al + 1) * nh, cs]
                act = _gelu_tanh(ht_ref[rows, cs].astype(BF16)).reshape(nk // pk, pk, LANES)
                g = jnp.zeros((nk // pk, pk, LANES), BF16)
                for h in range(nh):
                    cb = jnp.broadcast_to(cnt8[h:h + 1, :], (pk, LANES)).astype(BF16)
                    eb_ = jnp.broadcast_to(e18[h:h + 1, :], (pk, LANES)).astype(BF16)
                    rk = rank_ref[h * nk:(h + 1) * nk, cs].reshape(nk // pk, pk, LANES)
                    e2 = e2_ref[h * nk:(h + 1) * nk, cs].reshape(nk // pk, pk, LANES)
                    g = g + jnp.where(rk < cb[None], e2, jnp.zeros_like(e2)) * eb_[None]
                w_ref[rows, cs] = (act * g).reshape(nk, LANES)

    def project(s):
        rows = slice(s * slab, (s + 1) * slab)
        acc_ref[...] += jnp.dot(vt_ref[:, rows], w_ref[rows, :], preferred_element_type=F32)

    hidden(0)
    for s in range(n_slab):
        if s + 1 < n_slab:
            hidden(s + 1)
        gate(s)
        if s >= 1:
            project(s - 1)
    project(n_slab - 1)

    @pl.when(j == pl.num_programs(1) - 1)
    def _():
        x = x_ref[...] + acc_ref[...].T
        if final_norm:
            x = x * lax.rsqrt(jnp.mean(x * x, axis=-1, keepdims=True) + RMS_EPS) * gn_ref[...]
        o_ref[...] = x


def _peer_dense(xnt, u, vt, cnt, e1, rank, e2, x2, gn, final_norm):
    d, rows = xnt.shape
    ne = u.shape[0]
    tb = min(512, rows)
    eb = 2048
    slab = 1024
    nrow = PEER_KEYS * PEER_HEADS
    arow = (eb // PEER_KEYS) * PEER_HEADS
    return pl.pallas_call(
        functools.partial(_peer_dense_kernel, tb=tb, eb=eb, slab=slab, final_norm=final_norm),
        grid=(rows // tb, ne // eb),
        in_specs=[
            pl.BlockSpec((d, tb), lambda i, j: (0, i)),
            pl.BlockSpec((eb, d), lambda i, j: (j, 0)),
            pl.BlockSpec((d, eb), lambda i, j: (0, j)),
            pl.BlockSpec((arow, tb), lambda i, j: (j, i)),
            pl.BlockSpec((arow, tb), lambda i, j: (j, i)),
            pl.BlockSpec((nrow, tb), lambda i, j: (0, i)),
            pl.BlockSpec((nrow, tb), lambda i, j: (0, i)),
            pl.BlockSpec((tb, d), lambda i, j: (i, 0)),
            pl.BlockSpec((1, d), lambda i, j: (0, 0)),
        ],
        out_specs=pl.BlockSpec((tb, d), lambda i, j: (i, 0)),
        out_shape=jax.ShapeDtypeStruct((rows, d), F32),
        scratch_shapes=[
            pltpu.VMEM((d, tb), F32),
            pltpu.VMEM((eb, tb), F32),
            pltpu.VMEM((eb, tb), BF16),
        ],
        compiler_params=_cparams(("parallel", "arbitrary")),
        name="peer_dense",
    )(xnt, u, vt, cnt, e1, rank, e2, x2, gn.reshape(1, d))


def _t5_bucket(dist):
    max_exact = N_BUCKETS // 2
    large = max_exact + (jnp.log(jnp.maximum(dist, 1).astype(F32) / max_exact)
                         / math.log(BUCKET_MAX_DIST / max_exact) * (N_BUCKETS - max_exact)).astype(jnp.int32)
    large = jnp.minimum(large, N_BUCKETS - 1)
    return jnp.where(dist < max_exact, dist, large)


def _rel_bias(table_cols, dilation):
    qi = jnp.arange(BLOCK)[:, None]
    kj = jnp.arange(2 * BLOCK)[None, :]
    dist = jnp.maximum(qi + BLOCK - kj, 0) * dilation
    onehot = (_t5_bucket(dist)[..., None] == jnp.arange(N_BUCKETS)).astype(F32)
    return jnp.einsum('qkb,bh->hqk', onehot, table_cols.astype(F32), precision=lax.Precision.HIGHEST)


def _split_w_in(w):
    hd = HEAD_DIM
    sizes = [A_Q_HEADS * hd, A_KV_HEADS * hd, A_KV_HEADS * hd] + [B_HEADS * hd] * 9 + [
        C_HEADS * C_DK, C_HEADS * C_DK, C_HEADS * C_DV, C_GATE_RANK, C_HEADS * C_DV, N_BRANCHES * D_MODEL]
    offs = np.concatenate([[0], np.cumsum(sizes)])
    col = lambda a, b: w[:, int(offs[a]):int(offs[b])].astype(BF16)
    groups = [col(3 + 3 * g, 6 + 3 * g) for g in range(3)]
    w_nat = jnp.concatenate([groups[0], col(14, 15), col(0, 3), col(12, 14)], axis=1)
    pad = jnp.zeros((w.shape[0], G_WIDTH - G_ALR_COL - C_GATE_RANK), BF16)
    w_r = jnp.concatenate([col(16, 17), col(15, 16), pad], axis=1)
    return w_nat, w_r, groups, col(17, 18)


def _peer_key_matrix(keys):
    nh, _, nk, c = keys.shape
    eye = jnp.eye(nh, dtype=keys.dtype)
    big = jnp.einsum('hpnc,hg->pnhgc', keys, eye)
    return big.reshape(2, nk * nh, nh * c).astype(BF16)


def _residue_major(x2, batch, seq, dil):
    if dil == 1:
        return x2
    d = x2.shape[-1]
    return x2.reshape(batch, seq // dil, dil, d).transpose(0, 2, 1, 3).reshape(batch * seq, d)


def kernel(x, w_in, attn_sinks, gla_alpha_w, gla_alpha_b, gla_head_norm, w_branch, w_out, norm_mix,
           norm_ffn, peer_wq, peer_keys, peer_u, peer_v, rel_bias_table, norm_final):
    batch, seq, d = x.shape
    rows = batch * seq
    depth = w_in.shape[0]
    hw = B_HEADS * HEAD_DIM

    bias_a = _banded_bias(_rel_bias(rel_bias_table[:, :A_Q_HEADS], 1), A_WINDOW - 1)
    bias_b = [_banded_bias(
        _rel_bias(rel_bias_table[:, A_Q_HEADS + i * B_HEADS:A_Q_HEADS + (i + 1) * B_HEADS], dil), window // dil)
        for i, (window, dil) in enumerate(B_PATTERNS)]

    x2 = x.reshape(rows, d)
    for l in range(depth):
        w_nat, w_r, w_groups, w_gates = _split_w_in(w_in[l])
        nat, g_all = _norm_matmul(x2, norm_mix[l], [w_nat, w_r], [BF16, F32], "proj_nat")

        o_a = _banded_attention(
            nat, bias_a, attn_sinks[l], batch=batch, seq=seq, dil=1, width=N_WIDTH,
            q_col=N_A_COL, k_col=N_A_COL + A_Q_HEADS * HEAD_DIM,
            v_col=N_A_COL + (A_Q_HEADS + A_KV_HEADS) * HEAD_DIM,
            hq=A_Q_HEADS, grp=A_Q_HEADS // A_KV_HEADS, want_lse=False, out_dtype=BF16, name="attn_a")
        o_b, lse_b = [], []
        for gi, (window, dil) in enumerate(B_PATTERNS):
            if dil == 1:
                b_qkv, width, col0 = nat, N_WIDTH, N_B0_COL
            else:
                b_qkv, = _norm_matmul(_residue_major(x2, batch, seq, dil), norm_mix[l], [w_groups[gi]], [BF16],
                                      "proj_b%d" % gi)
                width, col0 = GROUP_WIDTH, 0
            o, lse = _banded_attention(
                b_qkv, bias_b[gi], None, batch=batch, seq=seq, dil=dil, width=width,
                q_col=col0, k_col=col0 + hw, v_col=col0 + 2 * hw, hq=B_HEADS, grp=1,
                want_lse=True, out_dtype=BF16, name="attn_b%d" % gi)
            o_b.append(o)
            lse_b.append(lse)

        aw = jnp.zeros((LANES, C_HEADS * C_DK), BF16).at[:C_GATE_RANK].set(gla_alpha_w[l].astype(BF16))
        o_c = _gla(nat, g_all, aw, gla_alpha_b[l].reshape(1, -1), gla_head_norm[l].reshape(1, -1),
                   batch=batch, seq=seq)

        x2, xnt = _merge(o_a, o_b, lse_b, o_c, x2, norm_mix[l], w_gates, w_branch[l].astype(BF16),
                         w_out[l].astype(BF16), norm_ffn[l])

        wq_t = peer_wq[l].reshape(d, PEER_HEADS, 2, PEER_DKEY // 2).transpose(2, 1, 3, 0)
        wq_t = wq_t.reshape(2, PEER_HEADS * (PEER_DKEY // 2), d).astype(BF16)
        cnt, e1, rank, e2 = _peer_select(xnt, wq_t, _peer_key_matrix(peer_keys[l]))
        last = l == depth - 1
        x2 = _peer_dense(xnt, peer_u[l].astype(BF16), peer_v[l].T.astype(BF16), cnt, e1, rank, e2,
                         x2, norm_final, final_norm=last)
    return x2.reshape(batch, seq, d)
```

```python
import functools
import math

import numpy as np
import jax
import jax.numpy as jnp
from jax import lax
from jax.experimental import pallas as pl
from jax.experimental.pallas import tpu as pltpu

F32 = jnp.float32
BF16 = jnp.bfloat16

D_MODEL = 1024
HEAD_DIM = 64
BLOCK = 128
A_Q_HEADS = 8
A_KV_HEADS = 2
A_WINDOW = 128
B_PATTERNS = ((128, 1), (512, 4), (2048, 16))
B_HEADS = 8
C_HEADS = 4
C_DK = 64
C_DV = 128
C_GATE_RANK = 16
C_GATE_TAU = 16.0
C_CHUNK = 64
BRANCH_WIDTH = 512
N_BRANCHES = 3
N_BUCKETS = 32
BUCKET_MAX_DIST = 2048
PEER_HEADS = 8
PEER_KEYS = 128
PEER_N_EXPERTS = PEER_KEYS * PEER_KEYS
PEER_TOPK = 16
PEER_DKEY = 256
RMS_EPS = 1e-6
NEG_INF = -1e30

LANES = 128
VMEM_LIMIT = 56 * 1024 * 1024

GROUP_WIDTH = 3 * B_HEADS * HEAD_DIM
N_B0_COL = 0
N_CV_COL = GROUP_WIDTH
N_A_COL = N_CV_COL + C_HEADS * C_DV
N_CQ_COL = N_A_COL + (A_Q_HEADS + 2 * A_KV_HEADS) * HEAD_DIM
N_CK_COL = N_CQ_COL + C_HEADS * C_DK
N_WIDTH = N_CK_COL + C_HEADS * C_DK
G_GATES = N_BRANCHES * D_MODEL
G_WIDTH = C_HEADS * C_DV + LANES
G_R_COL = 0
G_ALR_COL = C_HEADS * C_DV
GLA_CHUNKS_PER_STEP = 8
ATTN_ROWS_PER_STEP = 2 * BLOCK
LSE_LANES = LANES // B_HEADS


def _cparams(sem):
    return pltpu.CompilerParams(dimension_semantics=sem, vmem_limit_bytes=VMEM_LIMIT)


def _norm_matmul_kernel(x_ref, g_ref, *refs):
    n_out = len(refs) // 2
    x = x_ref[...]
    ms = jnp.mean(x * x, axis=-1, keepdims=True)
    h = (x * lax.rsqrt(ms + RMS_EPS) * g_ref[...]).astype(BF16)
    for w_ref, o_ref in zip(refs[:n_out], refs[n_out:2 * n_out]):
        o_ref[...] = jnp.dot(h, w_ref[...], preferred_element_type=F32).astype(o_ref.dtype)
    refs[2 * n_out][...] = h


def _norm_matmul(x2, g, ws, out_dtypes, name):
    rows, d = x2.shape
    tm = min(512, rows)
    row_spec = lambda n: pl.BlockSpec((tm, n), lambda i: (i, 0))
    return pl.pallas_call(
        _norm_matmul_kernel,
        grid=(rows // tm,),
        in_specs=[row_spec(d), pl.BlockSpec((1, d), lambda i: (0, 0))]
        + [pl.BlockSpec(w.shape, lambda i: (0, 0)) for w in ws],
        out_specs=[row_spec(w.shape[1]) for w in ws] + [row_spec(d)],
        out_shape=[jax.ShapeDtypeStruct((rows, w.shape[1]), dt) for w, dt in zip(ws, out_dtypes)]
        + [jax.ShapeDtypeStruct((rows, d), BF16)],
        compiler_params=_cparams(("parallel",)),
        name=name,
    )(x2, g.reshape(1, d), *ws)


def _matmul_kernel(h_ref, w_ref, o_ref):
    o_ref[...] = jnp.dot(h_ref[...], w_ref[...], preferred_element_type=F32).astype(o_ref.dtype)


def _matmul(h, w, out_dtype, name):
    rows, d = h.shape
    tm = min(1024, rows)
    return pl.pallas_call(
        _matmul_kernel,
        grid=(rows // tm,),
        in_specs=[pl.BlockSpec((tm, d), lambda i: (i, 0)), pl.BlockSpec(w.shape, lambda i: (0, 0))],
        out_specs=pl.BlockSpec((tm, w.shape[1]), lambda i: (i, 0)),
        out_shape=jax.ShapeDtypeStruct((rows, w.shape[1]), out_dtype),
        compiler_params=_cparams(("parallel",)),
        name=name,
    )(h, w)


def _attn_kernel(*refs, hq, grp, has_sink, want_lse, nsb, nseq):
    q_ref, kp_ref, kc_ref, vp_ref, vc_ref, bias_ref = refs[:6]
    pos = 6
    sink_ref = None
    if has_sink:
        sink_ref = refs[pos]
        pos += 1
    o_ref = refs[pos]
    lse_ref = refs[pos + 1] if want_lse else None

    first = jnp.minimum(pl.program_id(2), 1)
    qw = hq * HEAD_DIM
    for sq in range(nseq):
        k = jnp.concatenate([kp_ref[sq], kc_ref[sq]], axis=0)
        v = jnp.concatenate([vp_ref[sq], vc_ref[sq]], axis=0)
        for sb in range(nsb):
            q = q_ref[sq, sb * BLOCK:(sb + 1) * BLOCK, :] * (HEAD_DIM ** -0.5)
            ks = k[sb * BLOCK:(sb + 2) * BLOCK]
            vs = v[sb * BLOCK:(sb + 2) * BLOCK]
            outs, lses = [], []
            for h in range(hq):
                kvh = h // grp
                qh = q[:, h * HEAD_DIM:(h + 1) * HEAD_DIM]
                kh = ks[:, kvh * HEAD_DIM:(kvh + 1) * HEAD_DIM]
                vh = vs[:, kvh * HEAD_DIM:(kvh + 1) * HEAD_DIM]
                s = lax.dot_general(qh, kh, (((1,), (1,)), ((), ())), preferred_element_type=F32)
                bias = bias_ref[first, h] if sb == 0 else bias_ref[1, h]
                s = s + bias
                m = jnp.max(s, axis=-1, keepdims=True)
                if has_sink:
                    m = jnp.maximum(m, sink_ref[h])
                p = jnp.exp(s - m)
                denom = jnp.sum(p, axis=-1, keepdims=True)
                if has_sink:
                    denom = denom + jnp.exp(sink_ref[h] - m)
                o = jnp.dot(p.astype(BF16), vh, preferred_element_type=F32) / denom
                outs.append(o)
                if want_lse:
                    lses.append(jnp.broadcast_to(m + jnp.log(denom), (BLOCK, LSE_LANES)))
            rs = slice(sb * BLOCK, (sb + 1) * BLOCK)
            o_ref[0, rs, sq * qw:(sq + 1) * qw] = jnp.concatenate(outs, axis=-1).astype(o_ref.dtype)
            if want_lse:
                lse_ref[0, rs, sq * LANES:(sq + 1) * LANES] = jnp.concatenate(lses, axis=-1)


def _banded_bias(bias, max_dist):
    qi = np.arange(BLOCK)[:, None]
    kj = np.arange(2 * BLOCK)[None, :]
    dist = qi + BLOCK - kj
    in_window = (dist >= 0) & (dist <= max_dist)
    first = in_window & (kj >= BLOCK)
    return jnp.stack([jnp.where(first, bias, NEG_INF), jnp.where(in_window, bias, NEG_INF)])


def _banded_attention(qkv, bias, sink, *, batch, seq, dil, width, q_col, k_col, v_col,
                      hq, grp, want_lse, out_dtype, name):
    sub = seq // dil
    rows = min(ATTN_ROWS_PER_STEP, sub)
    nsb = rows // BLOCK
    nseq = min(ATTN_ROWS_PER_STEP // rows, dil)
    qw = hq * HEAD_DIM
    kw = (hq // grp) * HEAD_DIM
    view = qkv.reshape(batch * dil, sub, width)
    q_blk, k_blk, v_blk = q_col // qw, k_col // kw, v_col // kw
    assert q_col % qw == 0 and k_col % kw == 0 and v_col % kw == 0
    groups = dil // nseq

    prev_spec = lambda c: pl.BlockSpec((nseq, BLOCK, kw),
                                       lambda b, r, i: (b * groups + r, jnp.maximum(i * nsb - 1, 0), c))
    cur_spec = lambda w, c: pl.BlockSpec((nseq, rows, w), lambda b, r, i: (b * groups + r, i, c))
    in_specs = [
        cur_spec(qw, q_blk), prev_spec(k_blk), cur_spec(kw, k_blk), prev_spec(v_blk), cur_spec(kw, v_blk),
        pl.BlockSpec((2, hq, BLOCK, 2 * BLOCK), lambda b, r, i: (0, 0, 0, 0)),
    ]
    args = [view, view, view, view, view, bias]
    if sink is not None:
        in_specs.append(pl.BlockSpec(memory_space=pltpu.SMEM))
        args.append(sink)
    o_spec = pl.BlockSpec((1, rows, nseq * qw), lambda b, r, i: (b, i, r))
    o_shape = jax.ShapeDtypeStruct((batch, sub, dil * qw), out_dtype)
    out_specs, out_shape = o_spec, o_shape
    if want_lse:
        out_specs = [o_spec, pl.BlockSpec((1, rows, nseq * LANES), lambda b, r, i: (b, i, r))]
        out_shape = [o_shape, jax.ShapeDtypeStruct((batch, sub, dil * LANES), F32)]
    res = pl.pallas_call(
        functools.partial(_attn_kernel, hq=hq, grp=grp, has_sink=sink is not None, want_lse=want_lse, nsb=nsb,
                          nseq=nseq),
        grid=(batch, groups, sub // rows),
        in_specs=in_specs,
        out_specs=out_specs,
        out_shape=out_shape,
        compiler_params=_cparams(("parallel", "parallel", "arbitrary")),
        name=name,
    )(*args)
    if want_lse:
        return res[0].reshape(batch * seq, qw), res[1].reshape(batch * seq, LANES)
    return res.reshape(batch * seq, qw)


def _gla_kernel(q_ref, k_ref, v_ref, alr_ref, r_ref, aw_ref, ab_ref, hn_ref, o_ref, st_ref):
    @pl.when(pl.program_id(1) == 0)
    def _():
        st_ref[...] = jnp.zeros_like(st_ref)

    ck = C_CHUNK
    z = jnp.dot(alr_ref[0].astype(BF16), aw_ref[...], preferred_element_type=F32) + ab_ref[...]
    log_a = (jnp.minimum(z, 0.0) - jnp.log1p(jnp.exp(-jnp.abs(z)))) * (1.0 / C_GATE_TAU)
    row = lax.broadcasted_iota(jnp.int32, log_a.shape, 0) & (ck - 1)
    cum = log_a
    shift = 1
    while shift < ck:
        cum = cum + jnp.where(row >= shift, pltpu.roll(cum, shift, 0), 0.0)
        shift *= 2
    q_all = q_ref[0].astype(F32) * (C_DK ** -0.5)
    k_all = k_ref[0].astype(F32)
    q_dec_all = (q_all * jnp.exp(cum)).astype(BF16)
    k_inv_all = (k_all * jnp.exp(-cum)).astype(BF16)
    ti = lax.broadcasted_iota(jnp.int32, (ck, ck), 0)
    si = lax.broadcasted_iota(jnp.int32, (ck, ck), 1)
    causal = ti >= si
    nt = (((1,), (1,)), ((), ()))
    tn = (((0,), (0,)), ((), ()))
    for c in range(GLA_CHUNKS_PER_STEP):
        rs = slice(c * ck, (c + 1) * ck)
        cum_c = cum[rs]
        last = cum_c[ck - 1:ck, :]
        q_dec, k_inv = q_dec_all[rs], k_inv_all[rs]
        k_end = (k_all[rs] * jnp.exp(last - cum_c)).astype(BF16)
        decay = jnp.exp(last)
        v = v_ref[0, rs, :]
        r = r_ref[0, rs, :]
        outs = []
        for h in range(C_HEADS):
            ks = slice(h * C_DK, (h + 1) * C_DK)
            vs = slice(h * C_DV, (h + 1) * C_DV)
            qd, ki, ke, vh = q_dec[:, ks], k_inv[:, ks], k_end[:, ks], v[:, vs]
            att = jnp.where(causal, lax.dot_general(qd, ki, nt, preferred_element_type=F32), 0.0)
            st = st_ref[h]
            o = jnp.dot(att.astype(BF16), vh, preferred_element_type=F32)
            o = o + lax.dot_general(qd, st.astype(BF16), nt, preferred_element_type=F32)
            kv_t = lax.dot_general(vh, ke, tn, preferred_element_type=F32)
            st_ref[h] = st * decay[:, ks] + kv_t
            o = o * lax.rsqrt(jnp.mean(o * o, axis=-1, keepdims=True) + RMS_EPS) * hn_ref[...]
            rh = r[:, vs]
            outs.append(o * (rh * jax.nn.sigmoid(rh)))
        o_ref[0, rs, :] = jnp.concatenate(outs, axis=-1).astype(o_ref.dtype)


def _gla(c_qkv, g_all, aw, ab, hn, *, batch, seq):
    rows = C_CHUNK * GLA_CHUNKS_PER_STEP
    qk_w = C_HEADS * C_DK
    v_w = C_HEADS * C_DV
    cv = c_qkv.reshape(batch, seq, N_WIDTH)
    gv = g_all.reshape(batch, seq, G_WIDTH)
    return pl.pallas_call(
        _gla_kernel,
        grid=(batch, seq // rows),
        in_specs=[
            pl.BlockSpec((1, rows, qk_w), lambda b, c: (b, c, N_CQ_COL // qk_w)),
            pl.BlockSpec((1, rows, qk_w), lambda b, c: (b, c, N_CK_COL // qk_w)),
            pl.BlockSpec((1, rows, v_w), lambda b, c: (b, c, N_CV_COL // v_w)),
            pl.BlockSpec((1, rows, LANES), lambda b, c: (b, c, G_ALR_COL // LANES)),
            pl.BlockSpec((1, rows, v_w), lambda b, c: (b, c, G_R_COL // v_w)),
            pl.BlockSpec((LANES, qk_w), lambda b, c: (0, 0)),
            pl.BlockSpec((1, qk_w), lambda b, c: (0, 0)),
            pl.BlockSpec((1, C_DV), lambda b, c: (0, 0)),
        ],
        out_specs=pl.BlockSpec((1, rows, v_w), lambda b, c: (b, c, 0)),
        out_shape=jax.ShapeDtypeStruct((batch, seq, v_w), BF16),
        scratch_shapes=[pltpu.VMEM((C_HEADS, C_DV, C_DK), F32)],
        compiler_params=_cparams(("parallel", "arbitrary")),
        name="gla",
    )(cv, cv, cv, gv, gv, aw, ab, hn).reshape(batch * seq, v_w)


def _merge_kernel(oa_ref, ob1_ref, ob2_ref, ob3_ref, l1_ref, l2_ref, l3_ref, oc_ref, x_ref,
                  gm_ref, ex_ref, wg_ref, wb_ref, wo_ref, gn_ref, xo_ref, xnt_ref):
    l1, l2, l3 = l1_ref[...], l2_ref[...], l3_ref[...]
    lm = jnp.maximum(jnp.maximum(l1, l2), l3)
    e1, e2, e3 = jnp.exp(l1 - lm), jnp.exp(l2 - lm), jnp.exp(l3 - lm)
    inv = 1.0 / (e1 + e2 + e3)

    def widen(wc):
        hi = wc.astype(BF16)
        r1 = wc - hi.astype(F32)
        mid = r1.astype(BF16)
        lo = (r1 - mid.astype(F32)).astype(BF16)
        ex = ex_ref[...]
        return (jnp.dot(hi, ex, preferred_element_type=F32) + jnp.dot(mid, ex, preferred_element_type=F32)
                + jnp.dot(lo, ex, preferred_element_type=F32))

    ob = (widen(e1 * inv) * ob1_ref[...].astype(F32) + widen(e2 * inv) * ob2_ref[...].astype(F32)
          + widen(e3 * inv) * ob3_ref[...].astype(F32))
    branches = (oa_ref[...], ob.astype(BF16), oc_ref[...])
    x = x_ref[...]
    h = (x * lax.rsqrt(jnp.mean(x * x, axis=-1, keepdims=True) + RMS_EPS) * gm_ref[...]).astype(BF16)
    merged = None
    for n in range(N_BRANCHES):
        gate = jnp.dot(h, wg_ref[:, n * D_MODEL:(n + 1) * D_MODEL], preferred_element_type=F32)
        proj = jnp.dot(branches[n], wb_ref[n], preferred_element_type=F32)
        term = jax.nn.sigmoid(gate) * proj
        merged = term if merged is None else merged + term
    x = x + jnp.dot(merged.astype(BF16), wo_ref[...], preferred_element_type=F32)
    xo_ref[...] = x
    xn = x * lax.rsqrt(jnp.mean(x * x, axis=-1, keepdims=True) + RMS_EPS) * gn_ref[...]
    xnt_ref[...] = xn.T.astype(BF16)


def _head_expander():
    src = np.arange(LANES)[:, None]
    dst = np.arange(BRANCH_WIDTH)[None, :]
    return jnp.asarray(src == (dst // HEAD_DIM) * LSE_LANES, dtype=BF16)


def _merge(o_a, o_b, lse_b, o_c, x2, gm, wg, wb, wo, gn):
    rows = x2.shape[0]
    tm = min(512, rows)
    row_spec = lambda w: pl.BlockSpec((tm, w), lambda i: (i, 0))
    vec_spec = pl.BlockSpec((1, D_MODEL), lambda i: (0, 0))
    return pl.pallas_call(
        _merge_kernel,
        grid=(rows // tm,),
        in_specs=[row_spec(BRANCH_WIDTH)] * 4 + [row_spec(LANES)] * 3 + [
            row_spec(BRANCH_WIDTH),
            row_spec(D_MODEL),
            vec_spec,
            pl.BlockSpec((LANES, BRANCH_WIDTH), lambda i: (0, 0)),
            pl.BlockSpec((D_MODEL, G_GATES), lambda i: (0, 0)),
            pl.BlockSpec((N_BRANCHES, BRANCH_WIDTH, D_MODEL), lambda i: (0, 0, 0)),
            pl.BlockSpec((D_MODEL, D_MODEL), lambda i: (0, 0)),
            vec_spec,
        ],
        out_specs=[row_spec(D_MODEL), pl.BlockSpec((D_MODEL, tm), lambda i: (0, i))],
        out_shape=[jax.ShapeDtypeStruct((rows, D_MODEL), F32),
                   jax.ShapeDtypeStruct((D_MODEL, rows), BF16)],
        compiler_params=_cparams(("parallel",)),
        name="merge",
    )(o_a, o_b[0], o_b[1], o_b[2], lse_b[0], lse_b[1], lse_b[2], o_c, x2, gm.reshape(1, D_MODEL),
      _head_expander(), wg, wb, wo, gn.reshape(1, D_MODEL))


_PAIRS = tuple((i, j) for i in range(PEER_TOPK) for j in range(PEER_TOPK)
               if (i + 1) * (j + 1) <= PEER_TOPK)


def _sort16_desc(v):
    v = list(v)
    n = len(v)
    k = 2
    while k <= n:
        j = k // 2
        while j >= 1:
            for i in range(n):
                l = i ^ j
                if l > i:
                    hi, lo = jnp.maximum(v[i], v[l]), jnp.minimum(v[i], v[l])
                    v[i], v[l] = (hi, lo) if (i & k) == 0 else (lo, hi)
            j //= 2
        k *= 2
    return v


def _merge_top16(a, b):
    n = len(a)
    v = [jnp.maximum(a[i], b[n - 1 - i]) for i in range(n)]
    j = n // 2
    while j >= 1:
        for i in range(n):
            l = i ^ j
            if l > i:
                v[i], v[l] = jnp.maximum(v[i], v[l]), jnp.minimum(v[i], v[l])
        j //= 2
    return v


def _top16_of(vals):
    acc = None
    for g in range(len(vals) // PEER_TOPK):
        grp = _sort16_desc(vals[g * PEER_TOPK:(g + 1) * PEER_TOPK])
        acc = grp if acc is None else _merge_top16(acc, grp)
    return acc


def _peer_select_kernel(xnt_ref, wq_ref, kb_ref, cnt_ref, e1_ref, rank_ref, e2_ref,
                        sc_ref, rk_ref, eb_ref, *, tb):
    nk, nh, kk = PEER_KEYS, PEER_HEADS, PEER_TOPK
    xnt = xnt_ref[...]
    for p in range(2):
        qy = jnp.dot(wq_ref[p], xnt, preferred_element_type=F32).astype(BF16)
        sc_ref[p] = jnp.dot(kb_ref[p], qy, preferred_element_type=F32).reshape(nk, nh, tb)

    for c in range(tb // LANES):
        cs = slice(c * LANES, (c + 1) * LANES)
        s1s = _top16_of([sc_ref[0, n, :, cs] for n in range(nk)])
        s2s = _top16_of([sc_ref[1, n, :, cs] for n in range(nk)])
        cands = [s1s[i] + s2s[j] for (i, j) in _PAIRS]
        pad = [jnp.full_like(cands[0], -jnp.inf)] * (-len(cands) % kk)
        tau = _top16_of(cands + pad)[kk - 1]
        top = cands[0]
        zsum = None
        counts = [None] * kk
        for (i, j), cd in zip(_PAIRS, cands):
            sel = cd >= tau
            term = jnp.where(sel, jnp.exp(cd - top), 0.0)
            zsum = term if zsum is None else zsum + term
            one = jnp.where(sel, 1.0, 0.0)
            counts[i] = one if counts[i] is None else counts[i] + one
        inv_z = 1.0 / zsum

        def first_half(n, carry):
            s1 = sc_ref[0, n, :, cs]
            cnt = jnp.zeros_like(s1)
            for i in range(kk):
                cnt = jnp.where(s1 == s1s[i], counts[i], cnt)
            r0 = pl.multiple_of(n * nh, nh)
            cnt_ref[pl.ds(r0, nh), cs] = cnt
            e1_ref[pl.ds(r0, nh), cs] = jnp.exp(s1 - s1s[0])
            return carry

        def second_half(n, carry):
            s2 = sc_ref[1, n, :, cs]
            rank = jnp.full_like(s2, float(kk))
            for j in range(kk):
                rank = jnp.where(s2 == s2s[j], float(j), rank)
            r0 = pl.multiple_of(n * nh, nh)
            rk_ref[pl.ds(r0, nh), :] = rank
            eb_ref[pl.ds(r0, nh), :] = jnp.exp(s2 - s2s[0]) * inv_z
            return carry

        lax.fori_loop(0, nk, first_half, 0, unroll=4)
        lax.fori_loop(0, nk, second_half, 0, unroll=4)
        for h in range(nh):
            rank_ref[h * nk:(h + 1) * nk, cs] = rk_ref[pl.ds(h, nk, stride=nh), :].astype(BF16)
            e2_ref[h * nk:(h + 1) * nk, cs] = eb_ref[pl.ds(h, nk, stride=nh), :].astype(BF16)


def _peer_select(xnt, wq_t, k_big):
    d, rows = xnt.shape
    tb = min(256, rows)
    nrow = PEER_KEYS * PEER_HEADS
    tab_spec = pl.BlockSpec((nrow, tb), lambda i: (0, i))
    return pl.pallas_call(
        functools.partial(_peer_select_kernel, tb=tb),
        grid=(rows // tb,),
        in_specs=[
            pl.BlockSpec((d, tb), lambda i: (0, i)),
            pl.BlockSpec((2, nrow, d), lambda i: (0, 0, 0)),
            pl.BlockSpec((2, nrow, nrow), lambda i: (0, 0, 0)),
        ],
        out_specs=[tab_spec] * 4,
        out_shape=[jax.ShapeDtypeStruct((nrow, rows), dt) for dt in (F32, F32, BF16, BF16)],
        scratch_shapes=[
            pltpu.VMEM((2, PEER_KEYS, PEER_HEADS, tb), F32),
            pltpu.VMEM((nrow, LANES), F32),
            pltpu.VMEM((nrow, LANES), F32),
        ],
        compiler_params=_cparams(("parallel",)),
        name="peer_select",
    )(xnt, wq_t, k_big)


def _gelu_tanh(x):
    c = math.sqrt(2.0 / math.pi)
    return 0.5 * x * (1.0 + jnp.tanh(c * (x + 0.044715 * (x * x * x))))


def _peer_dense_kernel(xnt_ref, u_ref, vt_ref, cnt_ref, e1_ref, rank_ref, e2_ref, x_ref, gn_ref,
                       o_ref, acc_ref, ht_ref, w_ref, *, tb, eb, slab, final_norm):
    j = pl.program_id(1)
    nk, nh = PEER_KEYS, PEER_HEADS
    pk = 16
    n_slab = eb // slab

    @pl.when(j == 0)
    def _():
        acc_ref[...] = jnp.zeros_like(acc_ref)

    def hidden(s):
        rows = slice(s * slab, (s + 1) * slab)
        ht_ref[rows, :] = jnp.dot(u_ref[rows, :], xnt_ref[...], preferred_element_type=F32)

    def gate(s):
        for al in range(s * slab // nk, (s + 1) * slab // nk):
            rows = slice(al * nk, (al + 1) * nk)
            for c in range(tb // LANES):
                cs = slice(c * LANES, (c + 1) * LANES)
                cnt8 = cnt_ref[al * nh:(al + 1) * nh, cs]
                e18 = e1_ref[al * nh:(al + 1) * nh, cs]
                act = _gelu_tanh(ht_ref[rows, cs].astype(BF16)).reshape(nk // pk, pk, LANES)
                g = jnp.zeros((nk // pk, pk, LANES), BF16)
                for h in range(nh):
                    cb = jnp.broadcast_to(cnt8[h:h + 1, :], (pk, LANES)).astype(BF16)
                    eb_ = jnp.broadcast_to(e18[h:h + 1, :], (pk, LANES)).astype(BF16)
                    rk = rank_ref[h * nk:(h + 1) * nk, cs].reshape(nk // pk, pk, LANES)
                    e2 = e2_ref[h * nk:(h + 1) * nk, cs].reshape(nk // pk, pk, LANES)
                    g = g + jnp.where(rk < cb[None], e2, jnp.zeros_like(e2)) * eb_[None]
                w_ref[rows, cs] = (act * g).reshape(nk, LANES)

    def project(s):
        rows = slice(s * slab, (s + 1) * slab)
        acc_ref[...] += jnp.dot(vt_ref[:, rows], w_ref[rows, :], preferred_element_type=F32)

    hidden(0)
    for s in range(n_slab):
        if s + 1 < n_slab:
            hidden(s + 1)
        gate(s)
        if s >= 1:
            project(s - 1)
    project(n_slab - 1)

    @pl.when(j == pl.num_programs(1) - 1)
    def _():
        x = x_ref[...] + acc_ref[...].T
        if final_norm:
            x = x * lax.rsqrt(jnp.mean(x * x, axis=-1, keepdims=True) + RMS_EPS) * gn_ref[...]
        o_ref[...] = x


def _peer_dense(xnt, u, vt, cnt, e1, rank, e2, x2, gn, final_norm):
    d, rows = xnt.shape
    ne = u.shape[0]
    tb = min(512, rows)
    eb = 2048
    slab = 1024
    nrow = PEER_KEYS * PEER_HEADS
    arow = (eb // PEER_KEYS) * PEER_HEADS
    return pl.pallas_call(
        functools.partial(_peer_dense_kernel, tb=tb, eb=eb, slab=slab, final_norm=final_norm),
        grid=(rows // tb, ne // eb),
        in_specs=[
            pl.BlockSpec((d, tb), lambda i, j: (0, i)),
            pl.BlockSpec((eb, d), lambda i, j: (j, 0)),
            pl.BlockSpec((d, eb), lambda i, j: (0, j)),
            pl.BlockSpec((arow, tb), lambda i, j: (j, i)),
            pl.BlockSpec((arow, tb), lambda i, j: (j, i)),
            pl.BlockSpec((nrow, tb), lambda i, j: (0, i)),
            pl.BlockSpec((nrow, tb), lambda i, j: (0, i)),
            pl.BlockSpec((tb, d), lambda i, j: (i, 0)),
            pl.BlockSpec((1, d), lambda i, j: (0, 0)),
        ],
        out_specs=pl.BlockSpec((tb, d), lambda i, j: (i, 0)),
        out_shape=jax.ShapeDtypeStruct((rows, d), F32),
        scratch_shapes=[
            pltpu.VMEM((d, tb), F32),
            pltpu.VMEM((eb, tb), F32),
            pltpu.VMEM((eb, tb), BF16),
        ],
        compiler_params=_cparams(("parallel", "arbitrary")),
        name="peer_dense",
    )(xnt, u, vt, cnt, e1, rank, e2, x2, gn.reshape(1, d))


def _t5_bucket(dist):
    max_exact = N_BUCKETS // 2
    large = max_exact + (jnp.log(jnp.maximum(dist, 1).astype(F32) / max_exact)
                         / math.log(BUCKET_MAX_DIST / max_exact) * (N_BUCKETS - max_exact)).astype(jnp.int32)
    large = jnp.minimum(large, N_BUCKETS - 1)
    return jnp.where(dist < max_exact, dist, large)


def _rel_bias(table_cols, dilation):
    qi = jnp.arange(BLOCK)[:, None]
    kj = jnp.arange(2 * BLOCK)[None, :]
    dist = jnp.maximum(qi + BLOCK - kj, 0) * dilation
    onehot = (_t5_bucket(dist)[..., None] == jnp.arange(N_BUCKETS)).astype(F32)
    return jnp.einsum('qkb,bh->hqk', onehot, table_cols.astype(F32), precision=lax.Precision.HIGHEST)


def _split_w_in(w):
    hd = HEAD_DIM
    sizes = [A_Q_HEADS * hd, A_KV_HEADS * hd, A_KV_HEADS * hd] + [B_HEADS * hd] * 9 + [
        C_HEADS * C_DK, C_HEADS * C_DK, C_HEADS * C_DV, C_GATE_RANK, C_HEADS * C_DV, N_BRANCHES * D_MODEL]
    offs = np.concatenate([[0], np.cumsum(sizes)])
    col = lambda a, b: w[:, int(offs[a]):int(offs[b])].astype(BF16)
    groups = [col(3 + 3 * g, 6 + 3 * g) for g in range(3)]
    w_nat = jnp.concatenate([groups[0], col(14, 15), col(0, 3), col(12, 14)], axis=1)
    pad = jnp.zeros((w.shape[0], G_WIDTH - G_ALR_COL - C_GATE_RANK), BF16)
    w_r = jnp.concatenate([col(16, 17), col(15, 16), pad], axis=1)
    return w_nat, w_r, groups, col(17, 18)


def _peer_key_matrix(keys):
    nh, _, nk, c = keys.shape
    eye = jnp.eye(nh, dtype=keys.dtype)
    big = jnp.einsum('hpnc,hg->pnhgc', keys, eye)
    return big.reshape(2, nk * nh, nh * c).astype(BF16)


def _residue_major(x2, batch, seq, dil):
    if dil == 1:
        return x2
    d = x2.shape[-1]
    return x2.reshape(batch, seq // dil, dil, d).transpose(0, 2, 1, 3).reshape(batch * seq, d)


def kernel(x, w_in, attn_sinks, gla_alpha_w, gla_alpha_b, gla_head_norm, w_branch, w_out, norm_mix,
           norm_ffn, peer_wq, peer_keys, peer_u, peer_v, rel_bias_table, norm_final):
    batch, seq, d = x.shape
    rows = batch * seq
    depth = w_in.shape[0]
    hw = B_HEADS * HEAD_DIM

    bias_a = _banded_bias(_rel_bias(rel_bias_table[:, :A_Q_HEADS], 1), A_WINDOW - 1)
    bias_b = [_banded_bias(
        _rel_bias(rel_bias_table[:, A_Q_HEADS + i * B_HEADS:A_Q_HEADS + (i + 1) * B_HEADS], dil), window // dil)
        for i, (window, dil) in enumerate(B_PATTERNS)]

    x2 = x.reshape(rows, d)
    for l in range(depth):
        w_nat, w_r, w_groups, w_gates = _split_w_in(w_in[l])
        nat, g_all, h_mix = _norm_matmul(x2, norm_mix[l], [w_nat, w_r], [BF16, F32], "proj_nat")

        o_a = _banded_attention(
            nat, bias_a, attn_sinks[l], batch=batch, seq=seq, dil=1, width=N_WIDTH,
            q_col=N_A_COL, k_col=N_A_COL + A_Q_HEADS * HEAD_DIM,
            v_col=N_A_COL + (A_Q_HEADS + A_KV_HEADS) * HEAD_DIM,
            hq=A_Q_HEADS, grp=A_Q_HEADS // A_KV_HEADS, want_lse=False, out_dtype=BF16, name="attn_a")
        o_b, lse_b = [], []
        for gi, (window, dil) in enumerate(B_PATTERNS):
            if dil == 1:
                b_qkv, width, col0 = nat, N_WIDTH, N_B0_COL
            else:
                b_qkv = _matmul(_residue_major(h_mix, batch, seq, dil), w_groups[gi], BF16, "proj_b%d" % gi)
                width, col0 = GROUP_WIDTH, 0
            o, lse = _banded_attention(
                b_qkv, bias_b[gi], None, batch=batch, seq=seq, dil=dil, width=width,
                q_col=col0, k_col=col0 + hw, v_col=col0 + 2 * hw, hq=B_HEADS, grp=1,
                want_lse=True, out_dtype=BF16, name="attn_b%d" % gi)
            o_b.append(o)
            lse_b.append(lse)

        aw = jnp.zeros((LANES, C_HEADS * C_DK), BF16).at[:C_GATE_RANK].set(gla_alpha_w[l].astype(BF16))
        o_c = _gla(nat, g_all, aw, gla_alpha_b[l].reshape(1, -1), gla_head_norm[l].reshape(1, -1),
                   batch=batch, seq=seq)

        x2, xnt = _merge(o_a, o_b, lse_b, o_c, x2, norm_mix[l], w_gates, w_branch[l].astype(BF16),
                         w_out[l].astype(BF16), norm_ffn[l])

        wq_t = peer_wq[l].reshape(d, PEER_HEADS, 2, PEER_DKEY // 2).transpose(2, 1, 3, 0)
        wq_t = wq_t.reshape(2, PEER_HEADS * (PEER_DKEY // 2), d).astype(BF16)
        cnt, e1, rank, e2 = _peer_select(xnt, wq_t, _peer_key_matrix(peer_keys[l]))
        last = l == depth - 1
        x2 = _peer_dense(xnt, peer_u[l].astype(BF16), peer_v[l].T.astype(BF16), cnt, e1, rank, e2,
                         x2, norm_final, final_norm=last)
    return x2.reshape(batch, seq, d)
```

```python
import functools
import math

import numpy as np
import jax
import jax.numpy as jnp
from jax import lax
from jax.experimental import pallas as pl
from jax.experimental.pallas import tpu as pltpu

F32 = jnp.float32
BF16 = jnp.bfloat16

D_MODEL = 1024
HEAD_DIM = 64
BLOCK = 128
A_Q_HEADS = 8
A_KV_HEADS = 2
A_WINDOW = 128
B_PATTERNS = ((128, 1), (512, 4), (2048, 16))
B_HEADS = 8
C_HEADS = 4
C_DK = 64
C_DV = 128
C_GATE_RANK = 16
C_GATE_TAU = 16.0
C_CHUNK = 64
BRANCH_WIDTH = 512
N_BRANCHES = 3
N_BUCKETS = 32
BUCKET_MAX_DIST = 2048
PEER_HEADS = 8
PEER_KEYS = 128
PEER_N_EXPERTS = PEER_KEYS * PEER_KEYS
PEER_TOPK = 16
PEER_DKEY = 256
RMS_EPS = 1e-6
NEG_INF = -1e30

LANES = 128
VMEM_LIMIT = 56 * 1024 * 1024

GROUP_WIDTH = 3 * B_HEADS * HEAD_DIM
N_B0_COL = 0
N_CV_COL = GROUP_WIDTH
N_A_COL = N_CV_COL + C_HEADS * C_DV
N_CQ_COL = N_A_COL + (A_Q_HEADS + 2 * A_KV_HEADS) * HEAD_DIM
N_CK_COL = N_CQ_COL + C_HEADS * C_DK
N_WIDTH = N_CK_COL + C_HEADS * C_DK
G_GATES = N_BRANCHES * D_MODEL
G_WIDTH = C_HEADS * C_DV + LANES
G_R_COL = 0
G_ALR_COL = C_HEADS * C_DV
GLA_CHUNKS_PER_STEP = 8
ATTN_ROWS_PER_STEP = 2 * BLOCK
LSE_LANES = LANES // B_HEADS


def _cparams(sem):
    return pltpu.CompilerParams(dimension_semantics=sem, vmem_limit_bytes=VMEM_LIMIT)


def _norm_matmul_kernel(x_ref, g_ref, *refs):
    n_out = len(refs) // 2
    x = x_ref[...]
    ms = jnp.mean(x * x, axis=-1, keepdims=True)
    h = (x * lax.rsqrt(ms + RMS_EPS) * g_ref[...]).astype(BF16)
    for w_ref, o_ref in zip(refs[:n_out], refs[n_out:2 * n_out]):
        o_ref[...] = jnp.dot(h, w_ref[...], preferred_element_type=F32).astype(o_ref.dtype)
    refs[2 * n_out][...] = h


def _norm_matmul(x2, g, ws, out_dtypes, name):
    rows, d = x2.shape
    tm = min(512, rows)
    row_spec = lambda n: pl.BlockSpec((tm, n), lambda i: (i, 0))
    return pl.pallas_call(
        _norm_matmul_kernel,
        grid=(rows // tm,),
        in_specs=[row_spec(d), pl.BlockSpec((1, d), lambda i: (0, 0))]
        + [pl.BlockSpec(w.shape, lambda i: (0, 0)) for w in ws],
        out_specs=[row_spec(w.shape[1]) for w in ws] + [row_spec(d)],
        out_shape=[jax.ShapeDtypeStruct((rows, w.shape[1]), dt) for w, dt in zip(ws, out_dtypes)]
        + [jax.ShapeDtypeStruct((rows, d), BF16)],
        compiler_params=_cparams(("parallel",)),
        name=name,
    )(x2, g.reshape(1, d), *ws)


def _matmul_kernel(h_ref, w_ref, o_ref):
    o_ref[...] = jnp.dot(h_ref[...], w_ref[...], preferred_element_type=F32).astype(o_ref.dtype)


def _matmul(h, w, out_dtype, name):
    rows, d = h.shape
    tm = min(1024, rows)
    return pl.pallas_call(
        _matmul_kernel,
        grid=(rows // tm,),
        in_specs=[pl.BlockSpec((tm, d), lambda i: (i, 0)), pl.BlockSpec(w.shape, lambda i: (0, 0))],
        out_specs=pl.BlockSpec((tm, w.shape[1]), lambda i: (i, 0)),
        out_shape=jax.ShapeDtypeStruct((rows, w.shape[1]), out_dtype),
        compiler_params=_cparams(("parallel",)),
        name=name,
    )(h, w)


def _attn_kernel(*refs, hq, grp, has_sink, want_lse, nsb, nseq):
    q_ref, kp_ref, kc_ref, vp_ref, vc_ref, bias_ref = refs[:6]
    pos = 6
    sink_ref = None
    if has_sink:
        sink_ref = refs[pos]
        pos += 1
    o_ref = refs[pos]
    lse_ref = refs[pos + 1] if want_lse else None

    first = jnp.minimum(pl.program_id(2), 1)
    qw = hq * HEAD_DIM
    for sq in range(nseq):
        k = jnp.concatenate([kp_ref[sq], kc_ref[sq]], axis=0)
        v = jnp.concatenate([vp_ref[sq], vc_ref[sq]], axis=0)
        for sb in range(nsb):
            q = q_ref[sq, sb * BLOCK:(sb + 1) * BLOCK, :] * (HEAD_DIM ** -0.5)
            ks = k[sb * BLOCK:(sb + 2) * BLOCK]
            vs = v[sb * BLOCK:(sb + 2) * BLOCK]
            outs, lses = [], []
            for h in range(hq):
                kvh = h // grp
                qh = q[:, h * HEAD_DIM:(h + 1) * HEAD_DIM]
                kh = ks[:, kvh * HEAD_DIM:(kvh + 1) * HEAD_DIM]
                vh = vs[:, kvh * HEAD_DIM:(kvh + 1) * HEAD_DIM]
                s = lax.dot_general(qh, kh, (((1,), (1,)), ((), ())), preferred_element_type=F32)
                bias = bias_ref[first, h] if sb == 0 else bias_ref[1, h]
                s = s + bias
                m = jnp.max(s, axis=-1, keepdims=True)
                if has_sink:
                    m = jnp.maximum(m, sink_ref[h])
                p = jnp.exp(s - m)
                denom = jnp.sum(p, axis=-1, keepdims=True)
                if has_sink:
                    denom = denom + jnp.exp(sink_ref[h] - m)
                o = jnp.dot(p.astype(BF16), vh, preferred_element_type=F32) / denom
                outs.append(o)
                if want_lse:
                    lses.append(jnp.broadcast_to(m + jnp.log(denom), (BLOCK, LSE_LANES)))
            rs = slice(sb * BLOCK, (sb + 1) * BLOCK)
            o_ref[0, rs, sq * qw:(sq + 1) * qw] = jnp.concatenate(outs, axis=-1).astype(o_ref.dtype)
            if want_lse:
                lse_ref[0, rs, sq * LANES:(sq + 1) * LANES] = jnp.concatenate(lses, axis=-1)


def _banded_bias(bias, max_dist):
    qi = np.arange(BLOCK)[:, None]
    kj = np.arange(2 * BLOCK)[None, :]
    dist = qi + BLOCK - kj
    in_window = (dist >= 0) & (dist <= max_dist)
    first = in_window & (kj >= BLOCK)
    return jnp.stack([jnp.where(first, bias, NEG_INF), jnp.where(in_window, bias, NEG_INF)])


def _banded_attention(qkv, bias, sink, *, batch, seq, dil, width, q_col, k_col, v_col,
                      hq, grp, want_lse, out_dtype, name):
    sub = seq // dil
    rows = min(ATTN_ROWS_PER_STEP, sub)
    nsb = rows // BLOCK
    nseq = min(ATTN_ROWS_PER_STEP // rows, dil)
    qw = hq * HEAD_DIM
    kw = (hq // grp) * HEAD_DIM
    view = qkv.reshape(batch * dil, sub, width)
    q_blk, k_blk, v_blk = q_col // qw, k_col // kw, v_col // kw
    assert q_col % qw == 0 and k_col % kw == 0 and v_col % kw == 0
    groups = dil // nseq

    prev_spec = lambda c: pl.BlockSpec((nseq, BLOCK, kw),
                                       lambda b, r, i: (b * groups + r, jnp.maximum(i * nsb - 1, 0), c))
    cur_spec = lambda w, c: pl.BlockSpec((nseq, rows, w), lambda b, r, i: (b * groups + r, i, c))
    in_specs = [
        cur_spec(qw, q_blk), prev_spec(k_blk), cur_spec(kw, k_blk), prev_spec(v_blk), cur_spec(kw, v_blk),
        pl.BlockSpec((2, hq, BLOCK, 2 * BLOCK), lambda b, r, i: (0, 0, 0, 0)),
    ]
    args = [view, view, view, view, view, bias]
    if sink is not None:
        in_specs.append(pl.BlockSpec(memory_space=pltpu.SMEM))
        args.append(sink)
    o_spec = pl.BlockSpec((1, rows, nseq * qw), lambda b, r, i: (b, i, r))
    o_shape = jax.ShapeDtypeStruct((batch, sub, dil * qw), out_dtype)
    out_specs, out_shape = o_spec, o_shape
    if want_lse:
        out_specs = [o_spec, pl.BlockSpec((1, rows, nseq * LANES), lambda b, r, i: (b, i, r))]
        out_shape = [o_shape, jax.ShapeDtypeStruct((batch, sub, dil * LANES), F32)]
    res = pl.pallas_call(
        functools.partial(_attn_kernel, hq=hq, grp=grp, has_sink=sink is not None, want_lse=want_lse, nsb=nsb,
                          nseq=nseq),
        grid=(batch, groups, sub // rows),
        in_specs=in_specs,
        out_specs=out_specs,
        out_shape=out_shape,
        compiler_params=_cparams(("parallel", "parallel", "arbitrary")),
        name=name,
    )(*args)
    if want_lse:
        return res[0].reshape(batch * seq, qw), res[1].reshape(batch * seq, LANES)
    return res.reshape(batch * seq, qw)


def _gla_kernel(q_ref, k_ref, v_ref, alr_ref, r_ref, aw_ref, ab_ref, hn_ref, o_ref, st_ref):
    @pl.when(pl.program_id(1) == 0)
    def _():
        st_ref[...] = jnp.zeros_like(st_ref)

    ck = C_CHUNK
    z = jnp.dot(alr_ref[0].astype(BF16), aw_ref[...], preferred_element_type=F32) + ab_ref[...]
    log_a = (jnp.minimum(z, 0.0) - jnp.log1p(jnp.exp(-jnp.abs(z)))) * (1.0 / C_GATE_TAU)
    row = lax.broadcasted_iota(jnp.int32, log_a.shape, 0) & (ck - 1)
    cum = log_a
    shift = 1
    while shift < ck:
        cum = cum + jnp.where(row >= shift, pltpu.roll(cum, shift, 0), 0.0)
        shift *= 2
    q_all = q_ref[0].astype(F32) * (C_DK ** -0.5)
    k_all = k_ref[0].astype(F32)
    q_dec_all = (q_all * jnp.exp(cum)).astype(BF16)
    k_inv_all = (k_all * jnp.exp(-cum)).astype(BF16)
    ti = lax.broadcasted_iota(jnp.int32, (ck, ck), 0)
    si = lax.broadcasted_iota(jnp.int32, (ck, ck), 1)
    causal = ti >= si
    nt = (((1,), (1,)), ((), ()))
    tn = (((0,), (0,)), ((), ()))
    for c in range(GLA_CHUNKS_PER_STEP):
        rs = slice(c * ck, (c + 1) * ck)
        cum_c = cum[rs]
        last = cum_c[ck - 1:ck, :]
        q_dec, k_inv = q_dec_all[rs], k_inv_all[rs]
        k_end = (k_all[rs] * jnp.exp(last - cum_c)).astype(BF16)
        decay = jnp.exp(last)
        v = v_ref[0, rs, :]
        r = r_ref[0, rs, :]
        outs = []
        for h in range(C_HEADS):
            ks = slice(h * C_DK, (h + 1) * C_DK)
            vs = slice(h * C_DV, (h + 1) * C_DV)
            qd, ki, ke, vh = q_dec[:, ks], k_inv[:, ks], k_end[:, ks], v[:, vs]
            att = jnp.where(causal, lax.dot_general(qd, ki, nt, preferred_element_type=F32), 0.0)
            st = st_ref[h]
            o = jnp.dot(att.astype(BF16), vh, preferred_element_type=F32)
            o = o + lax.dot_general(qd, st.astype(BF16), nt, preferred_element_type=F32)
            kv_t = lax.dot_general(vh, ke, tn, preferred_element_type=F32)
            st_ref[h] = st * decay[:, ks] + kv_t
            o = o * lax.rsqrt(jnp.mean(o * o, axis=-1, keepdims=True) + RMS_EPS) * hn_ref[...]
            rh = r[:, vs]
            outs.append(o * (rh * jax.nn.sigmoid(rh)))
        o_ref[0, rs, :] = jnp.concatenate(outs, axis=-1).astype(o_ref.dtype)


def _gla(c_qkv, g_all, aw, ab, hn, *, batch, seq):
    rows = C_CHUNK * GLA_CHUNKS_PER_STEP
    qk_w = C_HEADS * C_DK
    v_w = C_HEADS * C_DV
    cv = c_qkv.reshape(batch, seq, N_WIDTH)
    gv = g_all.reshape(batch, seq, G_WIDTH)
    return pl.pallas_call(
        _gla_kernel,
        grid=(batch, seq // rows),
        in_specs=[
            pl.BlockSpec((1, rows, qk_w), lambda b, c: (b, c, N_CQ_COL // qk_w)),
            pl.BlockSpec((1, rows, qk_w), lambda b, c: (b, c, N_CK_COL // qk_w)),
            pl.BlockSpec((1, rows, v_w), lambda b, c: (b, c, N_CV_COL // v_w)),
            pl.BlockSpec((1, rows, LANES), lambda b, c: (b, c, G_ALR_COL // LANES)),
            pl.BlockSpec((1, rows, v_w), lambda b, c: (b, c, G_R_COL // v_w)),
            pl.BlockSpec((LANES, qk_w), lambda b, c: (0, 0)),
            pl.BlockSpec((1, qk_w), lambda b, c: (0, 0)),
            pl.BlockSpec((1, C_DV), lambda b, c: (0, 0)),
        ],
        out_specs=pl.BlockSpec((1, rows, v_w), lambda b, c: (b, c, 0)),
        out_shape=jax.ShapeDtypeStruct((batch, seq, v_w), BF16),
        scratch_shapes=[pltpu.VMEM((C_HEADS, C_DV, C_DK), F32)],
        compiler_params=_cparams(("parallel", "arbitrary")),
        name="gla",
    )(cv, cv, cv, gv, gv, aw, ab, hn).reshape(batch * seq, v_w)


def _merge_kernel(oa_ref, ob1_ref, ob2_ref, ob3_ref, l1_ref, l2_ref, l3_ref, oc_ref, x_ref,
                  gm_ref, ex_ref, wg_ref, wb_ref, wo_ref, gn_ref, xo_ref, xnt_ref):
    l1, l2, l3 = l1_ref[...], l2_ref[...], l3_ref[...]
    lm = jnp.maximum(jnp.maximum(l1, l2), l3)
    e1, e2, e3 = jnp.exp(l1 - lm), jnp.exp(l2 - lm), jnp.exp(l3 - lm)
    inv = 1.0 / (e1 + e2 + e3)

    def widen(wc):
        hi = wc.astype(BF16)
        r1 = wc - hi.astype(F32)
        mid = r1.astype(BF16)
        lo = (r1 - mid.astype(F32)).astype(BF16)
        ex = ex_ref[...]
        return (jnp.dot(hi, ex, preferred_element_type=F32) + jnp.dot(mid, ex, preferred_element_type=F32)
                + jnp.dot(lo, ex, preferred_element_type=F32))

    ob = (widen(e1 * inv) * ob1_ref[...].astype(F32) + widen(e2 * inv) * ob2_ref[...].astype(F32)
          + widen(e3 * inv) * ob3_ref[...].astype(F32))
    branches = (oa_ref[...], ob.astype(BF16), oc_ref[...])
    x = x_ref[...]
    h = (x * lax.rsqrt(jnp.mean(x * x, axis=-1, keepdims=True) + RMS_EPS) * gm_ref[...]).astype(BF16)
    merged = None
    for n in range(N_BRANCHES):
        gate = jnp.dot(h, wg_ref[:, n * D_MODEL:(n + 1) * D_MODEL], preferred_element_type=F32)
        proj = jnp.dot(branches[n], wb_ref[n], preferred_element_type=F32)
        term = jax.nn.sigmoid(gate) * proj
        merged = term if merged is None else merged + term
    x = x + jnp.dot(merged.astype(BF16), wo_ref[...], preferred_element_type=F32)
    xo_ref[...] = x
    xn = x * lax.rsqrt(jnp.mean(x * x, axis=-1, keepdims=True) + RMS_EPS) * gn_ref[...]
    xnt_ref[...] = xn.T.astype(BF16)


def _head_expander():
    src = np.arange(LANES)[:, None]
    dst = np.arange(BRANCH_WIDTH)[None, :]
    return jnp.asarray(src == (dst // HEAD_DIM) * LSE_LANES, dtype=BF16)


def _merge(o_a, o_b, lse_b, o_c, x2, gm, wg, wb, wo, gn):
    rows = x2.shape[0]
    tm = min(512, rows)
    row_spec = lambda w: pl.BlockSpec((tm, w), lambda i: (i, 0))
    vec_spec = pl.BlockSpec((1, D_MODEL), lambda i: (0, 0))
    return pl.pallas_call(
        _merge_kernel,
        grid=(rows // tm,),
        in_specs=[row_spec(BRANCH_WIDTH)] * 4 + [row_spec(LANES)] * 3 + [
            row_spec(BRANCH_WIDTH),
            row_spec(D_MODEL),
            vec_spec,
            pl.BlockSpec((LANES, BRANCH_WIDTH), lambda i: (0, 0)),
            pl.BlockSpec((D_MODEL, G_GATES), lambda i: (0, 0)),
            pl.BlockSpec((N_BRANCHES, BRANCH_WIDTH, D_MODEL), lambda i: (0, 0, 0)),
            pl.BlockSpec((D_MODEL, D_MODEL), lambda i: (0, 0)),
            vec_spec,
        ],
        out_specs=[row_spec(D_MODEL), pl.BlockSpec((D_MODEL, tm), lambda i: (0, i))],
        out_shape=[jax.ShapeDtypeStruct((rows, D_MODEL), F32),
                   jax.ShapeDtypeStruct((D_MODEL, rows), BF16)],
        compiler_params=_cparams(("parallel",)),
        name="merge",
    )(o_a, o_b[0], o_b[1], o_b[2], lse_b[0], lse_b[1], lse_b[2], o_c, x2, gm.reshape(1, D_MODEL),
      _head_expander(), wg, wb, wo, gn.reshape(1, D_MODEL))


_PAIRS = tuple((i, j) for i in range(PEER_TOPK) for j in range(PEER_TOPK)
               if (i + 1) * (j + 1) <= PEER_TOPK)


def _sort16_desc(v):
    v = list(v)
    n = len(v)
    k = 2
    while k <= n:
        j = k // 2
        while j >= 1:
            for i in range(n):
                l = i ^ j
                if l > i:
                    hi, lo = jnp.maximum(v[i], v[l]), jnp.minimum(v[i], v[l])
                    v[i], v[l] = (hi, lo) if (i & k) == 0 else (lo, hi)
            j //= 2
        k *= 2
    return v


def _merge_top16(a, b):
    n = len(a)
    v = [jnp.maximum(a[i], b[n - 1 - i]) for i in range(n)]
    j = n // 2
    while j >= 1:
        for i in range(n):
            l = i ^ j
            if l > i:
                v[i], v[l] = jnp.maximum(v[i], v[l]), jnp.minimum(v[i], v[l])
        j //= 2
    return v


def _top16_of(vals):
    acc = None
    for g in range(len(vals) // PEER_TOPK):
        grp = _sort16_desc(vals[g * PEER_TOPK:(g + 1) * PEER_TOPK])
        acc = grp if acc is None else _merge_top16(acc, grp)
    return acc


def _peer_select_kernel(xnt_ref, wq_ref, kb_ref, cnt_ref, e1_ref, rank_ref, e2_ref,
                        sc_ref, rk_ref, eb_ref, *, tb):
    nk, nh, kk = PEER_KEYS, PEER_HEADS, PEER_TOPK
    xnt = xnt_ref[...]
    qys = [jnp.dot(wq_ref[p], xnt, preferred_element_type=F32).astype(BF16) for p in range(2)]
    for p in range(2):
        sc_ref[p] = jnp.dot(kb_ref[p], qys[p], preferred_element_type=F32).reshape(nk, nh, tb)

    for c in range(tb // LANES):
        cs = slice(c * LANES, (c + 1) * LANES)
        s1s = _top16_of([sc_ref[0, n, :, cs] for n in range(nk)])
        s2s = _top16_of([sc_ref[1, n, :, cs] for n in range(nk)])
        cands = [s1s[i] + s2s[j] for (i, j) in _PAIRS]
        pad = [jnp.full_like(cands[0], -jnp.inf)] * (-len(cands) % kk)
        tau = _top16_of(cands + pad)[kk - 1]
        top = cands[0]
        zsum = None
        counts = [None] * kk
        for (i, j), cd in zip(_PAIRS, cands):
            sel = cd >= tau
            term = jnp.where(sel, jnp.exp(cd - top), 0.0)
            zsum = term if zsum is None else zsum + term
            one = jnp.where(sel, 1.0, 0.0)
            counts[i] = one if counts[i] is None else counts[i] + one
        inv_z = 1.0 / zsum

        def first_half(n, carry):
            s1 = sc_ref[0, n, :, cs]
            cnt = jnp.zeros_like(s1)
            for i in range(kk):
                cnt = jnp.where(s1 == s1s[i], counts[i], cnt)
            r0 = pl.multiple_of(n * nh, nh)
            cnt_ref[pl.ds(r0, nh), cs] = cnt
            e1_ref[pl.ds(r0, nh), cs] = jnp.exp(s1 - s1s[0])
            return carry

        def second_half(n, carry):
            s2 = sc_ref[1, n, :, cs]
            rank = jnp.full_like(s2, float(kk))
            for j in range(kk):
                rank = jnp.where(s2 == s2s[j], float(j), rank)
            r0 = pl.multiple_of(n * nh, nh)
            rk_ref[pl.ds(r0, nh), :] = rank
            eb_ref[pl.ds(r0, nh), :] = jnp.exp(s2 - s2s[0]) * inv_z
            return carry

        lax.fori_loop(0, nk, first_half, 0, unroll=4)
        lax.fori_loop(0, nk, second_half, 0, unroll=4)
        for h in range(nh):
            rank_ref[h * nk:(h + 1) * nk, cs] = rk_ref[pl.ds(h, nk, stride=nh), :].astype(BF16)
            e2_ref[h * nk:(h + 1) * nk, cs] = eb_ref[pl.ds(h, nk, stride=nh), :].astype(BF16)


def _peer_select(xnt, wq_t, k_big):
    d, rows = xnt.shape
    tb = min(256, rows)
    nrow = PEER_KEYS * PEER_HEADS
    tab_spec = pl.BlockSpec((nrow, tb), lambda i: (0, i))
    return pl.pallas_call(
        functools.partial(_peer_select_kernel, tb=tb),
        grid=(rows // tb,),
        in_specs=[
            pl.BlockSpec((d, tb), lambda i: (0, i)),
            pl.BlockSpec((2, nrow, d), lambda i: (0, 0, 0)),
            pl.BlockSpec((2, nrow, nrow), lambda i: (0, 0, 0)),
        ],
        out_specs=[tab_spec] * 4,
        out_shape=[jax.ShapeDtypeStruct((nrow, rows), dt) for dt in (F32, F32, BF16, BF16)],
        scratch_shapes=[
            pltpu.VMEM((2, PEER_KEYS, PEER_HEADS, tb), F32),
            pltpu.VMEM((nrow, LANES), F32),
            pltpu.VMEM((nrow, LANES), F32),
        ],
        compiler_params=_cparams(("parallel",)),
        name="peer_select",
    )(xnt, wq_t, k_big)


def _gelu_tanh(x):
    c = math.sqrt(2.0 / math.pi)
    return 0.5 * x * (1.0 + jnp.tanh(c * (x + 0.044715 * (x * x * x))))


def _peer_dense_kernel(xnt_ref, u_ref, vt_ref, cnt_ref, e1_ref, rank_ref, e2_ref, x_ref, gn_ref,
                       o_ref, acc_ref, ht_ref, w_ref, *, tb, eb, slab, final_norm):
    j = pl.program_id(1)
    nk, nh = PEER_KEYS, PEER_HEADS
    pk = 16
    n_slab = eb // slab

    @pl.when(j == 0)
    def _():
        acc_ref[...] = jnp.zeros_like(acc_ref)

    def hidden(s):
        rows = slice(s * slab, (s + 1) * slab)
        ht_ref[rows, :] = jnp.dot(u_ref[rows, :], xnt_ref[...], preferred_element_type=F32)

    def gate(s):
        for al in range(s * slab // nk, (s + 1) * slab // nk):
            rows = slice(al * nk, (al + 1) * nk)
            for c in range(tb // LANES):
                cs = slice(c * LANES, (c + 1) * LANES)
                cnt8 = cnt_ref[al * nh:(al + 1) * nh, cs]
                e18 = e1_ref[al * nh:(al + 1) * nh, cs]
                act = _gelu_tanh(ht_ref[rows, cs].astype(BF16)).reshape(nk // pk, pk, LANES)
                g = jnp.zeros((nk // pk, pk, LANES), BF16)
                for h in range(nh):
                    cb = jnp.broadcast_to(cnt8[h:h + 1, :], (pk, LANES)).astype(BF16)
                    eb_ = jnp.broadcast_to(e18[h:h + 1, :], (pk, LANES)).astype(BF16)
                    rk = rank_ref[h * nk:(h + 1) * nk, cs].reshape(nk // pk, pk, LANES)
                    e2 = e2_ref[h * nk:(h + 1) * nk, cs].reshape(nk // pk, pk, LANES)
                    g = g + jnp.where(rk < cb[None], e2, jnp.zeros_like(e2)) * eb_[None]
                w_ref[rows, cs] = (act * g).reshape(nk, LANES)

    def project(s):
        rows = slice(s * slab, (s + 1) * slab)
        acc_ref[...] += jnp.dot(vt_ref[:, rows], w_ref[rows, :], preferred_element_type=F32)

    hidden(0)
    for s in range(n_slab):
        if s + 1 < n_slab:
            hidden(s + 1)
        gate(s)
        if s >= 1:
            project(s - 1)
    project(n_slab - 1)

    @pl.when(j == pl.num_programs(1) - 1)
    def _():
        x = x_ref[...] + acc_ref[...].T
        if final_norm:
            x = x * lax.rsqrt(jnp.mean(x * x, axis=-1, keepdims=True) + RMS_EPS) * gn_ref[...]
        o_ref[...] = x


def _peer_dense(xnt, u, vt, cnt, e1, rank, e2, x2, gn, final_norm):
    d, rows = xnt.shape
    ne = u.shape[0]
    tb = min(512, rows)
    eb = 2048
    slab = 1024
    nrow = PEER_KEYS * PEER_HEADS
    arow = (eb // PEER_KEYS) * PEER_HEADS
    return pl.pallas_call(
        functools.partial(_peer_dense_kernel, tb=tb, eb=eb, slab=slab, final_norm=final_norm),
        grid=(rows // tb, ne // eb),
        in_specs=[
            pl.BlockSpec((d, tb), lambda i, j: (0, i)),
            pl.BlockSpec((eb, d), lambda i, j: (j, 0)),
            pl.BlockSpec((d, eb), lambda i, j: (0, j)),
            pl.BlockSpec((arow, tb), lambda i, j: (j, i)),
            pl.BlockSpec((arow, tb), lambda i, j: (j, i)),
            pl.BlockSpec((nrow, tb), lambda i, j: (0, i)),
            pl.BlockSpec((nrow, tb), lambda i, j: (0, i)),
            pl.BlockSpec((tb, d), lambda i, j: (i, 0)),
            pl.BlockSpec((1, d), lambda i, j: (0, 0)),
        ],
        out_specs=pl.BlockSpec((tb, d), lambda i, j: (i, 0)),
        out_shape=jax.ShapeDtypeStruct((rows, d), F32),
        scratch_shapes=[
            pltpu.VMEM((d, tb), F32),
            pltpu.VMEM((eb, tb), F32),
            pltpu.VMEM((eb, tb), BF16),
        ],
        compiler_params=_cparams(("parallel", "arbitrary")),
        name="peer_dense",
    )(xnt, u, vt, cnt, e1, rank, e2, x2, gn.reshape(1, d))


def _t5_bucket(dist):
    max_exact = N_BUCKETS // 2
    large = max_exact + (jnp.log(jnp.maximum(dist, 1).astype(F32) / max_exact)
                         / math.log(BUCKET_MAX_DIST / max_exact) * (N_BUCKETS - max_exact)).astype(jnp.int32)
    large = jnp.minimum(large, N_BUCKETS - 1)
    return jnp.where(dist < max_exact, dist, large)


def _rel_bias(table_cols, dilation):
    qi = jnp.arange(BLOCK)[:, None]
    kj = jnp.arange(2 * BLOCK)[None, :]
    dist = jnp.maximum(qi + BLOCK - kj, 0) * dilation
    onehot = (_t5_bucket(dist)[..., None] == jnp.arange(N_BUCKETS)).astype(F32)
    return jnp.einsum('qkb,bh->hqk', onehot, table_cols.astype(F32), precision=lax.Precision.HIGHEST)


def _split_w_in(w):
    hd = HEAD_DIM
    sizes = [A_Q_HEADS * hd, A_KV_HEADS * hd, A_KV_HEADS * hd] + [B_HEADS * hd] * 9 + [
        C_HEADS * C_DK, C_HEADS * C_DK, C_HEADS * C_DV, C_GATE_RANK, C_HEADS * C_DV, N_BRANCHES * D_MODEL]
    offs = np.concatenate([[0], np.cumsum(sizes)])
    col = lambda a, b: w[:, int(offs[a]):int(offs[b])].astype(BF16)
    groups = [col(3 + 3 * g, 6 + 3 * g) for g in range(3)]
    w_nat = jnp.concatenate([groups[0], col(14, 15), col(0, 3), col(12, 14)], axis=1)
    pad = jnp.zeros((w.shape[0], G_WIDTH - G_ALR_COL - C_GATE_RANK), BF16)
    w_r = jnp.concatenate([col(16, 17), col(15, 16), pad], axis=1)
    return w_nat, w_r, groups, col(17, 18)


def _peer_key_matrix(keys):
    nh, _, nk, c = keys.shape
    eye = jnp.eye(nh, dtype=keys.dtype)
    big = jnp.einsum('hpnc,hg->pnhgc', keys, eye)
    return big.reshape(2, nk * nh, nh * c).astype(BF16)


def _residue_major(x2, batch, seq, dil):
    if dil == 1:
        return x2
    d = x2.shape[-1]
    return x2.reshape(batch, seq // dil, dil, d).transpose(0, 2, 1, 3).reshape(batch * seq, d)


def kernel(x, w_in, attn_sinks, gla_alpha_w, gla_alpha_b, gla_head_norm, w_branch, w_out, norm_mix,
           norm_ffn, peer_wq, peer_keys, peer_u, peer_v, rel_bias_table, norm_final):
    batch, seq, d = x.shape
    rows = batch * seq
    depth = w_in.shape[0]
    hw = B_HEADS * HEAD_DIM

    bias_a = _banded_bias(_rel_bias(rel_bias_table[:, :A_Q_HEADS], 1), A_WINDOW - 1)
    bias_b = [_banded_bias(
        _rel_bias(rel_bias_table[:, A_Q_HEADS + i * B_HEADS:A_Q_HEADS + (i + 1) * B_HEADS], dil), window // dil)
        for i, (window, dil) in enumerate(B_PATTERNS)]

    x2 = x.reshape(rows, d)
    for l in range(depth):
        w_nat, w_r, w_groups, w_gates = _split_w_in(w_in[l])
        nat, g_all, h_mix = _norm_matmul(x2, norm_mix[l], [w_nat, w_r], [BF16, F32], "proj_nat")

        o_a = _banded_attention(
            nat, bias_a, attn_sinks[l], batch=batch, seq=seq, dil=1, width=N_WIDTH,
            q_col=N_A_COL, k_col=N_A_COL + A_Q_HEADS * HEAD_DIM,
            v_col=N_A_COL + (A_Q_HEADS + A_KV_HEADS) * HEAD_DIM,
            hq=A_Q_HEADS, grp=A_Q_HEADS // A_KV_HEADS, want_lse=False, out_dtype=BF16, name="attn_a")
        o_b, lse_b = [], []
        for gi, (window, dil) in enumerate(B_PATTERNS):
            if dil == 1:
                b_qkv, width, col0 = nat, N_WIDTH, N_B0_COL
            else:
                b_qkv = _matmul(_residue_major(h_mix, batch, seq, dil), w_groups[gi], BF16, "proj_b%d" % gi)
                width, col0 = GROUP_WIDTH, 0
            o, lse = _banded_attention(
                b_qkv, bias_b[gi], None, batch=batch, seq=seq, dil=dil, width=width,
                q_col=col0, k_col=col0 + hw, v_col=col0 + 2 * hw, hq=B_HEADS, grp=1,
                want_lse=True, out_dtype=BF16, name="attn_b%d" % gi)
            o_b.append(o)
            lse_b.append(lse)

        aw = jnp.zeros((LANES, C_HEADS * C_DK), BF16).at[:C_GATE_RANK].set(gla_alpha_w[l].astype(BF16))
        o_c = _gla(nat, g_all, aw, gla_alpha_b[l].reshape(1, -1), gla_head_norm[l].reshape(1, -1),
                   batch=batch, seq=seq)

        x2, xnt = _merge(o_a, o_b, lse_b, o_c, x2, norm_mix[l], w_gates, w_branch[l].astype(BF16),
                         w_out[l].astype(BF16), norm_ffn[l])

        wq_t = peer_wq[l].reshape(d, PEER_HEADS, 2, PEER_DKEY // 2).transpose(2, 1, 3, 0)
        wq_t = wq_t.reshape(2, PEER_HEADS * (PEER_DKEY // 2), d).astype(BF16)
        cnt, e1, rank, e2 = _peer_select(xnt, wq_t, _peer_key_matrix(peer_keys[l]))
        last = l == depth - 1
        x2 = _peer_dense(xnt, peer_u[l].astype(BF16), peer_v[l].T.astype(BF16), cnt, e1, rank, e2,
                         x2, norm_final, final_norm=last)
    return x2.reshape(batch, seq, d)
```

```python
import functools
import math

import numpy as np
import jax
import jax.numpy as jnp
from jax import lax
from jax.experimental import pallas as pl
from jax.experimental.pallas import tpu as pltpu

F32 = jnp.float32
BF16 = jnp.bfloat16

D_MODEL = 1024
HEAD_DIM = 64
BLOCK = 128
A_Q_HEADS = 8
A_KV_HEADS = 2
A_WINDOW = 128
B_PATTERNS = ((128, 1), (512, 4), (2048, 16))
B_HEADS = 8
C_HEADS = 4
C_DK = 64
C_DV = 128
C_GATE_RANK = 16
C_GATE_TAU = 16.0
C_CHUNK = 64
BRANCH_WIDTH = 512
N_BRANCHES = 3
N_BUCKETS = 32
BUCKET_MAX_DIST = 2048
PEER_HEADS = 8
PEER_KEYS = 128
PEER_N_EXPERTS = PEER_KEYS * PEER_KEYS
PEER_TOPK = 16
PEER_DKEY = 256
RMS_EPS = 1e-6
NEG_INF = -1e30

LANES = 128
VMEM_LIMIT = 56 * 1024 * 1024

GROUP_WIDTH = 3 * B_HEADS * HEAD_DIM
N_B0_COL = 0
N_CV_COL = GROUP_WIDTH
N_A_COL = N_CV_COL + C_HEADS * C_DV
N_CQ_COL = N_A_COL + (A_Q_HEADS + 2 * A_KV_HEADS) * HEAD_DIM
N_CK_COL = N_CQ_COL + C_HEADS * C_DK
N_WIDTH = N_CK_COL + C_HEADS * C_DK
G_GATES = N_BRANCHES * D_MODEL
G_WIDTH = C_HEADS * C_DV + LANES
G_R_COL = 0
G_ALR_COL = C_HEADS * C_DV
GLA_CHUNKS_PER_STEP = 8
ATTN_ROWS_PER_STEP = 2 * BLOCK
LSE_LANES = LANES // B_HEADS


def _cparams(sem):
    return pltpu.CompilerParams(dimension_semantics=sem, vmem_limit_bytes=VMEM_LIMIT)


def _norm_matmul_kernel(x_ref, g_ref, *refs):
    n_out = len(refs) // 2
    x = x_ref[...]
    ms = jnp.mean(x * x, axis=-1, keepdims=True)
    h = (x * lax.rsqrt(ms + RMS_EPS) * g_ref[...]).astype(BF16)
    for w_ref, o_ref in zip(refs[:n_out], refs[n_out:2 * n_out]):
        o_ref[...] = jnp.dot(h, w_ref[...], preferred_element_type=F32).astype(o_ref.dtype)
    refs[2 * n_out][...] = h


def _norm_matmul(x2, g, ws, out_dtypes, name):
    rows, d = x2.shape
    tm = min(512, rows)
    row_spec = lambda n: pl.BlockSpec((tm, n), lambda i: (i, 0))
    return pl.pallas_call(
        _norm_matmul_kernel,
        grid=(rows // tm,),
        in_specs=[row_spec(d), pl.BlockSpec((1, d), lambda i: (0, 0))]
        + [pl.BlockSpec(w.shape, lambda i: (0, 0)) for w in ws],
        out_specs=[row_spec(w.shape[1]) for w in ws] + [row_spec(d)],
        out_shape=[jax.ShapeDtypeStruct((rows, w.shape[1]), dt) for w, dt in zip(ws, out_dtypes)]
        + [jax.ShapeDtypeStruct((rows, d), BF16)],
        compiler_params=_cparams(("parallel",)),
        name=name,
    )(x2, g.reshape(1, d), *ws)


def _matmul_kernel(h_ref, w_ref, o_ref):
    o_ref[...] = jnp.dot(h_ref[...], w_ref[...], preferred_element_type=F32).astype(o_ref.dtype)


def _matmul(h, w, out_dtype, name):
    rows, d = h.shape
    tm = min(1024, rows)
    return pl.pallas_call(
        _matmul_kernel,
        grid=(rows // tm,),
        in_specs=[pl.BlockSpec((tm, d), lambda i: (i, 0)), pl.BlockSpec(w.shape, lambda i: (0, 0))],
        out_specs=pl.BlockSpec((tm, w.shape[1]), lambda i: (i, 0)),
        out_shape=jax.ShapeDtypeStruct((rows, w.shape[1]), out_dtype),
        compiler_params=_cparams(("parallel",)),
        name=name,
    )(h, w)


def _attn_kernel(*refs, hq, grp, has_sink, want_lse, nsb, nseq):
    q_ref, kp_ref, kc_ref, vp_ref, vc_ref, bias_ref = refs[:6]
    pos = 6
    sink_ref = None
    if has_sink:
        sink_ref = refs[pos]
        pos += 1
    o_ref = refs[pos]
    lse_ref = refs[pos + 1] if want_lse else None

    first = jnp.minimum(pl.program_id(2), 1)
    qw = hq * HEAD_DIM
    tiles = [(sq, sb, h) for sq in range(nseq) for sb in range(nsb) for h in range(hq)]
    kv = {}
    for sq in range(nseq):
        k = jnp.concatenate([kp_ref[sq], kc_ref[sq]], axis=0)
        v = jnp.concatenate([vp_ref[sq], vc_ref[sq]], axis=0)
        for sb in range(nsb):
            kv[sq, sb] = (k[sb * BLOCK:(sb + 2) * BLOCK], v[sb * BLOCK:(sb + 2) * BLOCK])
    scores = {}
    for sq, sb, h in tiles:
        kvh = h // grp
        q = q_ref[sq, sb * BLOCK:(sb + 1) * BLOCK, h * HEAD_DIM:(h + 1) * HEAD_DIM] * (HEAD_DIM ** -0.5)
        kh = kv[sq, sb][0][:, kvh * HEAD_DIM:(kvh + 1) * HEAD_DIM]
        s = lax.dot_general(q, kh, (((1,), (1,)), ((), ())), preferred_element_type=F32)
        bias = bias_ref[first, h] if sb == 0 else bias_ref[1, h]
        scores[sq, sb, h] = s + bias
    probs = {}
    for t in tiles:
        h = t[2]
        s = scores[t]
        m = jnp.max(s, axis=-1, keepdims=True)
        if has_sink:
            m = jnp.maximum(m, sink_ref[h])
        p = jnp.exp(s - m)
        denom = jnp.sum(p, axis=-1, keepdims=True)
        if has_sink:
            denom = denom + jnp.exp(sink_ref[h] - m)
        probs[t] = (p.astype(BF16), denom, m)
    for sq in range(nseq):
        for sb in range(nsb):
            outs, lses = [], []
            for h in range(hq):
                p, denom, m = probs[sq, sb, h]
                kvh = h // grp
                vh = kv[sq, sb][1][:, kvh * HEAD_DIM:(kvh + 1) * HEAD_DIM]
                outs.append(jnp.dot(p, vh, preferred_element_type=F32) / denom)
                if want_lse:
                    lses.append(jnp.broadcast_to(m + jnp.log(denom), (BLOCK, LSE_LANES)))
            rs = slice(sb * BLOCK, (sb + 1) * BLOCK)
            o_ref[0, rs, sq * qw:(sq + 1) * qw] = jnp.concatenate(outs, axis=-1).astype(o_ref.dtype)
            if want_lse:
                lse_ref[0, rs, sq * LANES:(sq + 1) * LANES] = jnp.concatenate(lses, axis=-1)


def _banded_bias(bias, max_dist):
    qi = np.arange(BLOCK)[:, None]
    kj = np.arange(2 * BLOCK)[None, :]
    dist = qi + BLOCK - kj
    in_window = (dist >= 0) & (dist <= max_dist)
    first = in_window & (kj >= BLOCK)
    return jnp.stack([jnp.where(first, bias, NEG_INF), jnp.where(in_window, bias, NEG_INF)])


def _banded_attention(qkv, bias, sink, *, batch, seq, dil, width, q_col, k_col, v_col,
                      hq, grp, want_lse, out_dtype, name):
    sub = seq // dil
    rows = min(ATTN_ROWS_PER_STEP, sub)
    nsb = rows // BLOCK
    nseq = min(ATTN_ROWS_PER_STEP // rows, dil)
    qw = hq * HEAD_DIM
    kw = (hq // grp) * HEAD_DIM
    view = qkv.reshape(batch * dil, sub, width)
    q_blk, k_blk, v_blk = q_col // qw, k_col // kw, v_col // kw
    assert q_col % qw == 0 and k_col % kw == 0 and v_col % kw == 0
    groups = dil // nseq

    prev_spec = lambda c: pl.BlockSpec((nseq, BLOCK, kw),
                                       lambda b, r, i: (b * groups + r, jnp.maximum(i * nsb - 1, 0), c))
    cur_spec = lambda w, c: pl.BlockSpec((nseq, rows, w), lambda b, r, i: (b * groups + r, i, c))
    in_specs = [
        cur_spec(qw, q_blk), prev_spec(k_blk), cur_spec(kw, k_blk), prev_spec(v_blk), cur_spec(kw, v_blk),
        pl.BlockSpec((2, hq, BLOCK, 2 * BLOCK), lambda b, r, i: (0, 0, 0, 0)),
    ]
    args = [view, view, view, view, view, bias]
    if sink is not None:
        in_specs.append(pl.BlockSpec(memory_space=pltpu.SMEM))
        args.append(sink)
    o_spec = pl.BlockSpec((1, rows, nseq * qw), lambda b, r, i: (b, i, r))
    o_shape = jax.ShapeDtypeStruct((batch, sub, dil * qw), out_dtype)
    out_specs, out_shape = o_spec, o_shape
    if want_lse:
        out_specs = [o_spec, pl.BlockSpec((1, rows, nseq * LANES), lambda b, r, i: (b, i, r))]
        out_shape = [o_shape, jax.ShapeDtypeStruct((batch, sub, dil * LANES), F32)]
    res = pl.pallas_call(
        functools.partial(_attn_kernel, hq=hq, grp=grp, has_sink=sink is not None, want_lse=want_lse, nsb=nsb,
                          nseq=nseq),
        grid=(batch, groups, sub // rows),
        in_specs=in_specs,
        out_specs=out_specs,
        out_shape=out_shape,
        compiler_params=_cparams(("parallel", "parallel", "arbitrary")),
        name=name,
    )(*args)
    if want_lse:
        return res[0].reshape(batch * seq, qw), res[1].reshape(batch * seq, LANES)
    return res.reshape(batch * seq, qw)


def _gla_kernel(q_ref, k_ref, v_ref, alr_ref, r_ref, aw_ref, ab_ref, hn_ref, o_ref, st_ref):
    @pl.when(pl.program_id(1) == 0)
    def _():
        st_ref[...] = jnp.zeros_like(st_ref)

    ck = C_CHUNK
    z = jnp.dot(alr_ref[0].astype(BF16), aw_ref[...], preferred_element_type=F32) + ab_ref[...]
    log_a = (jnp.minimum(z, 0.0) - jnp.log1p(jnp.exp(-jnp.abs(z)))) * (1.0 / C_GATE_TAU)
    row = lax.broadcasted_iota(jnp.int32, log_a.shape, 0) & (ck - 1)
    cum = log_a
    shift = 1
    while shift < ck:
        cum = cum + jnp.where(row >= shift, pltpu.roll(cum, shift, 0), 0.0)
        shift *= 2
    q_all = q_ref[0].astype(F32) * (C_DK ** -0.5)
    k_all = k_ref[0].astype(F32)
    q_dec_all = (q_all * jnp.exp(cum)).astype(BF16)
    k_inv_all = (k_all * jnp.exp(-cum)).astype(BF16)
    ti = lax.broadcasted_iota(jnp.int32, (ck, ck), 0)
    si = lax.broadcasted_iota(jnp.int32, (ck, ck), 1)
    causal = ti >= si
    nt = (((1,), (1,)), ((), ()))
    tn = (((0,), (0,)), ((), ()))
    for c in range(GLA_CHUNKS_PER_STEP):
        rs = slice(c * ck, (c + 1) * ck)
        cum_c = cum[rs]
        last = cum_c[ck - 1:ck, :]
        q_dec, k_inv = q_dec_all[rs], k_inv_all[rs]
        k_end = (k_all[rs] * jnp.exp(last - cum_c)).astype(BF16)
        decay = jnp.exp(last)
        v = v_ref[0, rs, :]
        r = r_ref[0, rs, :]
        outs = []
        for h in range(C_HEADS):
            ks = slice(h * C_DK, (h + 1) * C_DK)
            vs = slice(h * C_DV, (h + 1) * C_DV)
            qd, ki, ke, vh = q_dec[:, ks], k_inv[:, ks], k_end[:, ks], v[:, vs]
            att = jnp.where(causal, lax.dot_general(qd, ki, nt, preferred_element_type=F32), 0.0)
            st = st_ref[h]
            o = jnp.dot(att.astype(BF16), vh, preferred_element_type=F32)
            o = o + lax.dot_general(qd, st.astype(BF16), nt, preferred_element_type=F32)
            kv_t = lax.dot_general(vh, ke, tn, preferred_element_type=F32)
            st_ref[h] = st * decay[:, ks] + kv_t
            o = o * lax.rsqrt(jnp.mean(o * o, axis=-1, keepdims=True) + RMS_EPS) * hn_ref[...]
            rh = r[:, vs]
            outs.append(o * (rh * jax.nn.sigmoid(rh)))
        o_ref[0, rs, :] = jnp.concatenate(outs, axis=-1).astype(o_ref.dtype)


def _gla(c_qkv, g_all, aw, ab, hn, *, batch, seq):
    rows = C_CHUNK * GLA_CHUNKS_PER_STEP
    qk_w = C_HEADS * C_DK
    v_w = C_HEADS * C_DV
    cv = c_qkv.reshape(batch, seq, N_WIDTH)
    gv = g_all.reshape(batch, seq, G_WIDTH)
    return pl.pallas_call(
        _gla_kernel,
        grid=(batch, seq // rows),
        in_specs=[
            pl.BlockSpec((1, rows, qk_w), lambda b, c: (b, c, N_CQ_COL // qk_w)),
            pl.BlockSpec((1, rows, qk_w), lambda b, c: (b, c, N_CK_COL // qk_w)),
            pl.BlockSpec((1, rows, v_w), lambda b, c: (b, c, N_CV_COL // v_w)),
            pl.BlockSpec((1, rows, LANES), lambda b, c: (b, c, G_ALR_COL // LANES)),
            pl.BlockSpec((1, rows, v_w), lambda b, c: (b, c, G_R_COL // v_w)),
            pl.BlockSpec((LANES, qk_w), lambda b, c: (0, 0)),
            pl.BlockSpec((1, qk_w), lambda b, c: (0, 0)),
            pl.BlockSpec((1, C_DV), lambda b, c: (0, 0)),
        ],
        out_specs=pl.BlockSpec((1, rows, v_w), lambda b, c: (b, c, 0)),
        out_shape=jax.ShapeDtypeStruct((batch, seq, v_w), BF16),
        scratch_shapes=[pltpu.VMEM((C_HEADS, C_DV, C_DK), F32)],
        compiler_params=_cparams(("parallel", "arbitrary")),
        name="gla",
    )(cv, cv, cv, gv, gv, aw, ab, hn).reshape(batch * seq, v_w)


def _merge_kernel(oa_ref, ob1_ref, ob2_ref, ob3_ref, l1_ref, l2_ref, l3_ref, oc_ref, x_ref,
                  gm_ref, ex_ref, wg_ref, wb_ref, wo_ref, gn_ref, xo_ref, xnt_ref):
    l1, l2, l3 = l1_ref[...], l2_ref[...], l3_ref[...]
    lm = jnp.maximum(jnp.maximum(l1, l2), l3)
    e1, e2, e3 = jnp.exp(l1 - lm), jnp.exp(l2 - lm), jnp.exp(l3 - lm)
    inv = 1.0 / (e1 + e2 + e3)

    def widen(wc):
        hi = wc.astype(BF16)
        r1 = wc - hi.astype(F32)
        mid = r1.astype(BF16)
        lo = (r1 - mid.astype(F32)).astype(BF16)
        ex = ex_ref[...]
        return (jnp.dot(hi, ex, preferred_element_type=F32) + jnp.dot(mid, ex, preferred_element_type=F32)
                + jnp.dot(lo, ex, preferred_element_type=F32))

    ob = (widen(e1 * inv) * ob1_ref[...].astype(F32) + widen(e2 * inv) * ob2_ref[...].astype(F32)
          + widen(e3 * inv) * ob3_ref[...].astype(F32))
    branches = (oa_ref[...], ob.astype(BF16), oc_ref[...])
    x = x_ref[...]
    h = (x * lax.rsqrt(jnp.mean(x * x, axis=-1, keepdims=True) + RMS_EPS) * gm_ref[...]).astype(BF16)
    merged = None
    for n in range(N_BRANCHES):
        gate = jnp.dot(h, wg_ref[:, n * D_MODEL:(n + 1) * D_MODEL], preferred_element_type=F32)
        proj = jnp.dot(branches[n], wb_ref[n], preferred_element_type=F32)
        term = jax.nn.sigmoid(gate) * proj
        merged = term if merged is None else merged + term
    x = x + jnp.dot(merged.astype(BF16), wo_ref[...], preferred_element_type=F32)
    xo_ref[...] = x
    xn = x * lax.rsqrt(jnp.mean(x * x, axis=-1, keepdims=True) + RMS_EPS) * gn_ref[...]
    xnt_ref[...] = xn.T.astype(BF16)


def _head_expander():
    src = np.arange(LANES)[:, None]
    dst = np.arange(BRANCH_WIDTH)[None, :]
    return jnp.asarray(src == (dst // HEAD_DIM) * LSE_LANES, dtype=BF16)


def _merge(o_a, o_b, lse_b, o_c, x2, gm, wg, wb, wo, gn):
    rows = x2.shape[0]
    tm = min(512, rows)
    row_spec = lambda w: pl.BlockSpec((tm, w), lambda i: (i, 0))
    vec_spec = pl.BlockSpec((1, D_MODEL), lambda i: (0, 0))
    return pl.pallas_call(
        _merge_kernel,
        grid=(rows // tm,),
        in_specs=[row_spec(BRANCH_WIDTH)] * 4 + [row_spec(LANES)] * 3 + [
            row_spec(BRANCH_WIDTH),
            row_spec(D_MODEL),
            vec_spec,
            pl.BlockSpec((LANES, BRANCH_WIDTH), lambda i: (0, 0)),
            pl.BlockSpec((D_MODEL, G_GATES), lambda i: (0, 0)),
            pl.BlockSpec((N_BRANCHES, BRANCH_WIDTH, D_MODEL), lambda i: (0, 0, 0)),
            pl.BlockSpec((D_MODEL, D_MODEL), lambda i: (0, 0)),
            vec_spec,
        ],
        out_specs=[row_spec(D_MODEL), pl.BlockSpec((D_MODEL, tm), lambda i: (0, i))],
        out_shape=[jax.ShapeDtypeStruct((rows, D_MODEL), F32),
                   jax.ShapeDtypeStruct((D_MODEL, rows), BF16)],
        compiler_params=_cparams(("parallel",)),
        name="merge",
    )(o_a, o_b[0], o_b[1], o_b[2], lse_b[0], lse_b[1], lse_b[2], o_c, x2, gm.reshape(1, D_MODEL),
      _head_expander(), wg, wb, wo, gn.reshape(1, D_MODEL))


_PAIRS = tuple((i, j) for i in range(PEER_TOPK) for j in range(PEER_TOPK)
               if (i + 1) * (j + 1) <= PEER_TOPK)


def _sort16_desc(v):
    v = list(v)
    n = len(v)
    k = 2
    while k <= n:
        j = k // 2
        while j >= 1:
            for i in range(n):
                l = i ^ j
                if l > i:
                    hi, lo = jnp.maximum(v[i], v[l]), jnp.minimum(v[i], v[l])
                    v[i], v[l] = (hi, lo) if (i & k) == 0 else (lo, hi)
            j //= 2
        k *= 2
    return v


def _merge_top16(a, b):
    n = len(a)
    v = [jnp.maximum(a[i], b[n - 1 - i]) for i in range(n)]
    j = n // 2
    while j >= 1:
        for i in range(n):
            l = i ^ j
            if l > i:
                v[i], v[l] = jnp.maximum(v[i], v[l]), jnp.minimum(v[i], v[l])
        j //= 2
    return v


def _top16_of(vals):
    acc = None
    for g in range(len(vals) // PEER_TOPK):
        grp = _sort16_desc(vals[g * PEER_TOPK:(g + 1) * PEER_TOPK])
        acc = grp if acc is None else _merge_top16(acc, grp)
    return acc


def _peer_select_kernel(xnt_ref, wq_ref, kb_ref, cnt_ref, e1_ref, rank_ref, e2_ref,
                        sc_ref, rk_ref, eb_ref, *, tb):
    nk, nh, kk = PEER_KEYS, PEER_HEADS, PEER_TOPK
    xnt = xnt_ref[...]
    qys = [jnp.dot(wq_ref[p], xnt, preferred_element_type=F32).astype(BF16) for p in range(2)]
    for p in range(2):
        sc_ref[p] = jnp.dot(kb_ref[p], qys[p], preferred_element_type=F32).reshape(nk, nh, tb)

    for c in range(tb // LANES):
        cs = slice(c * LANES, (c + 1) * LANES)
        s1s = _top16_of([sc_ref[0, n, :, cs] for n in range(nk)])
        s2s = _top16_of([sc_ref[1, n, :, cs] for n in range(nk)])
        cands = [s1s[i] + s2s[j] for (i, j) in _PAIRS]
        pad = [jnp.full_like(cands[0], -jnp.inf)] * (-len(cands) % kk)
        tau = _top16_of(cands + pad)[kk - 1]
        top = cands[0]
        zsum = None
        counts = [None] * kk
        for (i, j), cd in zip(_PAIRS, cands):
            sel = cd >= tau
            term = jnp.where(sel, jnp.exp(cd - top), 0.0)
            zsum = term if zsum is None else zsum + term
            one = jnp.where(sel, 1.0, 0.0)
            counts[i] = one if counts[i] is None else counts[i] + one
        inv_z = 1.0 / zsum

        def first_half(n, carry):
            s1 = sc_ref[0, n, :, cs]
            cnt = jnp.zeros_like(s1)
            for i in range(kk):
                cnt = jnp.where(s1 == s1s[i], counts[i], cnt)
            r0 = pl.multiple_of(n * nh, nh)
            cnt_ref[pl.ds(r0, nh), cs] = cnt
            e1_ref[pl.ds(r0, nh), cs] = jnp.exp(s1 - s1s[0])
            return carry

        def second_half(n, carry):
            s2 = sc_ref[1, n, :, cs]
            rank = jnp.full_like(s2, float(kk))
            for j in range(kk):
                rank = jnp.where(s2 == s2s[j], float(j), rank)
            r0 = pl.multiple_of(n * nh, nh)
            rk_ref[pl.ds(r0, nh), :] = rank
            eb_ref[pl.ds(r0, nh), :] = jnp.exp(s2 - s2s[0]) * inv_z
            return carry

        lax.fori_loop(0, nk, first_half, 0, unroll=4)
        lax.fori_loop(0, nk, second_half, 0, unroll=4)
        for h in range(nh):
            rank_ref[h * nk:(h + 1) * nk, cs] = rk_ref[pl.ds(h, nk, stride=nh), :].astype(BF16)
            e2_ref[h * nk:(h + 1) * nk, cs] = eb_ref[pl.ds(h, nk, stride=nh), :].astype(BF16)


def _peer_select(xnt, wq_t, k_big):
    d, rows = xnt.shape
    tb = min(256, rows)
    nrow = PEER_KEYS * PEER_HEADS
    tab_spec = pl.BlockSpec((nrow, tb), lambda i: (0, i))
    return pl.pallas_call(
        functools.partial(_peer_select_kernel, tb=tb),
        grid=(rows // tb,),
        in_specs=[
            pl.BlockSpec((d, tb), lambda i: (0, i)),
            pl.BlockSpec((2, nrow, d), lambda i: (0, 0, 0)),
            pl.BlockSpec((2, nrow, nrow), lambda i: (0, 0, 0)),
        ],
        out_specs=[tab_spec] * 4,
        out_shape=[jax.ShapeDtypeStruct((nrow, rows), dt) for dt in (F32, F32, BF16, BF16)],
        scratch_shapes=[
            pltpu.VMEM((2, PEER_KEYS, PEER_HEADS, tb), F32),
            pltpu.VMEM((nrow, LANES), F32),
            pltpu.VMEM((nrow, LANES), F32),
        ],
        compiler_params=_cparams(("parallel",)),
        name="peer_select",
    )(xnt, wq_t, k_big)


def _gelu_tanh(x):
    c = math.sqrt(2.0 / math.pi)
    return 0.5 * x * (1.0 + jnp.tanh(c * (x + 0.044715 * (x * x * x))))


def _peer_dense_kernel(xnt_ref, u_ref, vt_ref, cnt_ref, e1_ref, rank_ref, e2_ref, x_ref, gn_ref,
                       o_ref, acc_ref, ht_ref, w_ref, *, tb, eb, slab, final_norm):
    j = pl.program_id(1)
    nk, nh = PEER_KEYS, PEER_HEADS
    pk = 16
    n_slab = eb // slab

    @pl.when(j == 0)
    def _():
        acc_ref[...] = jnp.zeros_like(acc_ref)

    def hidden(s):
        rows = slice(s * slab, (s + 1) * slab)
        ht_ref[rows, :] = jnp.dot(u_ref[rows, :], xnt_ref[...], preferred_element_type=F32)

    def gate(s):
        for al in range(s * slab // nk, (s + 1) * slab // nk):
            rows = slice(al * nk, (al + 1) * nk)
            for c in range(tb // LANES):
                cs = slice(c * LANES, (c + 1) * LANES)
                cnt8 = cnt_ref[al * nh:(al + 1) * nh, cs]
                e18 = e1_ref[al * nh:(al + 1) * nh, cs]
                act = _gelu_tanh(ht_ref[rows, cs].astype(BF16)).reshape(nk // pk, pk, LANES)
                g = jnp.zeros((nk // pk, pk, LANES), BF16)
                for h in range(nh):
                    cb = jnp.broadcast_to(cnt8[h:h + 1, :], (pk, LANES)).astype(BF16)
                    eb_ = jnp.broadcast_to(e18[h:h + 1, :], (pk, LANES)).astype(BF16)
                    rk = rank_ref[h * nk:(h + 1) * nk, cs].reshape(nk // pk, pk, LANES)
                    e2 = e2_ref[h * nk:(h + 1) * nk, cs].reshape(nk // pk, pk, LANES)
                    g = g + jnp.where(rk < cb[None], e2, jnp.zeros_like(e2)) * eb_[None]
                w_ref[rows, cs] = (act * g).reshape(nk, LANES)

    def project(s):
        rows = slice(s * slab, (s + 1) * slab)
        acc_ref[...] += jnp.dot(vt_ref[:, rows], w_ref[rows, :], preferred_element_type=F32)

    hidden(0)
    for s in range(n_slab):
        if s + 1 < n_slab:
            hidden(s + 1)
        gate(s)
        if s >= 1:
            project(s - 1)
    project(n_slab - 1)

    @pl.when(j == pl.num_programs(1) - 1)
    def _():
        x = x_ref[...] + acc_ref[...].T
        if final_norm:
            x = x * lax.rsqrt(jnp.mean(x * x, axis=-1, keepdims=True) + RMS_EPS) * gn_ref[...]
        o_ref[...] = x


def _peer_dense(xnt, u, vt, cnt, e1, rank, e2, x2, gn, final_norm):
    d, rows = xnt.shape
    ne = u.shape[0]
    tb = min(512, rows)
    eb = 2048
    slab = 1024
    nrow = PEER_KEYS * PEER_HEADS
    arow = (eb // PEER_KEYS) * PEER_HEADS
    return pl.pallas_call(
        functools.partial(_peer_dense_kernel, tb=tb, eb=eb, slab=slab, final_norm=final_norm),
        grid=(rows // tb, ne // eb),
        in_specs=[
            pl.BlockSpec((d, tb), lambda i, j: (0, i)),
            pl.BlockSpec((eb, d), lambda i, j: (j, 0)),
            pl.BlockSpec((d, eb), lambda i, j: (0, j)),
            pl.BlockSpec((arow, tb), lambda i, j: (j, i)),
            pl.BlockSpec((arow, tb), lambda i, j: (j, i)),
            pl.BlockSpec((nrow, tb), lambda i, j: (0, i)),
            pl.BlockSpec((nrow, tb), lambda i, j: (0, i)),
            pl.BlockSpec((tb, d), lambda i, j: (i, 0)),
            pl.BlockSpec((1, d), lambda i, j: (0, 0)),
        ],
        out_specs=pl.BlockSpec((tb, d), lambda i, j: (i, 0)),
        out_shape=jax.ShapeDtypeStruct((rows, d), F32),
        scratch_shapes=[
            pltpu.VMEM((d, tb), F32),
            pltpu.VMEM((eb, tb), F32),
            pltpu.VMEM((eb, tb), BF16),
        ],
        compiler_params=_cparams(("parallel", "arbitrary")),
        name="peer_dense",
    )(xnt, u, vt, cnt, e1, rank, e2, x2, gn.reshape(1, d))


def _t5_bucket(dist):
    max_exact = N_BUCKETS // 2
    large = max_exact + (jnp.log(jnp.maximum(dist, 1).astype(F32) / max_exact)
                         / math.log(BUCKET_MAX_DIST / max_exact) * (N_BUCKETS - max_exact)).astype(jnp.int32)
    large = jnp.minimum(large, N_BUCKETS - 1)
    return jnp.where(dist < max_exact, dist, large)


def _rel_bias(table_cols, dilation):
    qi = jnp.arange(BLOCK)[:, None]
    kj = jnp.arange(2 * BLOCK)[None, :]
    dist = jnp.maximum(qi + BLOCK - kj, 0) * dilation
    onehot = (_t5_bucket(dist)[..., None] == jnp.arange(N_BUCKETS)).astype(F32)
    return jnp.einsum('qkb,bh->hqk', onehot, table_cols.astype(F32), precision=lax.Precision.HIGHEST)


def _split_w_in(w):
    hd = HEAD_DIM
    sizes = [A_Q_HEADS * hd, A_KV_HEADS * hd, A_KV_HEADS * hd] + [B_HEADS * hd] * 9 + [
        C_HEADS * C_DK, C_HEADS * C_DK, C_HEADS * C_DV, C_GATE_RANK, C_HEADS * C_DV, N_BRANCHES * D_MODEL]
    offs = np.concatenate([[0], np.cumsum(sizes)])
    col = lambda a, b: w[:, int(offs[a]):int(offs[b])].astype(BF16)
    groups = [col(3 + 3 * g, 6 + 3 * g) for g in range(3)]
    w_nat = jnp.concatenate([groups[0], col(14, 15), col(0, 3), col(12, 14)], axis=1)
    pad = jnp.zeros((w.shape[0], G_WIDTH - G_ALR_COL - C_GATE_RANK), BF16)
    w_r = jnp.concatenate([col(16, 17), col(15, 16), pad], axis=1)
    return w_nat, w_r, groups, col(17, 18)


def _peer_key_matrix(keys):
    nh, _, nk, c = keys.shape
    eye = jnp.eye(nh, dtype=keys.dtype)
    big = jnp.einsum('hpnc,hg->pnhgc', keys, eye)
    return big.reshape(2, nk * nh, nh * c).astype(BF16)


def _residue_major(x2, batch, seq, dil):
    if dil == 1:
        return x2
    d = x2.shape[-1]
    return x2.reshape(batch, seq // dil, dil, d).transpose(0, 2, 1, 3).reshape(batch * seq, d)


def kernel(x, w_in, attn_sinks, gla_alpha_w, gla_alpha_b, gla_head_norm, w_branch, w_out, norm_mix,
           norm_ffn, peer_wq, peer_keys, peer_u, peer_v, rel_bias_table, norm_final):
    batch, seq, d = x.shape
    rows = batch * seq
    depth = w_in.shape[0]
    hw = B_HEADS * HEAD_DIM

    bias_a = _banded_bias(_rel_bias(rel_bias_table[:, :A_Q_HEADS], 1), A_WINDOW - 1)
    bias_b = [_banded_bias(
        _rel_bias(rel_bias_table[:, A_Q_HEADS + i * B_HEADS:A_Q_HEADS + (i + 1) * B_HEADS], dil), window // dil)
        for i, (window, dil) in enumerate(B_PATTERNS)]

    x2 = x.reshape(rows, d)
    for l in range(depth):
        w_nat, w_r, w_groups, w_gates = _split_w_in(w_in[l])
        nat, g_all, h_mix = _norm_matmul(x2, norm_mix[l], [w_nat, w_r], [BF16, F32], "proj_nat")

        o_a = _banded_attention(
            nat, bias_a, attn_sinks[l], batch=batch, seq=seq, dil=1, width=N_WIDTH,
            q_col=N_A_COL, k_col=N_A_COL + A_Q_HEADS * HEAD_DIM,
            v_col=N_A_COL + (A_Q_HEADS + A_KV_HEADS) * HEAD_DIM,
            hq=A_Q_HEADS, grp=A_Q_HEADS // A_KV_HEADS, want_lse=False, out_dtype=BF16, name="attn_a")
        o_b, lse_b = [], []
        for gi, (window, dil) in enumerate(B_PATTERNS):
            if dil == 1:
                b_qkv, width, col0 = nat, N_WIDTH, N_B0_COL
            else:
                b_qkv = _matmul(_residue_major(h_mix, batch, seq, dil), w_groups[gi], BF16, "proj_b%d" % gi)
                width, col0 = GROUP_WIDTH, 0
            o, lse = _banded_attention(
                b_qkv, bias_b[gi], None, batch=batch, seq=seq, dil=dil, width=width,
                q_col=col0, k_col=col0 + hw, v_col=col0 + 2 * hw, hq=B_HEADS, grp=1,
                want_lse=True, out_dtype=BF16, name="attn_b%d" % gi)
            o_b.append(o)
            lse_b.append(lse)

        aw = jnp.zeros((LANES, C_HEADS * C_DK), BF16).at[:C_GATE_RANK].set(gla_alpha_w[l].astype(BF16))
        o_c = _gla(nat, g_all, aw, gla_alpha_b[l].reshape(1, -1), gla_head_norm[l].reshape(1, -1),
                   batch=batch, seq=seq)

        x2, xnt = _merge(o_a, o_b, lse_b, o_c, x2, norm_mix[l], w_gates, w_branch[l].astype(BF16),
                         w_out[l].astype(BF16), norm_ffn[l])

        wq_t = peer_wq[l].reshape(d, PEER_HEADS, 2, PEER_DKEY // 2).transpose(2, 1, 3, 0)
        wq_t = wq_t.reshape(2, PEER_HEADS * (PEER_DKEY // 2), d).astype(BF16)
        cnt, e1, rank, e2 = _peer_select(xnt, wq_t, _peer_key_matrix(peer_keys[l]))
        last = l == depth - 1
        x2 = _peer_dense(xnt, peer_u[l].astype(BF16), peer_v[l].T.astype(BF16), cnt, e1, rank, e2,
                         x2, norm_final, final_norm=last)
    return x2.reshape(batch, seq, d)
```

```python
import functools
import math

import numpy as np
import jax
import jax.numpy as jnp
from jax import lax
from jax.experimental import pallas as pl
from jax.experimental.pallas import tpu as pltpu

F32 = jnp.float32
BF16 = jnp.bfloat16

D_MODEL = 1024
HEAD_DIM = 64
BLOCK = 128
A_Q_HEADS = 8
A_KV_HEADS = 2
A_WINDOW = 128
B_PATTERNS = ((128, 1), (512, 4), (2048, 16))
B_HEADS = 8
C_HEADS = 4
C_DK = 64
C_DV = 128
C_GATE_RANK = 16
C_GATE_TAU = 16.0
C_CHUNK = 64
BRANCH_WIDTH = 512
N_BRANCHES = 3
N_BUCKETS = 32
BUCKET_MAX_DIST = 2048
PEER_HEADS = 8
PEER_KEYS = 128
PEER_N_EXPERTS = PEER_KEYS * PEER_KEYS
PEER_TOPK = 16
PEER_DKEY = 256
RMS_EPS = 1e-6
NEG_INF = -1e30

LANES = 128
VMEM_LIMIT = 56 * 1024 * 1024

GROUP_WIDTH = 3 * B_HEADS * HEAD_DIM
N_B0_COL = 0
N_CV_COL = GROUP_WIDTH
N_A_COL = N_CV_COL + C_HEADS * C_DV
N_CQ_COL = N_A_COL + (A_Q_HEADS + 2 * A_KV_HEADS) * HEAD_DIM
N_CK_COL = N_CQ_COL + C_HEADS * C_DK
N_WIDTH = N_CK_COL + C_HEADS * C_DK
G_GATES = N_BRANCHES * D_MODEL
G_WIDTH = C_HEADS * C_DV + LANES
G_R_COL = 0
G_ALR_COL = C_HEADS * C_DV
GLA_CHUNKS_PER_STEP = 8
ATTN_ROWS_PER_STEP = 2 * BLOCK
LSE_LANES = LANES // B_HEADS


def _cparams(sem):
    return pltpu.CompilerParams(dimension_semantics=sem, vmem_limit_bytes=VMEM_LIMIT)


def _norm_matmul_kernel(x_ref, g_ref, *refs):
    n_out = len(refs) // 2
    x = x_ref[...]
    ms = jnp.mean(x * x, axis=-1, keepdims=True)
    h = (x * lax.rsqrt(ms + RMS_EPS) * g_ref[...]).astype(BF16)
    for w_ref, o_ref in zip(refs[:n_out], refs[n_out:2 * n_out]):
        o_ref[...] = jnp.dot(h, w_ref[...], preferred_element_type=F32).astype(o_ref.dtype)
    refs[2 * n_out][...] = h


def _norm_matmul(x2, g, ws, out_dtypes, name):
    rows, d = x2.shape
    tm = min(512, rows)
    row_spec = lambda n: pl.BlockSpec((tm, n), lambda i: (i, 0))
    return pl.pallas_call(
        _norm_matmul_kernel,
        grid=(rows // tm,),
        in_specs=[row_spec(d), pl.BlockSpec((1, d), lambda i: (0, 0))]
        + [pl.BlockSpec(w.shape, lambda i: (0, 0)) for w in ws],
        out_specs=[row_spec(w.shape[1]) for w in ws] + [row_spec(d)],
        out_shape=[jax.ShapeDtypeStruct((rows, w.shape[1]), dt) for w, dt in zip(ws, out_dtypes)]
        + [jax.ShapeDtypeStruct((rows, d), BF16)],
        compiler_params=_cparams(("parallel",)),
        name=name,
    )(x2, g.reshape(1, d), *ws)


def _matmul_kernel(h_ref, w_ref, o_ref):
    o_ref[...] = jnp.dot(h_ref[...], w_ref[...], preferred_element_type=F32).astype(o_ref.dtype)


def _matmul(h, w, out_dtype, name):
    rows, d = h.shape
    tm = min(1024, rows)
    return pl.pallas_call(
        _matmul_kernel,
        grid=(rows // tm,),
        in_specs=[pl.BlockSpec((tm, d), lambda i: (i, 0)), pl.BlockSpec(w.shape, lambda i: (0, 0))],
        out_specs=pl.BlockSpec((tm, w.shape[1]), lambda i: (i, 0)),
        out_shape=jax.ShapeDtypeStruct((rows, w.shape[1]), out_dtype),
        compiler_params=_cparams(("parallel",)),
        name=name,
    )(h, w)


def _attn_kernel(*refs, hq, grp, has_sink, want_lse, nsb, nseq):
    q_ref, kp_ref, kc_ref, vp_ref, vc_ref, bias_ref = refs[:6]
    pos = 6
    sink_ref = None
    if has_sink:
        sink_ref = refs[pos]
        pos += 1
    o_ref = refs[pos]
    lse_ref = refs[pos + 1] if want_lse else None

    first = jnp.minimum(pl.program_id(2), 1)
    qw = hq * HEAD_DIM
    tiles = [(sq, sb, h) for sq in range(nseq) for sb in range(nsb) for h in range(hq)]
    kv = {}
    for sq in range(nseq):
        k = jnp.concatenate([kp_ref[sq], kc_ref[sq]], axis=0)
        v = jnp.concatenate([vp_ref[sq], vc_ref[sq]], axis=0)
        for sb in range(nsb):
            kv[sq, sb] = (k[sb * BLOCK:(sb + 2) * BLOCK], v[sb * BLOCK:(sb + 2) * BLOCK])
    scores = {}
    for sq, sb, h in tiles:
        kvh = h // grp
        q = q_ref[sq, sb * BLOCK:(sb + 1) * BLOCK, h * HEAD_DIM:(h + 1) * HEAD_DIM] * (HEAD_DIM ** -0.5)
        kh = kv[sq, sb][0][:, kvh * HEAD_DIM:(kvh + 1) * HEAD_DIM]
        s = lax.dot_general(q, kh, (((1,), (1,)), ((), ())), preferred_element_type=F32)
        bias = bias_ref[first, h] if sb == 0 else bias_ref[1, h]
        scores[sq, sb, h] = s + bias
    probs = {}
    for t in tiles:
        h = t[2]
        s = scores[t]
        m = jnp.max(s, axis=-1, keepdims=True)
        if has_sink:
            m = jnp.maximum(m, sink_ref[h])
        p = jnp.exp(s - m)
        denom = jnp.sum(p, axis=-1, keepdims=True)
        if has_sink:
            denom = denom + jnp.exp(sink_ref[h] - m)
        probs[t] = (p.astype(BF16), denom, m)
    for sq in range(nseq):
        for sb in range(nsb):
            outs, lses = [], []
            for h in range(hq):
                p, denom, m = probs[sq, sb, h]
                kvh = h // grp
                vh = kv[sq, sb][1][:, kvh * HEAD_DIM:(kvh + 1) * HEAD_DIM]
                outs.append(jnp.dot(p, vh, preferred_element_type=F32) / denom)
                if want_lse:
                    lses.append(jnp.broadcast_to(m + jnp.log(denom), (BLOCK, LSE_LANES)))
            rs = slice(sb * BLOCK, (sb + 1) * BLOCK)
            o_ref[0, rs, sq * qw:(sq + 1) * qw] = jnp.concatenate(outs, axis=-1).astype(o_ref.dtype)
            if want_lse:
                lse_ref[0, rs, sq * LANES:(sq + 1) * LANES] = jnp.concatenate(lses, axis=-1)


def _banded_bias(bias, max_dist):
    qi = np.arange(BLOCK)[:, None]
    kj = np.arange(2 * BLOCK)[None, :]
    dist = qi + BLOCK - kj
    in_window = (dist >= 0) & (dist <= max_dist)
    first = in_window & (kj >= BLOCK)
    return jnp.stack([jnp.where(first, bias, NEG_INF), jnp.where(in_window, bias, NEG_INF)])


def _banded_attention(qkv, bias, sink, *, batch, seq, dil, width, q_col, k_col, v_col,
                      hq, grp, want_lse, out_dtype, name):
    sub = seq // dil
    rows = min(ATTN_ROWS_PER_STEP, sub)
    nsb = rows // BLOCK
    nseq = min(ATTN_ROWS_PER_STEP // rows, dil)
    qw = hq * HEAD_DIM
    kw = (hq // grp) * HEAD_DIM
    view = qkv.reshape(batch * dil, sub, width)
    q_blk, k_blk, v_blk = q_col // qw, k_col // kw, v_col // kw
    assert q_col % qw == 0 and k_col % kw == 0 and v_col % kw == 0
    groups = dil // nseq

    prev_spec = lambda c: pl.BlockSpec((nseq, BLOCK, kw),
                                       lambda b, r, i: (b * groups + r, jnp.maximum(i * nsb - 1, 0), c))
    cur_spec = lambda w, c: pl.BlockSpec((nseq, rows, w), lambda b, r, i: (b * groups + r, i, c))
    in_specs = [
        cur_spec(qw, q_blk), prev_spec(k_blk), cur_spec(kw, k_blk), prev_spec(v_blk), cur_spec(kw, v_blk),
        pl.BlockSpec((2, hq, BLOCK, 2 * BLOCK), lambda b, r, i: (0, 0, 0, 0)),
    ]
    args = [view, view, view, view, view, bias]
    if sink is not None:
        in_specs.append(pl.BlockSpec(memory_space=pltpu.SMEM))
        args.append(sink)
    o_spec = pl.BlockSpec((1, rows, nseq * qw), lambda b, r, i: (b, i, r))
    o_shape = jax.ShapeDtypeStruct((batch, sub, dil * qw), out_dtype)
    out_specs, out_shape = o_spec, o_shape
    if want_lse:
        out_specs = [o_spec, pl.BlockSpec((1, rows, nseq * LANES), lambda b, r, i: (b, i, r))]
        out_shape = [o_shape, jax.ShapeDtypeStruct((batch, sub, dil * LANES), F32)]
    res = pl.pallas_call(
        functools.partial(_attn_kernel, hq=hq, grp=grp, has_sink=sink is not None, want_lse=want_lse, nsb=nsb,
                          nseq=nseq),
        grid=(batch, groups, sub // rows),
        in_specs=in_specs,
        out_specs=out_specs,
        out_shape=out_shape,
        compiler_params=_cparams(("parallel", "parallel", "arbitrary")),
        name=name,
    )(*args)
    if want_lse:
        return res[0].reshape(batch * seq, qw), res[1].reshape(batch * seq, LANES)
    return res.reshape(batch * seq, qw)


def _gla_kernel(q_ref, k_ref, v_ref, alr_ref, r_ref, aw_ref, ab_ref, hn_ref, o_ref, st_ref):
    @pl.when(pl.program_id(1) == 0)
    def _():
        st_ref[...] = jnp.zeros_like(st_ref)

    ck = C_CHUNK
    z = jnp.dot(alr_ref[0].astype(BF16), aw_ref[...], preferred_element_type=F32) + ab_ref[...]
    log_a = (jnp.minimum(z, 0.0) - jnp.log1p(jnp.exp(-jnp.abs(z)))) * (1.0 / C_GATE_TAU)
    row = lax.broadcasted_iota(jnp.int32, log_a.shape, 0) & (ck - 1)
    cum = log_a
    shift = 1
    while shift < ck:
        cum = cum + jnp.where(row >= shift, pltpu.roll(cum, shift, 0), 0.0)
        shift *= 2
    q_all = q_ref[0].astype(F32) * (C_DK ** -0.5)
    k_all = k_ref[0].astype(F32)
    q_dec_all = (q_all * jnp.exp(cum)).astype(BF16)
    k_inv_all = (k_all * jnp.exp(-cum)).astype(BF16)
    ti = lax.broadcasted_iota(jnp.int32, (ck, ck), 0)
    si = lax.broadcasted_iota(jnp.int32, (ck, ck), 1)
    causal = ti >= si
    nt = (((1,), (1,)), ((), ()))
    tn = (((0,), (0,)), ((), ()))
    nchunks = GLA_CHUNKS_PER_STEP
    heads = range(C_HEADS)
    ksl = [slice(h * C_DK, (h + 1) * C_DK) for h in heads]
    vsl = [slice(h * C_DV, (h + 1) * C_DV) for h in heads]
    att, kv_t, decays = {}, {}, []
    for c in range(nchunks):
        rs = slice(c * ck, (c + 1) * ck)
        cum_c = cum[rs]
        last = cum_c[ck - 1:ck, :]
        k_end = (k_all[rs] * jnp.exp(last - cum_c)).astype(BF16)
        decays.append(jnp.exp(last))
        v = v_ref[0, rs, :]
        for h in heads:
            qd, ki = q_dec_all[rs, ksl[h]], k_inv_all[rs, ksl[h]]
            att[c, h] = jnp.where(causal, lax.dot_general(qd, ki, nt, preferred_element_type=F32), 0.0)
            kv_t[c, h] = lax.dot_general(v[:, vsl[h]], k_end[:, ksl[h]], tn, preferred_element_type=F32)
    o_intra = {}
    for c in range(nchunks):
        rs = slice(c * ck, (c + 1) * ck)
        v = v_ref[0, rs, :]
        for h in heads:
            o_intra[c, h] = jnp.dot(att[c, h].astype(BF16), v[:, vsl[h]], preferred_element_type=F32)
    st = [st_ref[h] for h in heads]
    for c in range(nchunks):
        rs = slice(c * ck, (c + 1) * ck)
        r = r_ref[0, rs, :]
        outs = []
        for h in heads:
            o = o_intra[c, h] + lax.dot_general(q_dec_all[rs, ksl[h]], st[h].astype(BF16), nt,
                                                preferred_element_type=F32)
            st[h] = st[h] * decays[c][:, ksl[h]] + kv_t[c, h]
            o = o * lax.rsqrt(jnp.mean(o * o, axis=-1, keepdims=True) + RMS_EPS) * hn_ref[...]
            rh = r[:, vsl[h]]
            outs.append(o * (rh * jax.nn.sigmoid(rh)))
        o_ref[0, rs, :] = jnp.concatenate(outs, axis=-1).astype(o_ref.dtype)
    for h in heads:
        st_ref[h] = st[h]


def _gla(c_qkv, g_all, aw, ab, hn, *, batch, seq):
    rows = C_CHUNK * GLA_CHUNKS_PER_STEP
    qk_w = C_HEADS * C_DK
    v_w = C_HEADS * C_DV
    cv = c_qkv.reshape(batch, seq, N_WIDTH)
    gv = g_all.reshape(batch, seq, G_WIDTH)
    return pl.pallas_call(
        _gla_kernel,
        grid=(batch, seq // rows),
        in_specs=[
            pl.BlockSpec((1, rows, qk_w), lambda b, c: (b, c, N_CQ_COL // qk_w)),
            pl.BlockSpec((1, rows, qk_w), lambda b, c: (b, c, N_CK_COL // qk_w)),
            pl.BlockSpec((1, rows, v_w), lambda b, c: (b, c, N_CV_COL // v_w)),
            pl.BlockSpec((1, rows, LANES), lambda b, c: (b, c, G_ALR_COL // LANES)),
            pl.BlockSpec((1, rows, v_w), lambda b, c: (b, c, G_R_COL // v_w)),
            pl.BlockSpec((LANES, qk_w), lambda b, c: (0, 0)),
            pl.BlockSpec((1, qk_w), lambda b, c: (0, 0)),
            pl.BlockSpec((1, C_DV), lambda b, c: (0, 0)),
        ],
        out_specs=pl.BlockSpec((1, rows, v_w), lambda b, c: (b, c, 0)),
        out_shape=jax.ShapeDtypeStruct((batch, seq, v_w), BF16),
        scratch_shapes=[pltpu.VMEM((C_HEADS, C_DV, C_DK), F32)],
        compiler_params=_cparams(("parallel", "arbitrary")),
        name="gla",
    )(cv, cv, cv, gv, gv, aw, ab, hn).reshape(batch * seq, v_w)


def _merge_kernel(oa_ref, ob1_ref, ob2_ref, ob3_ref, l1_ref, l2_ref, l3_ref, oc_ref, x_ref,
                  gm_ref, ex_ref, wg_ref, wb_ref, wo_ref, gn_ref, xo_ref, xnt_ref):
    l1, l2, l3 = l1_ref[...], l2_ref[...], l3_ref[...]
    lm = jnp.maximum(jnp.maximum(l1, l2), l3)
    e1, e2, e3 = jnp.exp(l1 - lm), jnp.exp(l2 - lm), jnp.exp(l3 - lm)
    inv = 1.0 / (e1 + e2 + e3)

    def widen(wc):
        hi = wc.astype(BF16)
        r1 = wc - hi.astype(F32)
        mid = r1.astype(BF16)
        lo = (r1 - mid.astype(F32)).astype(BF16)
        ex = ex_ref[...]
        return (jnp.dot(hi, ex, preferred_element_type=F32) + jnp.dot(mid, ex, preferred_element_type=F32)
                + jnp.dot(lo, ex, preferred_element_type=F32))

    ob = (widen(e1 * inv) * ob1_ref[...].astype(F32) + widen(e2 * inv) * ob2_ref[...].astype(F32)
          + widen(e3 * inv) * ob3_ref[...].astype(F32))
    branches = (oa_ref[...], ob.astype(BF16), oc_ref[...])
    x = x_ref[...]
    h = (x * lax.rsqrt(jnp.mean(x * x, axis=-1, keepdims=True) + RMS_EPS) * gm_ref[...]).astype(BF16)
    merged = None
    for n in range(N_BRANCHES):
        gate = jnp.dot(h, wg_ref[:, n * D_MODEL:(n + 1) * D_MODEL], preferred_element_type=F32)
        proj = jnp.dot(branches[n], wb_ref[n], preferred_element_type=F32)
        term = jax.nn.sigmoid(gate) * proj
        merged = term if merged is None else merged + term
    x = x + jnp.dot(merged.astype(BF16), wo_ref[...], preferred_element_type=F32)
    xo_ref[...] = x
    xn = x * lax.rsqrt(jnp.mean(x * x, axis=-1, keepdims=True) + RMS_EPS) * gn_ref[...]
    xnt_ref[...] = xn.T.astype(BF16)


def _head_expander():
    src = np.arange(LANES)[:, None]
    dst = np.arange(BRANCH_WIDTH)[None, :]
    return jnp.asarray(src == (dst // HEAD_DIM) * LSE_LANES, dtype=BF16)


def _merge(o_a, o_b, lse_b, o_c, x2, gm, wg, wb, wo, gn):
    rows = x2.shape[0]
    tm = min(512, rows)
    row_spec = lambda w: pl.BlockSpec((tm, w), lambda i: (i, 0))
    vec_spec = pl.BlockSpec((1, D_MODEL), lambda i: (0, 0))
    return pl.pallas_call(
        _merge_kernel,
        grid=(rows // tm,),
        in_specs=[row_spec(BRANCH_WIDTH)] * 4 + [row_spec(LANES)] * 3 + [
            row_spec(BRANCH_WIDTH),
            row_spec(D_MODEL),
            vec_spec,
            pl.BlockSpec((LANES, BRANCH_WIDTH), lambda i: (0, 0)),
            pl.BlockSpec((D_MODEL, G_GATES), lambda i: (0, 0)),
            pl.BlockSpec((N_BRANCHES, BRANCH_WIDTH, D_MODEL), lambda i: (0, 0, 0)),
            pl.BlockSpec((D_MODEL, D_MODEL), lambda i: (0, 0)),
            vec_spec,
        ],
        out_specs=[row_spec(D_MODEL), pl.BlockSpec((D_MODEL, tm), lambda i: (0, i))],
        out_shape=[jax.ShapeDtypeStruct((rows, D_MODEL), F32),
                   jax.ShapeDtypeStruct((D_MODEL, rows), BF16)],
        compiler_params=_cparams(("parallel",)),
        name="merge",
    )(o_a, o_b[0], o_b[1], o_b[2], lse_b[0], lse_b[1], lse_b[2], o_c, x2, gm.reshape(1, D_MODEL),
      _head_expander(), wg, wb, wo, gn.reshape(1, D_MODEL))


_PAIRS = tuple((i, j) for i in range(PEER_TOPK) for j in range(PEER_TOPK)
               if (i + 1) * (j + 1) <= PEER_TOPK)


def _sort16_desc(v):
    v = list(v)
    n = len(v)
    k = 2
    while k <= n:
        j = k // 2
        while j >= 1:
            for i in range(n):
                l = i ^ j
                if l > i:
                    hi, lo = jnp.maximum(v[i], v[l]), jnp.minimum(v[i], v[l])
                    v[i], v[l] = (hi, lo) if (i & k) == 0 else (lo, hi)
            j //= 2
        k *= 2
    return v


def _merge_top16(a, b):
    n = len(a)
    v = [jnp.maximum(a[i], b[n - 1 - i]) for i in range(n)]
    j = n // 2
    while j >= 1:
        for i in range(n):
            l = i ^ j
            if l > i:
                v[i], v[l] = jnp.maximum(v[i], v[l]), jnp.minimum(v[i], v[l])
        j //= 2
    return v


def _top16_of(vals):
    acc = None
    for g in range(len(vals) // PEER_TOPK):
        grp = _sort16_desc(vals[g * PEER_TOPK:(g + 1) * PEER_TOPK])
        acc = grp if acc is None else _merge_top16(acc, grp)
    return acc


def _peer_select_kernel(xnt_ref, wq_ref, kb_ref, cnt_ref, e1_ref, rank_ref, e2_ref,
                        sc_ref, rk_ref, eb_ref, *, tb):
    nk, nh, kk = PEER_KEYS, PEER_HEADS, PEER_TOPK
    xnt = xnt_ref[...]
    qys = [jnp.dot(wq_ref[p], xnt, preferred_element_type=F32).astype(BF16) for p in range(2)]
    for p in range(2):
        sc_ref[p] = jnp.dot(kb_ref[p], qys[p], preferred_element_type=F32).reshape(nk, nh, tb)

    for c in range(tb // LANES):
        cs = slice(c * LANES, (c + 1) * LANES)
        s1s = _top16_of([sc_ref[0, n, :, cs] for n in range(nk)])
        s2s = _top16_of([sc_ref[1, n, :, cs] for n in range(nk)])
        cands = [s1s[i] + s2s[j] for (i, j) in _PAIRS]
        pad = [jnp.full_like(cands[0], -jnp.inf)] * (-len(cands) % kk)
        tau = _top16_of(cands + pad)[kk - 1]
        top = cands[0]
        zsum = None
        counts = [None] * kk
        for (i, j), cd in zip(_PAIRS, cands):
            sel = cd >= tau
            term = jnp.where(sel, jnp.exp(cd - top), 0.0)
            zsum = term if zsum is None else zsum + term
            one = jnp.where(sel, 1.0, 0.0)
            counts[i] = one if counts[i] is None else counts[i] + one
        inv_z = 1.0 / zsum

        def first_half(n, carry):
            s1 = sc_ref[0, n, :, cs]
            cnt = jnp.zeros_like(s1)
            for i in range(kk):
                cnt = jnp.where(s1 == s1s[i], counts[i], cnt)
            r0 = pl.multiple_of(n * nh, nh)
            cnt_ref[pl.ds(r0, nh), cs] = cnt
            e1_ref[pl.ds(r0, nh), cs] = jnp.exp(s1 - s1s[0])
            return carry

        def second_half(n, carry):
            s2 = sc_ref[1, n, :, cs]
            rank = jnp.full_like(s2, float(kk))
            for j in range(kk):
                rank = jnp.where(s2 == s2s[j], float(j), rank)
            r0 = pl.multiple_of(n * nh, nh)
            rk_ref[pl.ds(r0, nh), :] = rank
            eb_ref[pl.ds(r0, nh), :] = jnp.exp(s2 - s2s[0]) * inv_z
            return carry

        lax.fori_loop(0, nk, first_half, 0, unroll=4)
        lax.fori_loop(0, nk, second_half, 0, unroll=4)
        for h in range(nh):
            rank_ref[h * nk:(h + 1) * nk, cs] = rk_ref[pl.ds(h, nk, stride=nh), :].astype(BF16)
            e2_ref[h * nk:(h + 1) * nk, cs] = eb_ref[pl.ds(h, nk, stride=nh), :].astype(BF16)


def _peer_select(xnt, wq_t, k_big):
    d, rows = xnt.shape
    tb = min(256, rows)
    nrow = PEER_KEYS * PEER_HEADS
    tab_spec = pl.BlockSpec((nrow, tb), lambda i: (0, i))
    return pl.pallas_call(
        functools.partial(_peer_select_kernel, tb=tb),
        grid=(rows // tb,),
        in_specs=[
            pl.BlockSpec((d, tb), lambda i: (0, i)),
            pl.BlockSpec((2, nrow, d), lambda i: (0, 0, 0)),
            pl.BlockSpec((2, nrow, nrow), lambda i: (0, 0, 0)),
        ],
        out_specs=[tab_spec] * 4,
        out_shape=[jax.ShapeDtypeStruct((nrow, rows), dt) for dt in (F32, F32, BF16, BF16)],
        scratch_shapes=[
            pltpu.VMEM((2, PEER_KEYS, PEER_HEADS, tb), F32),
            pltpu.VMEM((nrow, LANES), F32),
            pltpu.VMEM((nrow, LANES), F32),
        ],
        compiler_params=_cparams(("parallel",)),
        name="peer_select",
    )(xnt, wq_t, k_big)


def _gelu_tanh(x):
    c = math.sqrt(2.0 / math.pi)
    return 0.5 * x * (1.0 + jnp.tanh(c * (x + 0.044715 * (x * x * x))))


def _peer_dense_kernel(xnt_ref, u_ref, vt_ref, cnt_ref, e1_ref, rank_ref, e2_ref, x_ref, gn_ref,
                       o_ref, acc_ref, ht_ref, w_ref, *, tb, eb, slab, final_norm):
    j = pl.program_id(1)
    nk, nh = PEER_KEYS, PEER_HEADS
    pk = 16
    n_slab = eb // slab

    @pl.when(j == 0)
    def _():
        acc_ref[...] = jnp.zeros_like(acc_ref)

    def hidden(s):
        rows = slice(s * slab, (s + 1) * slab)
        ht_ref[rows, :] = jnp.dot(u_ref[rows, :], xnt_ref[...], preferred_element_type=F32)

    def gate(s):
        for al in range(s * slab // nk, (s + 1) * slab // nk):
            rows = slice(al * nk, (al + 1) * nk)
            for c in range(tb // LANES):
                cs = slice(c * LANES, (c + 1) * LANES)
                cnt8 = cnt_ref[al * nh:(al + 1) * nh, cs]
                e18 = e1_ref[al * nh:(al + 1) * nh, cs]
                act = _gelu_tanh(ht_ref[rows, cs].astype(BF16)).reshape(nk // pk, pk, LANES)
                g = jnp.zeros((nk // pk, pk, LANES), BF16)
                for h in range(nh):
                    cb = jnp.broadcast_to(cnt8[h:h + 1, :], (pk, LANES)).astype(BF16)
                    eb_ = jnp.broadcast_to(e18[h:h + 1, :], (pk, LANES)).astype(BF16)
                    rk = rank_ref[h * nk:(h + 1) * nk, cs].reshape(nk // pk, pk, LANES)
                    e2 = e2_ref[h * nk:(h + 1) * nk, cs].reshape(nk // pk, pk, LANES)
                    g = g + jnp.where(rk < cb[None], e2, jnp.zeros_like(e2)) * eb_[None]
                w_ref[rows, cs] = (act * g).reshape(nk, LANES)

    def project(s):
        rows = slice(s * slab, (s + 1) * slab)
        acc_ref[...] += jnp.dot(vt_ref[:, rows], w_ref[rows, :], preferred_element_type=F32)

    hidden(0)
    for s in range(n_slab):
        if s + 1 < n_slab:
            hidden(s + 1)
        gate(s)
        if s >= 1:
            project(s - 1)
    project(n_slab - 1)

    @pl.when(j == pl.num_programs(1) - 1)
    def _():
        x = x_ref[...] + acc_ref[...].T
        if final_norm:
            x = x * lax.rsqrt(jnp.mean(x * x, axis=-1, keepdims=True) + RMS_EPS) * gn_ref[...]
        o_ref[...] = x


def _peer_dense(xnt, u, vt, cnt, e1, rank, e2, x2, gn, final_norm):
    d, rows = xnt.shape
    ne = u.shape[0]
    tb = min(512, rows)
    eb = 2048
    slab = 1024
    nrow = PEER_KEYS * PEER_HEADS
    arow = (eb // PEER_KEYS) * PEER_HEADS
    return pl.pallas_call(
        functools.partial(_peer_dense_kernel, tb=tb, eb=eb, slab=slab, final_norm=final_norm),
        grid=(rows // tb, ne // eb),
        in_specs=[
            pl.BlockSpec((d, tb), lambda i, j: (0, i)),
            pl.BlockSpec((eb, d), lambda i, j: (j, 0)),
            pl.BlockSpec((d, eb), lambda i, j: (0, j)),
            pl.BlockSpec((arow, tb), lambda i, j: (j, i)),
            pl.BlockSpec((arow, tb), lambda i, j: (j, i)),
            pl.BlockSpec((nrow, tb), lambda i, j: (0, i)),
            pl.BlockSpec((nrow, tb), lambda i, j: (0, i)),
            pl.BlockSpec((tb, d), lambda i, j: (i, 0)),
            pl.BlockSpec((1, d), lambda i, j: (0, 0)),
        ],
        out_specs=pl.BlockSpec((tb, d), lambda i, j: (i, 0)),
        out_shape=jax.ShapeDtypeStruct((rows, d), F32),
        scratch_shapes=[
            pltpu.VMEM((d, tb), F32),
            pltpu.VMEM((eb, tb), F32),
            pltpu.VMEM((eb, tb), BF16),
        ],
        compiler_params=_cparams(("parallel", "arbitrary")),
        name="peer_dense",
    )(xnt, u, vt, cnt, e1, rank, e2, x2, gn.reshape(1, d))


def _t5_bucket(dist):
    max_exact = N_BUCKETS // 2
    large = max_exact + (jnp.log(jnp.maximum(dist, 1).astype(F32) / max_exact)
                         / math.log(BUCKET_MAX_DIST / max_exact) * (N_BUCKETS - max_exact)).astype(jnp.int32)
    large = jnp.minimum(large, N_BUCKETS - 1)
    return jnp.where(dist < max_exact, dist, large)


def _rel_bias(table_cols, dilation):
    qi = jnp.arange(BLOCK)[:, None]
    kj = jnp.arange(2 * BLOCK)[None, :]
    dist = jnp.maximum(qi + BLOCK - kj, 0) * dilation
    onehot = (_t5_bucket(dist)[..., None] == jnp.arange(N_BUCKETS)).astype(F32)
    return jnp.einsum('qkb,bh->hqk', onehot, table_cols.astype(F32), precision=lax.Precision.HIGHEST)


def _split_w_in(w):
    hd = HEAD_DIM
    sizes = [A_Q_HEADS * hd, A_KV_HEADS * hd, A_KV_HEADS * hd] + [B_HEADS * hd] * 9 + [
        C_HEADS * C_DK, C_HEADS * C_DK, C_HEADS * C_DV, C_GATE_RANK, C_HEADS * C_DV, N_BRANCHES * D_MODEL]
    offs = np.concatenate([[0], np.cumsum(sizes)])
    col = lambda a, b: w[:, int(offs[a]):int(offs[b])].astype(BF16)
    groups = [col(3 + 3 * g, 6 + 3 * g) for g in range(3)]
    w_nat = jnp.concatenate([groups[0], col(14, 15), col(0, 3), col(12, 14)], axis=1)
    pad = jnp.zeros((w.shape[0], G_WIDTH - G_ALR_COL - C_GATE_RANK), BF16)
    w_r = jnp.concatenate([col(16, 17), col(15, 16), pad], axis=1)
    return w_nat, w_r, groups, col(17, 18)


def _peer_key_matrix(keys):
    nh, _, nk, c = keys.shape
    eye = jnp.eye(nh, dtype=keys.dtype)
    big = jnp.einsum('hpnc,hg->pnhgc', keys, eye)
    return big.reshape(2, nk * nh, nh * c).astype(BF16)


def _residue_major(x2, batch, seq, dil):
    if dil == 1:
        return x2
    d = x2.shape[-1]
    return x2.reshape(batch, seq // dil, dil, d).transpose(0, 2, 1, 3).reshape(batch * seq, d)


def kernel(x, w_in, attn_sinks, gla_alpha_w, gla_alpha_b, gla_head_norm, w_branch, w_out, norm_mix,
           norm_ffn, peer_wq, peer_keys, peer_u, peer_v, rel_bias_table, norm_final):
    batch, seq, d = x.shape
    rows = batch * seq
    depth = w_in.shape[0]
    hw = B_HEADS * HEAD_DIM

    bias_a = _banded_bias(_rel_bias(rel_bias_table[:, :A_Q_HEADS], 1), A_WINDOW - 1)
    bias_b = [_banded_bias(
        _rel_bias(rel_bias_table[:, A_Q_HEADS + i * B_HEADS:A_Q_HEADS + (i + 1) * B_HEADS], dil), window // dil)
        for i, (window, dil) in enumerate(B_PATTERNS)]

    x2 = x.reshape(rows, d)
    for l in range(depth):
        w_nat, w_r, w_groups, w_gates = _split_w_in(w_in[l])
        nat, g_all, h_mix = _norm_matmul(x2, norm_mix[l], [w_nat, w_r], [BF16, F32], "proj_nat")

        o_a = _banded_attention(
            nat, bias_a, attn_sinks[l], batch=batch, seq=seq, dil=1, width=N_WIDTH,
            q_col=N_A_COL, k_col=N_A_COL + A_Q_HEADS * HEAD_DIM,
            v_col=N_A_COL + (A_Q_HEADS + A_KV_HEADS) * HEAD_DIM,
            hq=A_Q_HEADS, grp=A_Q_HEADS // A_KV_HEADS, want_lse=False, out_dtype=BF16, name="attn_a")
        o_b, lse_b = [], []
        for gi, (window, dil) in enumerate(B_PATTERNS):
            if dil == 1:
                b_qkv, width, col0 = nat, N_WIDTH, N_B0_COL
            else:
                b_qkv = _matmul(_residue_major(h_mix, batch, seq, dil), w_groups[gi], BF16, "proj_b%d" % gi)
                width, col0 = GROUP_WIDTH, 0
            o, lse = _banded_attention(
                b_qkv, bias_b[gi], None, batch=batch, seq=seq, dil=dil, width=width,
                q_col=col0, k_col=col0 + hw, v_col=col0 + 2 * hw, hq=B_HEADS, grp=1,
                want_lse=True, out_dtype=BF16, name="attn_b%d" % gi)
            o_b.append(o)
            lse_b.append(lse)

        aw = jnp.zeros((LANES, C_HEADS * C_DK), BF16).at[:C_GATE_RANK].set(gla_alpha_w[l].astype(BF16))
        o_c = _gla(nat, g_all, aw, gla_alpha_b[l].reshape(1, -1), gla_head_norm[l].reshape(1, -1),
                   batch=batch, seq=seq)

        x2, xnt = _merge(o_a, o_b, lse_b, o_c, x2, norm_mix[l], w_gates, w_branch[l].astype(BF16),
                         w_out[l].astype(BF16), norm_ffn[l])

        wq_t = peer_wq[l].reshape(d, PEER_HEADS, 2, PEER_DKEY // 2).transpose(2, 1, 3, 0)
        wq_t = wq_t.reshape(2, PEER_HEADS * (PEER_DKEY // 2), d).astype(BF16)
        cnt, e1, rank, e2 = _peer_select(xnt, wq_t, _peer_key_matrix(peer_keys[l]))
        last = l == depth - 1
        x2 = _peer_dense(xnt, peer_u[l].astype(BF16), peer_v[l].T.astype(BF16), cnt, e1, rank, e2,
                         x2, norm_final, final_norm=last)
    return x2.reshape(batch, seq, d)
```

```python
import functools
import math

import numpy as np
import jax
import jax.numpy as jnp
from jax import lax
from jax.experimental import pallas as pl
from jax.experimental.pallas import tpu as pltpu

F32 = jnp.float32
BF16 = jnp.bfloat16

D_MODEL = 1024
HEAD_DIM = 64
BLOCK = 128
A_Q_HEADS = 8
A_KV_HEADS = 2
A_WINDOW = 128
B_PATTERNS = ((128, 1), (512, 4), (2048, 16))
B_HEADS = 8
C_HEADS = 4
C_DK = 64
C_DV = 128
C_GATE_RANK = 16
C_GATE_TAU = 16.0
C_CHUNK = 64
BRANCH_WIDTH = 512
N_BRANCHES = 3
N_BUCKETS = 32
BUCKET_MAX_DIST = 2048
PEER_HEADS = 8
PEER_KEYS = 128
PEER_N_EXPERTS = PEER_KEYS * PEER_KEYS
PEER_TOPK = 16
PEER_DKEY = 256
RMS_EPS = 1e-6
NEG_INF = -1e30

LANES = 128
VMEM_LIMIT = 56 * 1024 * 1024

GROUP_WIDTH = 3 * B_HEADS * HEAD_DIM
N_B0_COL = 0
N_CV_COL = GROUP_WIDTH
N_A_COL = N_CV_COL + C_HEADS * C_DV
N_CQ_COL = N_A_COL + (A_Q_HEADS + 2 * A_KV_HEADS) * HEAD_DIM
N_CK_COL = N_CQ_COL + C_HEADS * C_DK
N_WIDTH = N_CK_COL + C_HEADS * C_DK
G_GATES = N_BRANCHES * D_MODEL
G_WIDTH = C_HEADS * C_DV + LANES
G_R_COL = 0
G_ALR_COL = C_HEADS * C_DV
GLA_CHUNKS_PER_STEP = 8
ATTN_ROWS_PER_STEP = 2 * BLOCK
LSE_LANES = LANES // B_HEADS


def _cparams(sem):
    return pltpu.CompilerParams(dimension_semantics=sem, vmem_limit_bytes=VMEM_LIMIT)


def _norm_matmul_kernel(x_ref, g_ref, *refs):
    n_out = len(refs) // 2
    x = x_ref[...]
    ms = jnp.mean(x * x, axis=-1, keepdims=True)
    h = (x * lax.rsqrt(ms + RMS_EPS) * g_ref[...]).astype(BF16)
    for w_ref, o_ref in zip(refs[:n_out], refs[n_out:2 * n_out]):
        o_ref[...] = jnp.dot(h, w_ref[...], preferred_element_type=F32).astype(o_ref.dtype)
    refs[2 * n_out][...] = h


def _norm_matmul(x2, g, ws, out_dtypes, name):
    rows, d = x2.shape
    tm = min(512, rows)
    row_spec = lambda n: pl.BlockSpec((tm, n), lambda i: (i, 0))
    return pl.pallas_call(
        _norm_matmul_kernel,
        grid=(rows // tm,),
        in_specs=[row_spec(d), pl.BlockSpec((1, d), lambda i: (0, 0))]
        + [pl.BlockSpec(w.shape, lambda i: (0, 0)) for w in ws],
        out_specs=[row_spec(w.shape[1]) for w in ws] + [row_spec(d)],
        out_shape=[jax.ShapeDtypeStruct((rows, w.shape[1]), dt) for w, dt in zip(ws, out_dtypes)]
        + [jax.ShapeDtypeStruct((rows, d), BF16)],
        compiler_params=_cparams(("parallel",)),
        name=name,
    )(x2, g.reshape(1, d), *ws)


def _matmul_kernel(h_ref, w_ref, o_ref):
    o_ref[...] = jnp.dot(h_ref[...], w_ref[...], preferred_element_type=F32).astype(o_ref.dtype)


def _matmul(h, w, out_dtype, name):
    rows, d = h.shape
    tm = min(1024, rows)
    return pl.pallas_call(
        _matmul_kernel,
        grid=(rows // tm,),
        in_specs=[pl.BlockSpec((tm, d), lambda i: (i, 0)), pl.BlockSpec(w.shape, lambda i: (0, 0))],
        out_specs=pl.BlockSpec((tm, w.shape[1]), lambda i: (i, 0)),
        out_shape=jax.ShapeDtypeStruct((rows, w.shape[1]), out_dtype),
        compiler_params=_cparams(("parallel",)),
        name=name,
    )(h, w)


def _attn_kernel(*refs, hq, grp, has_sink, want_lse, nsb, nseq):
    q_ref, kp_ref, kc_ref, vp_ref, vc_ref, bias_ref = refs[:6]
    pos = 6
    sink_ref = None
    if has_sink:
        sink_ref = refs[pos]
        pos += 1
    o_ref = refs[pos]
    lse_ref = refs[pos + 1] if want_lse else None

    first = jnp.minimum(pl.program_id(2), 1)
    qw = hq * HEAD_DIM
    tiles = [(sq, sb, h) for sq in range(nseq) for sb in range(nsb) for h in range(hq)]
    kv = {}
    for sq in range(nseq):
        k = jnp.concatenate([kp_ref[sq], kc_ref[sq]], axis=0)
        v = jnp.concatenate([vp_ref[sq], vc_ref[sq]], axis=0)
        for sb in range(nsb):
            kv[sq, sb] = (k[sb * BLOCK:(sb + 2) * BLOCK], v[sb * BLOCK:(sb + 2) * BLOCK])
    scores = {}
    for sq in range(nseq):
        for sb in range(nsb):
            for kvh in range(hq // grp):
                hs = range(kvh * grp, (kvh + 1) * grp)
                q = jnp.concatenate(
                    [q_ref[sq, sb * BLOCK:(sb + 1) * BLOCK, h * HEAD_DIM:(h + 1) * HEAD_DIM] for h in hs], axis=0)
                kh = kv[sq, sb][0][:, kvh * HEAD_DIM:(kvh + 1) * HEAD_DIM]
                s = lax.dot_general(q * (HEAD_DIM ** -0.5), kh, (((1,), (1,)), ((), ())),
                                    preferred_element_type=F32)
                for g, h in enumerate(hs):
                    bias = bias_ref[first, h] if sb == 0 else bias_ref[1, h]
                    scores[sq, sb, h] = s[g * BLOCK:(g + 1) * BLOCK] + bias
    probs = {}
    for t in tiles:
        h = t[2]
        s = scores[t]
        m = jnp.max(s, axis=-1, keepdims=True)
        if has_sink:
            m = jnp.maximum(m, sink_ref[h])
        p = jnp.exp(s - m)
        denom = jnp.sum(p, axis=-1, keepdims=True)
        if has_sink:
            denom = denom + jnp.exp(sink_ref[h] - m)
        probs[t] = (p.astype(BF16), denom, m)
    for sq in range(nseq):
        for sb in range(nsb):
            outs, lses = [], []
            for kvh in range(hq // grp):
                hs = range(kvh * grp, (kvh + 1) * grp)
                vh = kv[sq, sb][1][:, kvh * HEAD_DIM:(kvh + 1) * HEAD_DIM]
                pv = jnp.dot(jnp.concatenate([probs[sq, sb, h][0] for h in hs], axis=0), vh,
                             preferred_element_type=F32)
                for g, h in enumerate(hs):
                    _, denom, m = probs[sq, sb, h]
                    outs.append(pv[g * BLOCK:(g + 1) * BLOCK] / denom)
                    if want_lse:
                        lses.append(jnp.broadcast_to(m + jnp.log(denom), (BLOCK, LSE_LANES)))
            rs = slice(sb * BLOCK, (sb + 1) * BLOCK)
            o_ref[0, rs, sq * qw:(sq + 1) * qw] = jnp.concatenate(outs, axis=-1).astype(o_ref.dtype)
            if want_lse:
                lse_ref[0, rs, sq * LANES:(sq + 1) * LANES] = jnp.concatenate(lses, axis=-1)


def _banded_bias(bias, max_dist):
    qi = np.arange(BLOCK)[:, None]
    kj = np.arange(2 * BLOCK)[None, :]
    dist = qi + BLOCK - kj
    in_window = (dist >= 0) & (dist <= max_dist)
    first = in_window & (kj >= BLOCK)
    return jnp.stack([jnp.where(first, bias, NEG_INF), jnp.where(in_window, bias, NEG_INF)])


def _banded_attention(qkv, bias, sink, *, batch, seq, dil, width, q_col, k_col, v_col,
                      hq, grp, want_lse, out_dtype, name):
    sub = seq // dil
    rows = min(ATTN_ROWS_PER_STEP, sub)
    nsb = rows // BLOCK
    nseq = min(ATTN_ROWS_PER_STEP // rows, dil)
    qw = hq * HEAD_DIM
    kw = (hq // grp) * HEAD_DIM
    view = qkv.reshape(batch * dil, sub, width)
    q_blk, k_blk, v_blk = q_col // qw, k_col // kw, v_col // kw
    assert q_col % qw == 0 and k_col % kw == 0 and v_col % kw == 0
    groups = dil // nseq

    prev_spec = lambda c: pl.BlockSpec((nseq, BLOCK, kw),
                                       lambda b, r, i: (b * groups + r, jnp.maximum(i * nsb - 1, 0), c))
    cur_spec = lambda w, c: pl.BlockSpec((nseq, rows, w), lambda b, r, i: (b * groups + r, i, c))
    in_specs = [
        cur_spec(qw, q_blk), prev_spec(k_blk), cur_spec(kw, k_blk), prev_spec(v_blk), cur_spec(kw, v_blk),
        pl.BlockSpec((2, hq, BLOCK, 2 * BLOCK), lambda b, r, i: (0, 0, 0, 0)),
    ]
    args = [view, view, view, view, view, bias]
    if sink is not None:
        in_specs.append(pl.BlockSpec(memory_space=pltpu.SMEM))
        args.append(sink)
    o_spec = pl.BlockSpec((1, rows, nseq * qw), lambda b, r, i: (b, i, r))
    o_shape = jax.ShapeDtypeStruct((batch, sub, dil * qw), out_dtype)
    out_specs, out_shape = o_spec, o_shape
    if want_lse:
        out_specs = [o_spec, pl.BlockSpec((1, rows, nseq * LANES), lambda b, r, i: (b, i, r))]
        out_shape = [o_shape, jax.ShapeDtypeStruct((batch, sub, dil * LANES), F32)]
    res = pl.pallas_call(
        functools.partial(_attn_kernel, hq=hq, grp=grp, has_sink=sink is not None, want_lse=want_lse, nsb=nsb,
                          nseq=nseq),
        grid=(batch, groups, sub // rows),
        in_specs=in_specs,
        out_specs=out_specs,
        out_shape=out_shape,
        compiler_params=_cparams(("parallel", "parallel", "arbitrary")),
        name=name,
    )(*args)
    if want_lse:
        return res[0].reshape(batch * seq, qw), res[1].reshape(batch * seq, LANES)
    return res.reshape(batch * seq, qw)


def _gla_kernel(q_ref, k_ref, v_ref, alr_ref, r_ref, aw_ref, ab_ref, hn_ref, o_ref, st_ref):
    @pl.when(pl.program_id(1) == 0)
    def _():
        st_ref[...] = jnp.zeros_like(st_ref)

    ck = C_CHUNK
    z = jnp.dot(alr_ref[0].astype(BF16), aw_ref[...], preferred_element_type=F32) + ab_ref[...]
    log_a = (jnp.minimum(z, 0.0) - jnp.log1p(jnp.exp(-jnp.abs(z)))) * (1.0 / C_GATE_TAU)
    row = lax.broadcasted_iota(jnp.int32, log_a.shape, 0) & (ck - 1)
    cum = log_a
    shift = 1
    while shift < ck:
        cum = cum + jnp.where(row >= shift, pltpu.roll(cum, shift, 0), 0.0)
        shift *= 2
    q_all = q_ref[0].astype(F32) * (C_DK ** -0.5)
    k_all = k_ref[0].astype(F32)
    q_dec_all = (q_all * jnp.exp(cum)).astype(BF16)
    k_inv_all = (k_all * jnp.exp(-cum)).astype(BF16)
    ti = lax.broadcasted_iota(jnp.int32, (ck, ck), 0)
    si = lax.broadcasted_iota(jnp.int32, (ck, ck), 1)
    causal = ti >= si
    nt = (((1,), (1,)), ((), ()))
    tn = (((0,), (0,)), ((), ()))
    nchunks = GLA_CHUNKS_PER_STEP
    heads = range(C_HEADS)
    ksl = [slice(h * C_DK, (h + 1) * C_DK) for h in heads]
    vsl = [slice(h * C_DV, (h + 1) * C_DV) for h in heads]
    att, kv_t, decays = {}, {}, []
    for c in range(nchunks):
        rs = slice(c * ck, (c + 1) * ck)
        cum_c = cum[rs]
        last = cum_c[ck - 1:ck, :]
        k_end = (k_all[rs] * jnp.exp(last - cum_c)).astype(BF16)
        decays.append(jnp.exp(last))
        v = v_ref[0, rs, :]
        for h in heads:
            qd, ki = q_dec_all[rs, ksl[h]], k_inv_all[rs, ksl[h]]
            att[c, h] = jnp.where(causal, lax.dot_general(qd, ki, nt, preferred_element_type=F32), 0.0)
            kv_t[c, h] = lax.dot_general(v[:, vsl[h]], k_end[:, ksl[h]], tn, preferred_element_type=F32)
    o_intra = {}
    for c in range(nchunks):
        rs = slice(c * ck, (c + 1) * ck)
        v = v_ref[0, rs, :]
        for h in heads:
            o_intra[c, h] = jnp.dot(att[c, h].astype(BF16), v[:, vsl[h]], preferred_element_type=F32)
    st = [st_ref[h] for h in heads]
    for c in range(nchunks):
        rs = slice(c * ck, (c + 1) * ck)
        r = r_ref[0, rs, :]
        outs = []
        for h in heads:
            o = o_intra[c, h] + lax.dot_general(q_dec_all[rs, ksl[h]], st[h].astype(BF16), nt,
                                                preferred_element_type=F32)
            st[h] = st[h] * decays[c][:, ksl[h]] + kv_t[c, h]
            o = o * lax.rsqrt(jnp.mean(o * o, axis=-1, keepdims=True) + RMS_EPS) * hn_ref[...]
            rh = r[:, vsl[h]]
            outs.append(o * (rh * jax.nn.sigmoid(rh)))
        o_ref[0, rs, :] = jnp.concatenate(outs, axis=-1).astype(o_ref.dtype)
    for h in heads:
        st_ref[h] = st[h]


def _gla(c_qkv, g_all, aw, ab, hn, *, batch, seq):
    rows = C_CHUNK * GLA_CHUNKS_PER_STEP
    qk_w = C_HEADS * C_DK
    v_w = C_HEADS * C_DV
    cv = c_qkv.reshape(batch, seq, N_WIDTH)
    gv = g_all.reshape(batch, seq, G_WIDTH)
    return pl.pallas_call(
        _gla_kernel,
        grid=(batch, seq // rows),
        in_specs=[
            pl.BlockSpec((1, rows, qk_w), lambda b, c: (b, c, N_CQ_COL // qk_w)),
            pl.BlockSpec((1, rows, qk_w), lambda b, c: (b, c, N_CK_COL // qk_w)),
            pl.BlockSpec((1, rows, v_w), lambda b, c: (b, c, N_CV_COL // v_w)),
            pl.BlockSpec((1, rows, LANES), lambda b, c: (b, c, G_ALR_COL // LANES)),
            pl.BlockSpec((1, rows, v_w), lambda b, c: (b, c, G_R_COL // v_w)),
            pl.BlockSpec((LANES, qk_w), lambda b, c: (0, 0)),
            pl.BlockSpec((1, qk_w), lambda b, c: (0, 0)),
            pl.BlockSpec((1, C_DV), lambda b, c: (0, 0)),
        ],
        out_specs=pl.BlockSpec((1, rows, v_w), lambda b, c: (b, c, 0)),
        out_shape=jax.ShapeDtypeStruct((batch, seq, v_w), BF16),
        scratch_shapes=[pltpu.VMEM((C_HEADS, C_DV, C_DK), F32)],
        compiler_params=_cparams(("parallel", "arbitrary")),
        name="gla",
    )(cv, cv, cv, gv, gv, aw, ab, hn).reshape(batch * seq, v_w)


def _merge_kernel(oa_ref, ob1_ref, ob2_ref, ob3_ref, l1_ref, l2_ref, l3_ref, oc_ref, x_ref,
                  gm_ref, ex_ref, wg_ref, wb_ref, wo_ref, gn_ref, xo_ref, xnt_ref):
    l1, l2, l3 = l1_ref[...], l2_ref[...], l3_ref[...]
    lm = jnp.maximum(jnp.maximum(l1, l2), l3)
    e1, e2, e3 = jnp.exp(l1 - lm), jnp.exp(l2 - lm), jnp.exp(l3 - lm)
    inv = 1.0 / (e1 + e2 + e3)

    def widen(wc):
        hi = wc.astype(BF16)
        r1 = wc - hi.astype(F32)
        mid = r1.astype(BF16)
        lo = (r1 - mid.astype(F32)).astype(BF16)
        ex = ex_ref[...]
        return (jnp.dot(hi, ex, preferred_element_type=F32) + jnp.dot(mid, ex, preferred_element_type=F32)
                + jnp.dot(lo, ex, preferred_element_type=F32))

    ob = (widen(e1 * inv) * ob1_ref[...].astype(F32) + widen(e2 * inv) * ob2_ref[...].astype(F32)
          + widen(e3 * inv) * ob3_ref[...].astype(F32))
    branches = (oa_ref[...], ob.astype(BF16), oc_ref[...])
    x = x_ref[...]
    h = (x * lax.rsqrt(jnp.mean(x * x, axis=-1, keepdims=True) + RMS_EPS) * gm_ref[...]).astype(BF16)
    merged = None
    for n in range(N_BRANCHES):
        gate = jnp.dot(h, wg_ref[:, n * D_MODEL:(n + 1) * D_MODEL], preferred_element_type=F32)
        proj = jnp.dot(branches[n], wb_ref[n], preferred_element_type=F32)
        term = jax.nn.sigmoid(gate) * proj
        merged = term if merged is None else merged + term
    x = x + jnp.dot(merged.astype(BF16), wo_ref[...], preferred_element_type=F32)
    xo_ref[...] = x
    xn = x * lax.rsqrt(jnp.mean(x * x, axis=-1, keepdims=True) + RMS_EPS) * gn_ref[...]
    xnt_ref[...] = xn.T.astype(BF16)


def _head_expander():
    src = np.arange(LANES)[:, None]
    dst = np.arange(BRANCH_WIDTH)[None, :]
    return jnp.asarray(src == (dst // HEAD_DIM) * LSE_LANES, dtype=BF16)


def _merge(o_a, o_b, lse_b, o_c, x2, gm, wg, wb, wo, gn):
    rows = x2.shape[0]
    tm = min(512, rows)
    row_spec = lambda w: pl.BlockSpec((tm, w), lambda i: (i, 0))
    vec_spec = pl.BlockSpec((1, D_MODEL), lambda i: (0, 0))
    return pl.pallas_call(
        _merge_kernel,
        grid=(rows // tm,),
        in_specs=[row_spec(BRANCH_WIDTH)] * 4 + [row_spec(LANES)] * 3 + [
            row_spec(BRANCH_WIDTH),
            row_spec(D_MODEL),
            vec_spec,
            pl.BlockSpec((LANES, BRANCH_WIDTH), lambda i: (0, 0)),
            pl.BlockSpec((D_MODEL, G_GATES), lambda i: (0, 0)),
            pl.BlockSpec((N_BRANCHES, BRANCH_WIDTH, D_MODEL), lambda i: (0, 0, 0)),
            pl.BlockSpec((D_MODEL, D_MODEL), lambda i: (0, 0)),
            vec_spec,
        ],
        out_specs=[row_spec(D_MODEL), pl.BlockSpec((D_MODEL, tm), lambda i: (0, i))],
        out_shape=[jax.ShapeDtypeStruct((rows, D_MODEL), F32),
                   jax.ShapeDtypeStruct((D_MODEL, rows), BF16)],
        compiler_params=_cparams(("parallel",)),
        name="merge",
    )(o_a, o_b[0], o_b[1], o_b[2], lse_b[0], lse_b[1], lse_b[2], o_c, x2, gm.reshape(1, D_MODEL),
      _head_expander(), wg, wb, wo, gn.reshape(1, D_MODEL))


_PAIRS = tuple((i, j) for i in range(PEER_TOPK) for j in range(PEER_TOPK)
               if (i + 1) * (j + 1) <= PEER_TOPK)


def _sort16_desc(v):
    v = list(v)
    n = len(v)
    k = 2
    while k <= n:
        j = k // 2
        while j >= 1:
            for i in range(n):
                l = i ^ j
                if l > i:
                    hi, lo = jnp.maximum(v[i], v[l]), jnp.minimum(v[i], v[l])
                    v[i], v[l] = (hi, lo) if (i & k) == 0 else (lo, hi)
            j //= 2
        k *= 2
    return v


def _merge_top16(a, b):
    n = len(a)
    v = [jnp.maximum(a[i], b[n - 1 - i]) for i in range(n)]
    j = n // 2
    while j >= 1:
        for i in range(n):
            l = i ^ j
            if l > i:
                v[i], v[l] = jnp.maximum(v[i], v[l]), jnp.minimum(v[i], v[l])
        j //= 2
    return v


def _top16_of(vals):
    acc = None
    for g in range(len(vals) // PEER_TOPK):
        grp = _sort16_desc(vals[g * PEER_TOPK:(g + 1) * PEER_TOPK])
        acc = grp if acc is None else _merge_top16(acc, grp)
    return acc


def _peer_select_kernel(xnt_ref, wq_ref, kb_ref, cnt_ref, e1_ref, rank_ref, e2_ref,
                        sc_ref, rk_ref, eb_ref, *, tb):
    nk, nh, kk = PEER_KEYS, PEER_HEADS, PEER_TOPK
    xnt = xnt_ref[...]
    qys = [jnp.dot(wq_ref[p], xnt, preferred_element_type=F32).astype(BF16) for p in range(2)]
    for p in range(2):
        sc_ref[p] = jnp.dot(kb_ref[p], qys[p], preferred_element_type=F32).reshape(nk, nh, tb)

    for c in range(tb // LANES):
        cs = slice(c * LANES, (c + 1) * LANES)
        s1s = _top16_of([sc_ref[0, n, :, cs] for n in range(nk)])
        s2s = _top16_of([sc_ref[1, n, :, cs] for n in range(nk)])
        cands = [s1s[i] + s2s[j] for (i, j) in _PAIRS]
        pad = [jnp.full_like(cands[0], -jnp.inf)] * (-len(cands) % kk)
        tau = _top16_of(cands + pad)[kk - 1]
        top = cands[0]
        zsum = None
        counts = [None] * kk
        for (i, j), cd in zip(_PAIRS, cands):
            sel = cd >= tau
            term = jnp.where(sel, jnp.exp(cd - top), 0.0)
            zsum = term if zsum is None else zsum + term
            one = jnp.where(sel, 1.0, 0.0)
            counts[i] = one if counts[i] is None else counts[i] + one
        inv_z = 1.0 / zsum

        def first_half(n, carry):
            s1 = sc_ref[0, n, :, cs]
            cnt = jnp.zeros_like(s1)
            for i in range(kk):
                cnt = jnp.where(s1 == s1s[i], counts[i], cnt)
            r0 = pl.multiple_of(n * nh, nh)
            cnt_ref[pl.ds(r0, nh), cs] = cnt
            e1_ref[pl.ds(r0, nh), cs] = jnp.exp(s1 - s1s[0])
            return carry

        def second_half(n, carry):
            s2 = sc_ref[1, n, :, cs]
            rank = jnp.full_like(s2, float(kk))
            for j in range(kk):
                rank = jnp.where(s2 == s2s[j], float(j), rank)
            r0 = pl.multiple_of(n * nh, nh)
            rk_ref[pl.ds(r0, nh), :] = rank
            eb_ref[pl.ds(r0, nh), :] = jnp.exp(s2 - s2s[0]) * inv_z
            return carry

        lax.fori_loop(0, nk, first_half, 0, unroll=4)
        lax.fori_loop(0, nk, second_half, 0, unroll=4)
        for h in range(nh):
            rank_ref[h * nk:(h + 1) * nk, cs] = rk_ref[pl.ds(h, nk, stride=nh), :].astype(BF16)
            e2_ref[h * nk:(h + 1) * nk, cs] = eb_ref[pl.ds(h, nk, stride=nh), :].astype(BF16)


def _peer_select(xnt, wq_t, k_big):
    d, rows = xnt.shape
    tb = min(256, rows)
    nrow = PEER_KEYS * PEER_HEADS
    tab_spec = pl.BlockSpec((nrow, tb), lambda i: (0, i))
    return pl.pallas_call(
        functools.partial(_peer_select_kernel, tb=tb),
        grid=(rows // tb,),
        in_specs=[
            pl.BlockSpec((d, tb), lambda i: (0, i)),
            pl.BlockSpec((2, nrow, d), lambda i: (0, 0, 0)),
            pl.BlockSpec((2, nrow, nrow), lambda i: (0, 0, 0)),
        ],
        out_specs=[tab_spec] * 4,
        out_shape=[jax.ShapeDtypeStruct((nrow, rows), dt) for dt in (F32, F32, BF16, BF16)],
        scratch_shapes=[
            pltpu.VMEM((2, PEER_KEYS, PEER_HEADS, tb), F32),
            pltpu.VMEM((nrow, LANES), F32),
            pltpu.VMEM((nrow, LANES), F32),
        ],
        compiler_params=_cparams(("parallel",)),
        name="peer_select",
    )(xnt, wq_t, k_big)


def _gelu_tanh(x):
    c = math.sqrt(2.0 / math.pi)
    return 0.5 * x * (1.0 + jnp.tanh(c * (x + 0.044715 * (x * x * x))))


def _peer_dense_kernel(xnt_ref, u_ref, vt_ref, cnt_ref, e1_ref, rank_ref, e2_ref, x_ref, gn_ref,
                       o_ref, acc_ref, ht_ref, w_ref, *, tb, eb, slab, final_norm):
    j = pl.program_id(1)
    nk, nh = PEER_KEYS, PEER_HEADS
    pk = 16
    n_slab = eb // slab

    @pl.when(j == 0)
    def _():
        acc_ref[...] = jnp.zeros_like(acc_ref)

    def hidden(s):
        rows = slice(s * slab, (s + 1) * slab)
        ht_ref[rows, :] = jnp.dot(u_ref[rows, :], xnt_ref[...], preferred_element_type=F32)

    def gate(s):
        for al in range(s * slab // nk, (s + 1) * slab // nk):
            rows = slice(al * nk, (al + 1) * nk)
            for c in range(tb // LANES):
                cs = slice(c * LANES, (c + 1) * LANES)
                cnt8 = cnt_ref[al * nh:(al + 1) * nh, cs]
                e18 = e1_ref[al * nh:(al + 1) * nh, cs]
                act = _gelu_tanh(ht_ref[rows, cs].astype(BF16)).reshape(nk // pk, pk, LANES)
                g = jnp.zeros((nk // pk, pk, LANES), BF16)
                for h in range(nh):
                    cb = jnp.broadcast_to(cnt8[h:h + 1, :], (pk, LANES)).astype(BF16)
                    eb_ = jnp.broadcast_to(e18[h:h + 1, :], (pk, LANES)).astype(BF16)
                    rk = rank_ref[h * nk:(h + 1) * nk, cs].reshape(nk // pk, pk, LANES)
                    e2 = e2_ref[h * nk:(h + 1) * nk, cs].reshape(nk // pk, pk, LANES)
                    g = g + jnp.where(rk < cb[None], e2, jnp.zeros_like(e2)) * eb_[None]
                w_ref[rows, cs] = (act * g).reshape(nk, LANES)

    def project(s):
        rows = slice(s * slab, (s + 1) * slab)
        acc_ref[...] += jnp.dot(vt_ref[:, rows], w_ref[rows, :], preferred_element_type=F32)

    hidden(0)
    for s in range(n_slab):
        if s + 1 < n_slab:
            hidden(s + 1)
        gate(s)
        if s >= 1:
            project(s - 1)
    project(n_slab - 1)

    @pl.when(j == pl.num_programs(1) - 1)
    def _():
        x = x_ref[...] + acc_ref[...].T
        if final_norm:
            x = x * lax.rsqrt(jnp.mean(x * x, axis=-1, keepdims=True) + RMS_EPS) * gn_ref[...]
        o_ref[...] = x


def _peer_dense(xnt, u, vt, cnt, e1, rank, e2, x2, gn, final_norm):
    d, rows = xnt.shape
    ne = u.shape[0]
    tb = min(512, rows)
    eb = 2048
    slab = 1024
    nrow = PEER_KEYS * PEER_HEADS
    arow = (eb // PEER_KEYS) * PEER_HEADS
    return pl.pallas_call(
        functools.partial(_peer_dense_kernel, tb=tb, eb=eb, slab=slab, final_norm=final_norm),
        grid=(rows // tb, ne // eb),
        in_specs=[
            pl.BlockSpec((d, tb), lambda i, j: (0, i)),
            pl.BlockSpec((eb, d), lambda i, j: (j, 0)),
            pl.BlockSpec((d, eb), lambda i, j: (0, j)),
            pl.BlockSpec((arow, tb), lambda i, j: (j, i)),
            pl.BlockSpec((arow, tb), lambda i, j: (j, i)),
            pl.BlockSpec((nrow, tb), lambda i, j: (0, i)),
            pl.BlockSpec((nrow, tb), lambda i, j: (0, i)),
            pl.BlockSpec((tb, d), lambda i, j: (i, 0)),
            pl.BlockSpec((1, d), lambda i, j: (0, 0)),
        ],
        out_specs=pl.BlockSpec((tb, d), lambda i, j: (i, 0)),
        out_shape=jax.ShapeDtypeStruct((rows, d), F32),
        scratch_shapes=[
            pltpu.VMEM((d, tb), F32),
            pltpu.VMEM((eb, tb), F32),
            pltpu.VMEM((eb, tb), BF16),
        ],
        compiler_params=_cparams(("parallel", "arbitrary")),
        name="peer_dense",
    )(xnt, u, vt, cnt, e1, rank, e2, x2, gn.reshape(1, d))


def _t5_bucket(dist):
    max_exact = N_BUCKETS // 2
    large = max_exact + (jnp.log(jnp.maximum(dist, 1).astype(F32) / max_exact)
                         / math.log(BUCKET_MAX_DIST / max_exact) * (N_BUCKETS - max_exact)).astype(jnp.int32)
    large = jnp.minimum(large, N_BUCKETS - 1)
    return jnp.where(dist < max_exact, dist, large)


def _rel_bias(table_cols, dilation):
    qi = jnp.arange(BLOCK)[:, None]
    kj = jnp.arange(2 * BLOCK)[None, :]
    dist = jnp.maximum(qi + BLOCK - kj, 0) * dilation
    onehot = (_t5_bucket(dist)[..., None] == jnp.arange(N_BUCKETS)).astype(F32)
    return jnp.einsum('qkb,bh->hqk', onehot, table_cols.astype(F32), precision=lax.Precision.HIGHEST)


def _split_w_in(w):
    hd = HEAD_DIM
    sizes = [A_Q_HEADS * hd, A_KV_HEADS * hd, A_KV_HEADS * hd] + [B_HEADS * hd] * 9 + [
        C_HEADS * C_DK, C_HEADS * C_DK, C_HEADS * C_DV, C_GATE_RANK, C_HEADS * C_DV, N_BRANCHES * D_MODEL]
    offs = np.concatenate([[0], np.cumsum(sizes)])
    col = lambda a, b: w[:, int(offs[a]):int(offs[b])].astype(BF16)
    groups = [col(3 + 3 * g, 6 + 3 * g) for g in range(3)]
    w_nat = jnp.concatenate([groups[0], col(14, 15), col(0, 3), col(12, 14)], axis=1)
    pad = jnp.zeros((w.shape[0], G_WIDTH - G_ALR_COL - C_GATE_RANK), BF16)
    w_r = jnp.concatenate([col(16, 17), col(15, 16), pad], axis=1)
    return w_nat, w_r, groups, col(17, 18)


def _peer_key_matrix(keys):
    nh, _, nk, c = keys.shape
    eye = jnp.eye(nh, dtype=keys.dtype)
    big = jnp.einsum('hpnc,hg->pnhgc', keys, eye)
    return big.reshape(2, nk * nh, nh * c).astype(BF16)


def _residue_major(x2, batch, seq, dil):
    if dil == 1:
        return x2
    d = x2.shape[-1]
    return x2.reshape(batch, seq // dil, dil, d).transpose(0, 2, 1, 3).reshape(batch * seq, d)


def kernel(x, w_in, attn_sinks, gla_alpha_w, gla_alpha_b, gla_head_norm, w_branch, w_out, norm_mix,
           norm_ffn, peer_wq, peer_keys, peer_u, peer_v, rel_bias_table, norm_final):
    batch, seq, d = x.shape
    rows = batch * seq
    depth = w_in.shape[0]
    hw = B_HEADS * HEAD_DIM

    bias_a = _banded_bias(_rel_bias(rel_bias_table[:, :A_Q_HEADS], 1), A_WINDOW - 1)
    bias_b = [_banded_bias(
        _rel_bias(rel_bias_table[:, A_Q_HEADS + i * B_HEADS:A_Q_HEADS + (i + 1) * B_HEADS], dil), window // dil)
        for i, (window, dil) in enumerate(B_PATTERNS)]

    x2 = x.reshape(rows, d)
    for l in range(depth):
        w_nat, w_r, w_groups, w_gates = _split_w_in(w_in[l])
        nat, g_all, h_mix = _norm_matmul(x2, norm_mix[l], [w_nat, w_r], [BF16, F32], "proj_nat")

        o_a = _banded_attention(
            nat, bias_a, attn_sinks[l], batch=batch, seq=seq, dil=1, width=N_WIDTH,
            q_col=N_A_COL, k_col=N_A_COL + A_Q_HEADS * HEAD_DIM,
            v_col=N_A_COL + (A_Q_HEADS + A_KV_HEADS) * HEAD_DIM,
            hq=A_Q_HEADS, grp=A_Q_HEADS // A_KV_HEADS, want_lse=False, out_dtype=BF16, name="attn_a")
        o_b, lse_b = [], []
        for gi, (window, dil) in enumerate(B_PATTERNS):
            if dil == 1:
                b_qkv, width, col0 = nat, N_WIDTH, N_B0_COL
            else:
                b_qkv = _matmul(_residue_major(h_mix, batch, seq, dil), w_groups[gi], BF16, "proj_b%d" % gi)
                width, col0 = GROUP_WIDTH, 0
            o, lse = _banded_attention(
                b_qkv, bias_b[gi], None, batch=batch, seq=seq, dil=dil, width=width,
                q_col=col0, k_col=col0 + hw, v_col=col0 + 2 * hw, hq=B_HEADS, grp=1,
                want_lse=True, out_dtype=BF16, name="attn_b%d" % gi)
            o_b.append(o)
            lse_b.append(lse)

        aw = jnp.zeros((LANES, C_HEADS * C_DK), BF16).at[:C_GATE_RANK].set(gla_alpha_w[l].astype(BF16))
        o_c = _gla(nat, g_all, aw, gla_alpha_b[l].reshape(1, -1), gla_head_norm[l].reshape(1, -1),
                   batch=batch, seq=seq)

        x2, xnt = _merge(o_a, o_b, lse_b, o_c, x2, norm_mix[l], w_gates, w_branch[l].astype(BF16),
                         w_out[l].astype(BF16), norm_ffn[l])

        wq_t = peer_wq[l].reshape(d, PEER_HEADS, 2, PEER_DKEY // 2).transpose(2, 1, 3, 0)
        wq_t = wq_t.reshape(2, PEER_HEADS * (PEER_DKEY // 2), d).astype(BF16)
        cnt, e1, rank, e2 = _peer_select(xnt, wq_t, _peer_key_matrix(peer_keys[l]))
        last = l == depth - 1
        x2 = _peer_dense(xnt, peer_u[l].astype(BF16), peer_v[l].T.astype(BF16), cnt, e1, rank, e2,
                         x2, norm_final, final_norm=last)
    return x2.reshape(batch, seq, d)
```
